```python
import math
import jax, jax.numpy as jnp
from jax import lax
import numpy as np

D_MODEL = 1024
BATCH = 8
SEQ = 2048
DEPTH = 2
DEC_BATCH = 8
DEC_SEQ = 4096
PAST_LEN = 128

EPS = 1e-6
D_FF = 2816
SSD_D_INNER = 1024
SSD_HEAD_DIM = 64
SSD_N_HEADS = SSD_D_INNER // SSD_HEAD_DIM
SSD_N_GROUPS = 2
SSD_HEADS_PER_GROUP = SSD_N_HEADS // SSD_N_GROUPS
SSD_D_STATE = 128
SSD_CONV = 5
SSD_CHUNK = 128
SSD_XBC = SSD_D_INNER + 2 * SSD_N_GROUPS * SSD_D_STATE
MLA_N_HEADS = 8
MLA_Q_RANK = 384
MLA_KV_RANK = 256
MLA_NOPE = 64
MLA_ROPE = 32
MLA_QK = MLA_NOPE + MLA_ROPE
MLA_V = 64
MLA_OUT = MLA_N_HEADS * MLA_V
ROPE_BASE = 10000.0
ATTN_BLOCK = 128
ML_N_HEADS = 8
ML_QK = 64
ML_V = 128
ML_CHUNK = 64
ODD_OUT = ML_N_HEADS * ML_V
EVEN_IN = 2 * SSD_D_INNER + 2 * SSD_N_GROUPS * SSD_D_STATE + 2 * SSD_N_HEADS + MLA_Q_RANK + MLA_KV_RANK + MLA_ROPE
EVEN_OUT = SSD_D_INNER + MLA_OUT
ODD_IN = 2 * ML_N_HEADS * ML_QK + 2 * ML_N_HEADS * ML_V + 4 * ML_N_HEADS
N_EVEN = (DEPTH + 1) // 2
N_ODD = DEPTH // 2

kernel_name = "hybrid_ssd_mla_mlstm_macaron_encoder"


def split_cols(x, sizes):
    idx = [int(i) for i in np.cumsum(sizes)[:-1]]
    return jnp.split(x, idx, axis=-1)


def rms_norm(x, g):
    xf = x.astype(jnp.float32)
    y = xf * lax.rsqrt(jnp.mean(xf * xf, axis=-1, keepdims=True) + EPS)
    return (y * g.astype(jnp.float32)).astype(x.dtype)


def group_rms_norm(x, g, n_groups):
    shp = x.shape
    xf = x.astype(jnp.float32).reshape(shp[:-1] + (n_groups, shp[-1] // n_groups))
    y = xf * lax.rsqrt(jnp.mean(xf * xf, axis=-1, keepdims=True) + EPS)
    return (y.reshape(shp) * g.astype(jnp.float32)).astype(x.dtype)


def swiglu(h, w_in, w_out):
    g, u = split_cols(h @ w_in, (D_FF, D_FF))
    return (jax.nn.silu(g) * u) @ w_out


def segsum(x):
    T = x.shape[-1]
    xr = jnp.broadcast_to(x[..., None], x.shape + (T,))
    xr = jnp.where(jnp.tril(jnp.ones((T, T), bool), -1), xr, 0.0)
    xs = jnp.cumsum(xr, axis=-2)
    return jnp.where(jnp.tril(jnp.ones((T, T), bool), 0), xs, -jnp.inf)


def ssd_scan(x, dt, a, b, c):
    Bn, S, G, R, P = x.shape
    N = b.shape[-1]
    L = SSD_CHUNK
    nc = S // L
    x = x.reshape(Bn, nc, L, G, R, P)
    dt = dt.reshape(Bn, nc, L, G, R)
    b = b.reshape(Bn, nc, L, G, N)
    c = c.reshape(Bn, nc, L, G, N)
    a_dt = jnp.moveaxis(dt * a, 2, -1)
    a_cum = jnp.cumsum(a_dt, axis=-1)
    xdt = x * dt[..., None]
    cb = jnp.einsum('bclgn,bcsgn->bcgls', c, b)
    m = cb[:, :, :, None] * jnp.exp(segsum(a_dt))
    y_diag = jnp.einsum('bcgrls,bcsgrp->bclgrp', m, xdt)
    decay_states = jnp.moveaxis(jnp.exp(a_cum[..., -1:] - a_cum), -1, 2)
    states = jnp.einsum('bclgn,bclgrp->bcgrpn', b, xdt * decay_states[..., None])
    chunk_decay = jnp.exp(a_cum[..., -1])

    def step(hst, inp):
        s_c, d_c = inp
        return hst * d_c[..., None, None] + s_c, hst

    h0 = jnp.zeros((Bn, G, R, P, N), x.dtype)
    _, h_in = lax.scan(step, h0, (jnp.moveaxis(states, 1, 0), jnp.moveaxis(chunk_decay, 1, 0)))
    h_in = jnp.moveaxis(h_in, 0, 1)
    y_off = jnp.einsum('bclgn,bcgrpn->bclgrp', c, h_in) * jnp.moveaxis(jnp.exp(a_cum), -1, 2)[..., None]
    return (y_diag + y_off).reshape(Bn, S, G, R, P)


def ssd_branch(z, xbc, dt_f, dt_b, conv_w, conv_b, dt_bias, a_log, d_skip, norm_g):
    Bn, S, _ = z.shape
    G, R, P, N = SSD_N_GROUPS, SSD_HEADS_PER_GROUP, SSD_HEAD_DIM, SSD_D_STATE
    f32 = jnp.float32
    pad = SSD_CONV // 2
    xbc = lax.conv_general_dilated(xbc, conv_w[:, None, :], window_strides=(1,), padding=[(pad, pad)],
                                   dimension_numbers=('NWC', 'WIO', 'NWC'), feature_group_count=SSD_XBC)
    xbc = jax.nn.silu((xbc + conv_b).astype(f32))
    xs, bs, cs = split_cols(xbc, (SSD_D_INNER, G * N, G * N))
    xs = xs.reshape(Bn, S, G, R, P)
    bs = bs.reshape(Bn, S, G, N)
    cs = cs.reshape(Bn, S, G, N)
    dtf = jax.nn.softplus(dt_f.astype(f32) + dt_bias[0].astype(f32)).reshape(Bn, S, G, R)
    dtb = jax.nn.softplus(dt_b.astype(f32) + dt_bias[1].astype(f32)).reshape(Bn, S, G, R)
    af = -jnp.exp(a_log[0].astype(f32)).reshape(G, R)
    ab = -jnp.exp(a_log[1].astype(f32)).reshape(G, R)
    flip = lambda t: jnp.flip(t, axis=1)
    y_f = ssd_scan(xs, dtf, af, bs, cs)
    y_b = flip(ssd_scan(flip(xs), flip(dtb), ab, flip(bs), flip(cs)))
    y = y_f + y_b + xs * d_skip.astype(f32).reshape(G, R, 1)
    y = y.reshape(Bn, S, SSD_D_INNER) * jax.nn.silu(z.astype(f32))
    return group_rms_norm(y, norm_g, SSD_N_GROUPS)


def rope_tables(S):
    pos = jnp.arange(S, dtype=jnp.float32)
    inv_freq = jnp.power(ROPE_BASE, -jnp.arange(0, MLA_ROPE, 2, dtype=jnp.float32) / MLA_ROPE)
    freqs = pos[:, None] * inv_freq[None, :]
    emb = jnp.concatenate([freqs, freqs], axis=-1)
    return jnp.cos(emb)[:, None, :], jnp.sin(emb)[:, None, :]


def rope_tail(x, cos, sin):
    x_nope, x_pe = x[..., :MLA_NOPE], x[..., MLA_NOPE:]
    half = MLA_ROPE // 2
    rot = jnp.concatenate([-x_pe[..., half:], x_pe[..., :half]], axis=-1)
    x_pe = (x_pe.astype(jnp.float32) * cos + rot.astype(jnp.float32) * sin).astype(x.dtype)
    return jnp.concatenate([x_nope, x_pe], axis=-1)


def dense_attention(q, k, v):
    Bn, S, H, Dq = q.shape
    nb = S // ATTN_BLOCK
    qb = jnp.swapaxes(q.reshape(Bn, nb, ATTN_BLOCK, H, Dq), 0, 1)
    scale = Dq ** -0.5

    def one(qi):
        s = jnp.einsum('bqhd,bkhd->bhqk', qi, k, preferred_element_type=jnp.float32) * scale
        p = jax.nn.softmax(s, axis=-1)
        return jnp.einsum('bhqk,bkhd->bqhd', p.astype(v.dtype), v)

    o = lax.map(one, qb)
    return jnp.swapaxes(o, 0, 1).reshape(Bn, S, H, v.shape[-1])


def mla_branch(q_a, kv_a, k_pe, q_a_norm, w_q_b, kv_a_norm, w_kv_b, q_norm, k_norm):
    Bn, S, _ = q_a.shape
    H = MLA_N_HEADS
    q = (rms_norm(q_a, q_a_norm) @ w_q_b).reshape(Bn, S, H, MLA_QK)
    kv = (rms_norm(kv_a, kv_a_norm) @ w_kv_b).reshape(Bn, S, H, MLA_NOPE + MLA_V)
    k_nope, v = kv[..., :MLA_NOPE], kv[..., MLA_NOPE:]
    k = jnp.concatenate([k_nope, jnp.broadcast_to(k_pe[:, :, None, :], (Bn, S, H, MLA_ROPE))], axis=-1)
    q = rms_norm(q, q_norm)
    k = rms_norm(k, k_norm)
    cos, sin = rope_tables(S)
    q = rope_tail(q, cos, sin)
    k = rope_tail(k, cos, sin)
    return dense_attention(q, k, v).reshape(Bn, S, MLA_OUT)


def even_mixer(h, w_in, w_out, conv_w, conv_b, dt_bias, a_log, d_skip, ssd_norm,
               q_a_norm, w_q_b, kv_a_norm, w_kv_b, q_norm, k_norm):
    z, xbc, dt_f, dt_b, q_a, kv_a, k_pe = split_cols(
        h @ w_in, (SSD_D_INNER, SSD_XBC, SSD_N_HEADS, SSD_N_HEADS, MLA_Q_RANK, MLA_KV_RANK, MLA_ROPE))
    y_ssd = ssd_branch(z, xbc, dt_f, dt_b, conv_w, conv_b, dt_bias, a_log, d_skip, ssd_norm).astype(h.dtype)
    y_mla = mla_branch(q_a, kv_a, k_pe, q_a_norm, w_q_b, kv_a_norm, w_kv_b, q_norm, k_norm)
    return jnp.concatenate([y_ssd, y_mla], axis=-1) @ w_out


def mlstm_dir(q, k, v, li, lf):
    Bn, S, H, dk = q.shape
    dv = v.shape[-1]
    L = ML_CHUNK
    nc = S // L
    q = q.reshape(Bn, nc, L, H, dk)
    k = k.reshape(Bn, nc, L, H, dk) * (dk ** -0.5)
    v = v.reshape(Bn, nc, L, H, dv)
    li = li.reshape(Bn, nc, L, H)
    lf = lf.reshape(Bn, nc, L, H)
    bcum = jnp.cumsum(lf, axis=2)
    g_end = bcum[:, :, -1:, :] - bcum + li
    b_end = bcum[:, :, -1, :]
    m_intra = jnp.max(g_end, axis=2)

    def step(carry, inp):
        c_st, n_st, m_st = carry
        k_c, v_c, g_c, be_c, mi_c = inp
        m_new = jnp.maximum(be_c + m_st, mi_c)
        w = jnp.exp(g_c - m_new[:, None, :])
        decay = jnp.exp(be_c + m_st - m_new)
        wk = k_c * w[..., None]
        c_new = decay[..., None, None] * c_st + jnp.einsum('blhv,blhk->bhvk', v_c, wk)
        n_new = decay[..., None] * n_st + jnp.sum(wk, axis=1)
        return (c_new, n_new, m_new), (c_st, n_st, m_st)

    init = (jnp.zeros((Bn, H, dv, dk), q.dtype), jnp.zeros((Bn, H, dk), q.dtype), jnp.zeros((Bn, H), q.dtype))
    xs = (jnp.moveaxis(k, 1, 0), jnp.moveaxis(v, 1, 0), jnp.moveaxis(g_end, 1, 0),
          jnp.moveaxis(b_end, 1, 0), jnp.moveaxis(m_intra, 1, 0))
    _, (c_in, n_in, m_in) = lax.scan(step, init, xs)
    c_in = jnp.moveaxis(c_in, 0, 1)
    n_in = jnp.moveaxis(n_in, 0, 1)
    m_in = jnp.moveaxis(m_in, 0, 1)
    tri = jnp.tril(jnp.ones((L, L), bool))
    dmat = bcum[:, :, :, None, :] - bcum[:, :, None, :, :] + li[:, :, None, :, :]
    dmat = jnp.where(tri[None, None, :, :, None], dmat, -jnp.inf)
    a_inter = bcum + m_in[:, :, None, :]
    m_t = jnp.maximum(a_inter, jnp.max(dmat, axis=3))
    w_intra = jnp.exp(dmat - m_t[:, :, :, None, :])
    w_inter = jnp.exp(a_inter - m_t)
    sw = jnp.einsum('bcthd,bcshd->bctsh', q, k) * w_intra
    num = (jnp.einsum('bctsh,bcshv->bcthv', sw, v)
           + w_inter[..., None] * jnp.einsum('bcthd,bchvd->bcthv', q, c_in))
    den = jnp.sum(sw, axis=3) + w_inter * jnp.einsum('bcthd,bchd->bcth', q, n_in)
    hout = num / jnp.maximum(jnp.abs(den), jnp.exp(-m_t))[..., None]
    return hout.reshape(Bn, S, H, dv)


def odd_mixer(h, w_in, w_out, ig_bias, fg_bias, norm_g):
    Bn, S, _ = h.shape
    H, dk, dv = ML_N_HEADS, ML_QK, ML_V
    f32 = jnp.float32
    q, k, v, o, i_f, i_b, f_f, f_b = split_cols(h @ w_in, (H * dk, H * dk, H * dv, H * dv, H, H, H, H))
    q = q.astype(f32).reshape(Bn, S, H, dk)
    k = k.astype(f32).reshape(Bn, S, H, dk)
    v = v.astype(f32).reshape(Bn, S, H, dv)
    li_f = i_f.astype(f32) + ig_bias[0].astype(f32)
    li_b = i_b.astype(f32) + ig_bias[1].astype(f32)
    lf_f = jax.nn.log_sigmoid(f_f.astype(f32) + fg_bias[0].astype(f32))
    lf_b = jax.nn.log_sigmoid(f_b.astype(f32) + fg_bias[1].astype(f32))
    flip = lambda t: jnp.flip(t, axis=1)
    h_f = mlstm_dir(q, k, v, li_f, lf_f)
    h_b = flip(mlstm_dir(flip(q), flip(k), flip(v), flip(li_b), flip(lf_b)))
    hs = group_rms_norm((h_f + h_b).reshape(Bn, S, H * dv), norm_g, H)
    out = (jax.nn.sigmoid(o.astype(f32)) * hs).astype(h.dtype)
    return out @ w_out


def trunk(x, p):
    (ffn1_norm, ffn1_w_in, ffn1_w_out, mix_norm, ffn2_norm, ffn2_w_in, ffn2_w_out,
     ev_w_in, ev_w_out, ssd_conv_w, ssd_conv_b, ssd_dt_bias, ssd_a_log, ssd_d_skip, ssd_norm,
     mla_q_a_norm, mla_w_q_b, mla_kv_a_norm, mla_w_kv_b, mla_q_norm, mla_k_norm,
     od_w_in, od_w_out, ml_ig_bias, ml_fg_bias, ml_norm) = p
    for layer in range(DEPTH):
        j = layer // 2
        x = x + 0.5 * swiglu(rms_norm(x, ffn1_norm[layer]), ffn1_w_in[layer], ffn1_w_out[layer])
        h = rms_norm(x, mix_norm[layer])
        if layer % 2 == 0:
            x = x + even_mixer(h, ev_w_in[j], ev_w_out[j], ssd_conv_w[j], ssd_conv_b[j], ssd_dt_bias[j],
                               ssd_a_log[j], ssd_d_skip[j], ssd_norm[j], mla_q_a_norm[j], mla_w_q_b[j],
                               mla_kv_a_norm[j], mla_w_kv_b[j], mla_q_norm[j], mla_k_norm[j])
        else:
            x = x + odd_mixer(h, od_w_in[j], od_w_out[j], ml_ig_bias[j], ml_fg_bias[j], ml_norm[j])
        x = x + 0.5 * swiglu(rms_norm(x, ffn2_norm[layer]), ffn2_w_in[layer], ffn2_w_out[layer])
    return x


def setup_inputs(seed: int = 0) -> dict:
    key = jax.random.key(seed)
    keys = iter(jax.random.split(key, 40))
    f32 = jnp.float32

    def nrm(shape, scale):
        return scale * jax.random.normal(next(keys), shape, f32)

    def gain(shape):
        return 1.0 + 0.02 * jax.random.normal(next(keys), shape, f32)

    def unif(shape, lo, hi):
        return jax.random.uniform(next(keys), shape, f32, lo, hi)

    dt0 = jnp.exp(unif((N_EVEN, 2, SSD_N_HEADS), math.log(1e-3), math.log(1e-1)))
    return {
        "x_prompt": nrm((BATCH, SEQ, D_MODEL), 1.0),
        "x_sample": nrm((DEC_BATCH, DEC_SEQ, D_MODEL), 1.0),
        "ffn1_norm": gain((DEPTH, D_MODEL)),
        "ffn1_w_in": nrm((DEPTH, D_MODEL, 2 * D_FF), D_MODEL ** -0.5),
        "ffn1_w_out": nrm((DEPTH, D_FF, D_MODEL), D_FF ** -0.5),
        "mix_norm": gain((DEPTH, D_MODEL)),
        "ffn2_norm": gain((DEPTH, D_MODEL)),
        "ffn2_w_in": nrm((DEPTH, D_MODEL, 2 * D_FF), D_MODEL ** -0.5),
        "ffn2_w_out": nrm((DEPTH, D_FF, D_MODEL), D_FF ** -0.5),
        "ev_w_in": nrm((N_EVEN, D_MODEL, EVEN_IN), D_MODEL ** -0.5),
        "ev_w_out": nrm((N_EVEN, EVEN_OUT, D_MODEL), EVEN_OUT ** -0.5),
        "ssd_conv_w": nrm((N_EVEN, SSD_CONV, SSD_XBC), SSD_CONV ** -0.5),
        "ssd_conv_b": nrm((N_EVEN, SSD_XBC), 0.02),
        "ssd_dt_bias": dt0 + jnp.log(-jnp.expm1(-dt0)),
        "ssd_a_log": jnp.log(unif((N_EVEN, 2, SSD_N_HEADS), 1.0, 16.0)),
        "ssd_d_skip": gain((N_EVEN, SSD_N_HEADS)),
        "ssd_norm": gain((N_EVEN, SSD_D_INNER)),
        "mla_q_a_norm": gain((N_EVEN, MLA_Q_RANK)),
        "mla_w_q_b": nrm((N_EVEN, MLA_Q_RANK, MLA_N_HEADS * MLA_QK), MLA_Q_RANK ** -0.5),
        "mla_kv_a_norm": gain((N_EVEN, MLA_KV_RANK)),
        "mla_w_kv_b": nrm((N_EVEN, MLA_KV_RANK, MLA_N_HEADS * (MLA_NOPE + MLA_V)), MLA_KV_RANK ** -0.5),
        "mla_q_norm": gain((N_EVEN, MLA_QK)),
        "mla_k_norm": gain((N_EVEN, MLA_QK)),
        "od_w_in": nrm((N_ODD, D_MODEL, ODD_IN), D_MODEL ** -0.5),
        "od_w_out": nrm((N_ODD, ODD_OUT, D_MODEL), ODD_OUT ** -0.5),
        "ml_ig_bias": nrm((N_ODD, 2, ML_N_HEADS), 0.1),
        "ml_fg_bias": unif((N_ODD, 2, ML_N_HEADS), 3.0, 6.0),
        "ml_norm": gain((N_ODD, ODD_OUT)),
    }


def reference(x_prompt, x_sample, ffn1_norm, ffn1_w_in, ffn1_w_out, mix_norm, ffn2_norm, ffn2_w_in, ffn2_w_out,
              ev_w_in, ev_w_out, ssd_conv_w, ssd_conv_b, ssd_dt_bias, ssd_a_log, ssd_d_skip, ssd_norm,
              mla_q_a_norm, mla_w_q_b, mla_kv_a_norm, mla_w_kv_b, mla_q_norm, mla_k_norm,
              od_w_in, od_w_out, ml_ig_bias, ml_fg_bias, ml_norm):
    params = (ffn1_norm, ffn1_w_in, ffn1_w_out, mix_norm, ffn2_norm, ffn2_w_in, ffn2_w_out,
              ev_w_in, ev_w_out, ssd_conv_w, ssd_conv_b, ssd_dt_bias, ssd_a_log, ssd_d_skip, ssd_norm,
              mla_q_a_norm, mla_w_q_b, mla_kv_a_norm, mla_w_kv_b, mla_q_norm, mla_k_norm,
              od_w_in, od_w_out, ml_ig_bias, ml_fg_bias, ml_norm)
    y_prompt = trunk(x_prompt, params)
    y_sample = trunk(x_sample, params)
    return (y_prompt, y_sample)
```

```python
import functools

import jax
import jax.numpy as jnp
from jax import lax
from jax.experimental import pallas as pl
from jax.experimental.pallas import tpu as pltpu

F32 = jnp.float32
BF16 = jnp.bfloat16
EPS = 1e-6

LANES = 128
SUBLANES = 8
VMEM_LIMIT = 56 * 1024 * 1024

SSD_HEAD_DIM = 64
SSD_N_GROUPS = 2
SSD_D_STATE = 128
SSD_CONV = 5
MLA_N_HEADS = 8
MLA_Q_RANK = 384
MLA_KV_RANK = 256
MLA_NOPE = 64
MLA_ROPE = 32
MLA_V = 64
ROPE_BASE = 10000.0
ML_N_HEADS = 8
ML_QK = 64
ML_V = 128

TOKEN_TILE = 512
SCAN_CHUNK = 128
ATTN_Q_TILE = 256
HALO = SUBLANES


def _params(semantics):
    return pltpu.CompilerParams(dimension_semantics=semantics, vmem_limit_bytes=VMEM_LIMIT)


def _const_spec(shape):
    nd = len(shape)
    return pl.BlockSpec(shape, lambda *_: (0,) * nd, pipeline_mode=pl.Buffered(1))


def _rms(x, g):
    ms = jnp.mean(x * x, axis=-1, keepdims=True)
    return x * lax.rsqrt(ms + EPS) * g


def _sigmoid(x):
    return 1.0 / (1.0 + jnp.exp(-x))


def _softplus(x):
    return jnp.maximum(x, 0.0) + jnp.log1p(jnp.exp(-jnp.abs(x)))


def _dot(a, b):
    return jnp.dot(a, b, preferred_element_type=F32)


def _dot_nt(a, b):
    return lax.dot_general(a, b, (((1,), (1,)), ((), ())), preferred_element_type=F32)


def _dot_exact(a, b):
    return jnp.dot(a, b, preferred_element_type=F32, precision=lax.Precision.HIGHEST)


def _dot_split(x, e_bf16):
    hi = x.astype(BF16)
    lo = (x - hi.astype(F32)).astype(BF16)
    return _dot(hi, e_bf16) + _dot(lo, e_bf16)


def _scan_masks(n, reverse):
    row = lax.broadcasted_iota(jnp.int32, (n, n), 0)
    col = lax.broadcasted_iota(jnp.int32, (n, n), 1)
    if reverse:
        return col >= row, col <= row
    return col <= row, col >= row


def _ffn_kernel(x_ref, g_ref, win_ref, wout_ref, o_ref, *, d_ff):
    x = x_ref[...]
    h = _rms(x, g_ref[...]).astype(BF16)
    hw = _dot(h, win_ref[...])
    gate = hw[:, :d_ff]
    up = hw[:, d_ff:]
    a = (gate * _sigmoid(gate) * up).astype(BF16)
    o_ref[...] = x + 0.5 * _dot(a, wout_ref[...])


def _ffn(x, g, w_in, w_out):
    t, d = x.shape
    d_ff = w_out.shape[0]
    tm = TOKEN_TILE
    return pl.pallas_call(
        functools.partial(_ffn_kernel, d_ff=d_ff),
        out_shape=jax.ShapeDtypeStruct((t, d), F32),
        grid=(t // tm,),
        in_specs=[
            pl.BlockSpec((tm, d), lambda i: (i, 0)),
            _const_spec(g.shape), _const_spec(w_in.shape), _const_spec(w_out.shape),
        ],
        out_specs=pl.BlockSpec((tm, d), lambda i: (i, 0)),
        compiler_params=_params(("parallel",)),
        name="ffn",
    )(x, g, w_in, w_out)


def _rope(x, cos_t, sin_lo, sin_hi):
    half = MLA_ROPE // 2
    return (x * cos_t + pltpu.roll(x, LANES - half, axis=1) * sin_lo
            + pltpu.roll(x, half, axis=1) * sin_hi)


def _even_in_kernel(x_ref, g_ref, w_ref, dtb_ref, qan_ref, wqb_ref, kvan_ref, wkvb_ref,
                    qn_ref, kn_ref, cos_ref, slo_ref, shi_ref,
                    z_ref, xbc_ref, dt_ref, dtt_ref, q_ref, k_ref, vt_ref,
                    *, d_inner, d_xbc):
    x = x_ref[...]
    h = _rms(x, g_ref[...]).astype(BF16)
    hw = _dot(h, w_ref[...])
    c0 = d_inner
    c1 = c0 + d_xbc
    c2 = c1 + MLA_Q_RANK
    c3 = c2 + MLA_KV_RANK
    z_ref[...] = hw[:, :c0]
    xbc_ref[...] = hw[:, c0:c1]
    small = hw[:, c3:c3 + LANES]
    dt = _softplus(small + dtb_ref[...])
    dt_ref[...] = dt
    dtt_ref[...] = dt.T[:dtt_ref.shape[0], :]

    qa = _rms(hw[:, c1:c2], qan_ref[...]).astype(BF16)
    q = _dot(qa, wqb_ref[...])
    kva = _rms(hw[:, c2:c3], kvan_ref[...]).astype(BF16)
    kv = _dot(kva, wkvb_ref[...])
    n_k = MLA_N_HEADS * LANES
    vt_ref[...] = kv[:, n_k:].T.astype(BF16)

    lane = lax.broadcasted_iota(jnp.int32, (1, LANES), 1)
    pe_lanes = (lane >= MLA_NOPE) & (lane < MLA_NOPE + MLA_ROPE)
    kpe = jnp.where(pe_lanes, pltpu.roll(small, MLA_NOPE - MLA_ROPE, axis=1), 0.0)
    cos_t, sin_lo, sin_hi = cos_ref[...], slo_ref[...], shi_ref[...]
    inv_dim = 1.0 / (MLA_NOPE + MLA_ROPE)
    scale = (MLA_NOPE + MLA_ROPE) ** -0.5
    for hd in range(MLA_N_HEADS):
        sl = slice(hd * LANES, (hd + 1) * LANES)
        qh = q[:, sl]
        qh = qh * lax.rsqrt(jnp.sum(qh * qh, axis=-1, keepdims=True) * inv_dim + EPS) * qn_ref[...]
        q_ref[:, sl] = (_rope(qh, cos_t, sin_lo, sin_hi) * scale).astype(BF16)
        kh = kv[:, sl] + kpe
        kh = kh * lax.rsqrt(jnp.sum(kh * kh, axis=-1, keepdims=True) * inv_dim + EPS) * kn_ref[...]
        k_ref[:, sl] = _rope(kh, cos_t, sin_lo, sin_hi).astype(BF16)


def _even_in(x, g, w, dt_bias, qan, wqb, kvan, wkvb, qn, kn, cos_t, sin_lo, sin_hi, *, d_inner, d_xbc):
    b, s, d = x.shape
    tm = TOKEN_TILE
    n_q = MLA_N_HEADS * LANES
    n_v = MLA_N_HEADS * MLA_V
    n_dt = 2 * (d_inner // SSD_HEAD_DIM)
    tok = lambda c: pl.BlockSpec((None, tm, c), lambda bi, i: (bi, i, 0))
    tab = pl.BlockSpec((tm, LANES), lambda bi, i: (i, 0))
    return pl.pallas_call(
        functools.partial(_even_in_kernel, d_inner=d_inner, d_xbc=d_xbc),
        out_shape=[
            jax.ShapeDtypeStruct((b, s, d_inner), F32),
            jax.ShapeDtypeStruct((b, s, d_xbc), F32),
            jax.ShapeDtypeStruct((b, s, LANES), F32),
            jax.ShapeDtypeStruct((b, n_dt, s), F32),
            jax.ShapeDtypeStruct((b, s, n_q), BF16),
            jax.ShapeDtypeStruct((b, s, n_q), BF16),
            jax.ShapeDtypeStruct((b, n_v, s), BF16),
        ],
        grid=(b, s // tm),
        in_specs=[
            tok(d), _const_spec(g.shape), _const_spec(w.shape), _const_spec(dt_bias.shape),
            _const_spec(qan.shape), _const_spec(wqb.shape), _const_spec(kvan.shape),
            _const_spec(wkvb.shape), _const_spec(qn.shape), _const_spec(kn.shape),
            tab, tab, tab,
        ],
        out_specs=[
            tok(d_inner), tok(d_xbc), tok(LANES),
            pl.BlockSpec((None, n_dt, tm), lambda bi, i: (bi, 0, i)),
            tok(n_q), tok(n_q),
            pl.BlockSpec((None, n_v, tm), lambda bi, i: (bi, 0, i)),
        ],
        compiler_params=_params(("parallel", "parallel")),
        name="even_in",
    )(x, g, w, dt_bias, qan, wqb, kvan, wkvb, qn, kn, cos_t, sin_lo, sin_hi)


def _conv_kernel(main_ref, prev_ref, next_ref, w_ref, b_ref, o_ref, ext_ref, *, n_tiles):
    i = pl.program_id(1)
    tc = main_ref.shape[0]
    zero = jnp.zeros(prev_ref.shape, F32)
    ext_ref[0:HALO, :] = jnp.where(i > 0, prev_ref[...], zero)
    ext_ref[HALO:HALO + tc, :] = main_ref[...]
    ext_ref[HALO + tc:, :] = jnp.where(i < n_tiles - 1, next_ref[...], zero)
    pad = SSD_CONV // 2
    acc = b_ref[...] + w_ref[0:1, :] * ext_ref[pl.ds(HALO - pad, tc), :]
    for k in range(1, SSD_CONV):
        acc = acc + w_ref[k:k + 1, :] * ext_ref[pl.ds(HALO - pad + k, tc), :]
    o_ref[...] = acc * _sigmoid(acc)


def _conv(xbc, conv_w, conv_b):
    b, s, c = xbc.shape
    tc = TOKEN_TILE
    n_tiles = s // tc
    per = tc // HALO
    n_halo = s // HALO
    return pl.pallas_call(
        functools.partial(_conv_kernel, n_tiles=n_tiles),
        out_shape=jax.ShapeDtypeStruct((b, s, c), F32),
        grid=(b, n_tiles),
        in_specs=[
            pl.BlockSpec((None, tc, c), lambda bi, i: (bi, i, 0)),
            pl.BlockSpec((None, HALO, c), lambda bi, i: (bi, jnp.maximum(i * per - 1, 0), 0)),
            pl.BlockSpec((None, HALO, c),
                         lambda bi, i: (bi, jnp.minimum((i + 1) * per, n_halo - 1), 0)),
            _const_spec(conv_w.shape), _const_spec(conv_b.shape),
        ],
        out_specs=pl.BlockSpec((None, tc, c), lambda bi, i: (bi, i, 0)),
        scratch_shapes=[pltpu.VMEM((tc + 2 * HALO, c), F32)],
        compiler_params=_params(("parallel", "parallel")),
        name="ssd_conv",
    )(xbc, xbc, xbc, conv_w, conv_b)


def _ssd_direction(xbc, dt, dtt, a_row, a_col, expand, s_ref, d_skip, *, lane_off, reverse, d_inner):
    L = xbc.shape[0]
    n_heads = d_inner // SSD_HEAD_DIM
    hpg = n_heads // SSD_N_GROUPS
    gw = hpg * SSD_HEAD_DIM
    gn = SSD_D_STATE
    xs = xbc[:, :d_inner]
    keep, keep_t = _scan_masks(L, reverse)
    last = 0 if reverse else L - 1

    cum = _dot_exact(keep.astype(F32), dt * a_row)
    cumt = _dot_exact(dtt * a_col, keep_t.astype(F32))
    cum_last = cum[last:last + 1, :]
    narrow = jnp.concatenate(
        [jnp.exp(cum), dt * jnp.exp(cum_last - cum),
         jnp.broadcast_to(jnp.exp(cum_last), (SUBLANES, LANES))], axis=0)
    wide = _dot_split(narrow, expand)
    e_off = wide[:L]
    w_state = wide[L:2 * L]
    chunk_decay = wide[2 * L:2 * L + 1]

    lane = lax.broadcasted_iota(jnp.int32, (1, LANES), 1)
    lo_lanes = lane < SSD_HEAD_DIM
    parts = []
    for g in range(SSD_N_GROUPS):
        gsl = slice(g * gw, (g + 1) * gw)
        b_g = xbc[:, d_inner + g * gn:d_inner + (g + 1) * gn]
        c_off = d_inner + SSD_N_GROUPS * gn
        c_bf = xbc[:, c_off + g * gn:c_off + (g + 1) * gn].astype(BF16)
        cb = _dot_nt(c_bf, b_g.astype(BF16))
        state = s_ref[g]
        y_off = _dot(c_bf, state.astype(BF16)) * e_off[:, gsl]
        xw = (xs[:, gsl] * w_state[:, gsl]).astype(BF16)
        s_ref[g] = state * chunk_decay[:, gsl] + _dot(b_g.T.astype(BF16), xw)
        for pr in range(hpg // 2):
            m_pair = []
            for j in range(2):
                hd = g * hpg + 2 * pr + j
                seg = cum[:, lane_off + hd:lane_off + hd + 1] - cumt[hd:hd + 1, :]
                m_h = jnp.where(keep, cb * jnp.exp(seg) * dtt[hd:hd + 1, :], 0.0)
                m_pair.append(m_h.astype(BF16))
            psl = slice(g * gw + pr * LANES, g * gw + (pr + 1) * LANES)
            x_pair = xs[:, psl]
            rhs = jnp.concatenate([jnp.where(lo_lanes, x_pair, 0.0),
                                   jnp.where(lo_lanes, 0.0, x_pair)], axis=0).astype(BF16)
            y_pair = _dot(jnp.concatenate(m_pair, axis=1), rhs) + y_off[:, pr * LANES:(pr + 1) * LANES]
            if d_skip is not None:
                y_pair = y_pair + x_pair * d_skip[:, psl]
            parts.append(y_pair)
    return jnp.concatenate(parts, axis=1)


def _ssd_kernel(xf_ref, xb_ref, dtf_ref, dtb_ref, dttf_ref, dttb_ref, alog_row_ref, alog_col_ref,
                dskip_ref, ef_ref, eb_ref, yf_ref, yb_ref, sf_ref, sb_ref, *, d_inner):
    @pl.when(pl.program_id(1) == 0)
    def _():
        sf_ref[...] = jnp.zeros(sf_ref.shape, F32)
        sb_ref[...] = jnp.zeros(sb_ref.shape, F32)

    n_heads = d_inner // SSD_HEAD_DIM
    lane = lax.broadcasted_iota(jnp.int32, (1, LANES), 1)
    a_all = -jnp.exp(alog_row_ref[...])
    a_col = -jnp.exp(alog_col_ref[...])
    yf_ref[...] = _ssd_direction(
        xf_ref[...], dtf_ref[...], dttf_ref[0:n_heads, :],
        jnp.where(lane < n_heads, a_all, 0.0), a_col[0:n_heads, :], ef_ref[...], sf_ref,
        dskip_ref[...], lane_off=0, reverse=False, d_inner=d_inner)
    yb_ref[...] = _ssd_direction(
        xb_ref[...], dtb_ref[...], dttb_ref[n_heads:2 * n_heads, :],
        jnp.where((lane >= n_heads) & (lane < 2 * n_heads), a_all, 0.0), a_col[n_heads:2 * n_heads, :],
        eb_ref[...], sb_ref, None, lane_off=n_heads, reverse=True, d_inner=d_inner)


def _ssd(xbc, dt, dtt, alog_row, alog_col, d_skip, e_f, e_b, *, d_inner):
    b, s, c = xbc.shape
    L = SCAN_CHUNK
    nc = s // L
    n_dt = dtt.shape[1]
    hpg_w = d_inner // SSD_N_GROUPS
    fwd = lambda w: pl.BlockSpec((None, L, w), lambda bi, i: (bi, i, 0))
    bwd = lambda w: pl.BlockSpec((None, L, w), lambda bi, i: (bi, nc - 1 - i, 0))
    return pl.pallas_call(
        functools.partial(_ssd_kernel, d_inner=d_inner),
        out_shape=[jax.ShapeDtypeStruct((b, s, d_inner), F32)] * 2,
        grid=(b, nc),
        in_specs=[
            fwd(c), bwd(c), fwd(LANES), bwd(LANES),
            pl.BlockSpec((None, n_dt, L), lambda bi, i: (bi, 0, i)),
            pl.BlockSpec((None, n_dt, L), lambda bi, i: (bi, 0, nc - 1 - i)),
            _const_spec(alog_row.shape), _const_spec(alog_col.shape), _const_spec(d_skip.shape),
            _const_spec(e_f.shape), _const_spec(e_b.shape),
        ],
        out_specs=[fwd(d_inner), bwd(d_inner)],
        scratch_shapes=[pltpu.VMEM((SSD_N_GROUPS, SSD_D_STATE, hpg_w), F32)] * 2,
        compiler_params=_params(("parallel", "arbitrary")),
        name="ssd_scan",
    )(xbc, xbc, dt, dt, dtt, dtt, alog_row, alog_col, d_skip, e_f, e_b)


def _attn_kernel(q_ref, k_ref, vt_ref, o_ref):
    outs = []
    for hd in range(MLA_N_HEADS):
        sl = slice(hd * LANES, (hd + 1) * LANES)
        st = _dot_nt(k_ref[:, sl], q_ref[:, sl])
        m = jnp.max(st, axis=0, keepdims=True)
        p = jnp.exp(st - m)
        l = jnp.sum(p, axis=0, keepdims=True)
        ot = _dot(vt_ref[hd * MLA_V:(hd + 1) * MLA_V, :], p.astype(BF16))
        outs.append(ot / l)
    o_ref[...] = jnp.concatenate(outs, axis=0).T.astype(BF16)


def _attention(q, k, vt):
    b, s, n_q = q.shape
    n_v = vt.shape[1]
    tq = ATTN_Q_TILE
    return pl.pallas_call(
        _attn_kernel,
        out_shape=jax.ShapeDtypeStruct((b, s, n_v), BF16),
        grid=(b, s // tq),
        in_specs=[
            pl.BlockSpec((None, tq, n_q), lambda bi, i: (bi, i, 0)),
            pl.BlockSpec((None, s, n_q), lambda bi, i: (bi, 0, 0)),
            pl.BlockSpec((None, n_v, s), lambda bi, i: (bi, 0, 0)),
        ],
        out_specs=pl.BlockSpec((None, tq, n_v), lambda bi, i: (bi, i, 0)),
        compiler_params=_params(("parallel", "arbitrary")),
        name="mla_attention",
    )(q, k, vt)


def _even_out_kernel(x_ref, yf_ref, yb_ref, z_ref, o_ref, g_ref, w_ref, out_ref, *, d_inner):
    z = z_ref[...]
    y = (yf_ref[...] + yb_ref[...]) * (z * _sigmoid(z))
    gw = d_inner // SSD_N_GROUPS
    normed = []
    for g in range(SSD_N_GROUPS):
        seg = y[:, g * gw:(g + 1) * gw]
        normed.append(seg * lax.rsqrt(jnp.mean(seg * seg, axis=-1, keepdims=True) + EPS))
    yn = (jnp.concatenate(normed, axis=1) * g_ref[...]).astype(BF16)
    out_ref[...] = (x_ref[...] + _dot(yn, w_ref[0:d_inner, :])
                    + _dot(o_ref[...], w_ref[d_inner:, :]))


def _even_out(x, yf, yb, z, o, g, w):
    t, d = x.shape
    d_inner = yf.shape[1]
    tm = TOKEN_TILE
    tok = lambda c: pl.BlockSpec((tm, c), lambda i: (i, 0))
    return pl.pallas_call(
        functools.partial(_even_out_kernel, d_inner=d_inner),
        out_shape=jax.ShapeDtypeStruct((t, d), F32),
        grid=(t // tm,),
        in_specs=[tok(d), tok(d_inner), tok(d_inner), tok(d_inner), tok(o.shape[1]),
                  _const_spec(g.shape), _const_spec(w.shape)],
        out_specs=tok(d),
        compiler_params=_params(("parallel",)),
        name="even_out",
    )(x, yf, yb, z, o, g, w)


def _odd_in_kernel(x_ref, g_ref, w_ref, gb_ref, q_ref, k_ref, v_ref, o_ref, gates_ref, gatest_ref):
    h = _rms(x_ref[...], g_ref[...]).astype(BF16)
    hw = _dot(h, w_ref[...])
    n_qk = ML_N_HEADS * ML_QK
    n_v = ML_N_HEADS * ML_V
    q_ref[...] = hw[:, :n_qk].astype(BF16)
    k_ref[...] = hw[:, n_qk:2 * n_qk].astype(BF16)
    v_ref[...] = hw[:, 2 * n_qk:2 * n_qk + n_v].astype(BF16)
    o_ref[...] = hw[:, 2 * n_qk + n_v:2 * n_qk + 2 * n_v]
    pre = hw[:, 2 * n_qk + 2 * n_v:] + gb_ref[...]
    lane = lax.broadcasted_iota(jnp.int32, (1, LANES), 1)
    gates = jnp.where(lane < 2 * ML_N_HEADS, pre, -_softplus(-pre))
    gates_ref[...] = gates
    gatest_ref[...] = gates.T[:gatest_ref.shape[0], :]


def _odd_in(x, g, w, gate_bias):
    b, s, d = x.shape
    tm = TOKEN_TILE
    n_qk = ML_N_HEADS * ML_QK
    n_v = ML_N_HEADS * ML_V
    n_g = 4 * ML_N_HEADS
    tok = lambda c: pl.BlockSpec((None, tm, c), lambda bi, i: (bi, i, 0))
    return pl.pallas_call(
        _odd_in_kernel,
        out_shape=[
            jax.ShapeDtypeStruct((b, s, n_qk), BF16), jax.ShapeDtypeStruct((b, s, n_qk), BF16),
            jax.ShapeDtypeStruct((b, s, n_v), BF16), jax.ShapeDtypeStruct((b, s, n_v), F32),
            jax.ShapeDtypeStruct((b, s, LANES), F32), jax.ShapeDtypeStruct((b, n_g, s), F32),
        ],
        grid=(b, s // tm),
        in_specs=[tok(d), _const_spec(g.shape), _const_spec(w.shape), _const_spec(gate_bias.shape)],
        out_specs=[tok(n_qk), tok(n_qk), tok(n_v), tok(n_v), tok(LANES),
                   pl.BlockSpec((None, n_g, tm), lambda bi, i: (bi, 0, i))],
        compiler_params=_params(("parallel", "parallel")),
        name="odd_in",
    )(x, g, w, gate_bias)


def _mlstm_direction(q, k, v, gates, gatest, c_ref, m_ref, *, i_off, f_off, reverse):
    L = q.shape[0]
    keep, keep_t = _scan_masks(L, reverse)
    last = 0 if reverse else L - 1
    bcum = _dot_exact(keep.astype(F32), gates)
    bcumt = _dot_exact(gatest, keep_t.astype(F32))
    lane = lax.broadcasted_iota(jnp.int32, (1, LANES), 1)
    srow = lax.broadcasted_iota(jnp.int32, (LANES, 1), 0)
    ones = jnp.ones((L, ML_V), BF16)
    k_scale = ML_QK ** -0.5
    outs = []
    for pr in range(ML_N_HEADS // 2):
        q_pair = q[:, pr * LANES:(pr + 1) * LANES]
        k_pair = k[:, pr * LANES:(pr + 1) * LANES]
        c_aug = c_ref[pr]
        c_bf = c_aug.astype(BF16)
        update = jnp.zeros(c_aug.shape, F32)
        decays = []
        for j in range(2):
            hd = 2 * pr + j
            own = (lane < ML_QK) if j == 0 else (lane >= ML_QK)
            qm = jnp.where(own, q_pair, jnp.zeros_like(q_pair))
            bc = bcum[:, f_off + hd:f_off + hd + 1]
            li_c = gates[:, i_off + hd:i_off + hd + 1]
            br = bcumt[f_off + hd:f_off + hd + 1, :]
            li_r = gatest[i_off + hd:i_off + hd + 1, :]
            m_in = m_ref[hd:hd + 1, 0:1]
            dmat = jnp.where(keep, bc - br + li_r, -jnp.inf)
            a_inter = bc + m_in
            m_t = jnp.maximum(a_inter, jnp.max(dmat, axis=1, keepdims=True))
            w_intra = jnp.exp(dmat - m_t)
            w_inter = jnp.exp(a_inter - m_t)
            sw = (_dot_nt(qm, k_pair) * k_scale * w_intra).astype(BF16)
            v_aug = jnp.concatenate([v[:, hd * ML_V:(hd + 1) * ML_V], ones], axis=1)
            r = _dot(sw, v_aug) + w_inter * _dot(qm, c_bf)
            num = r[:, :ML_V]
            den = r[:, ML_V:]
            outs.append(num / jnp.maximum(jnp.abs(den), jnp.exp(-m_t)))
            b_end = bc[last:last + 1, :]
            g_end = b_end - bc + li_c
            m_new = jnp.maximum(b_end + m_in, jnp.max(g_end, axis=0, keepdims=True))
            w_src = jnp.exp(g_end - m_new) * k_scale
            wk = jnp.where(own, k_pair.astype(F32), 0.0) * w_src
            update = update + _dot(wk.T.astype(BF16), v_aug)
            decays.append(jnp.exp(b_end + m_in - m_new))
            m_ref[hd:hd + 1, :] = jnp.broadcast_to(m_new, (1, LANES))
        c_ref[pr] = c_aug * jnp.where(srow < ML_QK, decays[0], decays[1]) + update
    return jnp.concatenate(outs, axis=1)


def _mlstm_kernel(qf_ref, kf_ref, vf_ref, gf_ref, gtf_ref, qb_ref, kb_ref, vb_ref, gb_ref, gtb_ref,
                  hf_ref, hb_ref, cf_ref, cb_ref, mf_ref, mb_ref):
    @pl.when(pl.program_id(1) == 0)
    def _():
        for ref in (cf_ref, cb_ref, mf_ref, mb_ref):
            ref[...] = jnp.zeros(ref.shape, F32)

    h = ML_N_HEADS
    hf_ref[...] = _mlstm_direction(qf_ref[...], kf_ref[...], vf_ref[...], gf_ref[...], gtf_ref[...],
                                   cf_ref, mf_ref, i_off=0, f_off=2 * h, reverse=False)
    hb_ref[...] = _mlstm_direction(qb_ref[...], kb_ref[...], vb_ref[...], gb_ref[...], gtb_ref[...],
                                   cb_ref, mb_ref, i_off=h, f_off=3 * h, reverse=True)


def _mlstm(q, k, v, gates, gatest):
    b, s, n_qk = q.shape
    n_v = v.shape[2]
    n_g = gatest.shape[1]
    L = SCAN_CHUNK
    nc = s // L
    fwd = lambda w: pl.BlockSpec((None, L, w), lambda bi, i: (bi, i, 0))
    bwd = lambda w: pl.BlockSpec((None, L, w), lambda bi, i: (bi, nc - 1 - i, 0))
    fwd_t = pl.BlockSpec((None, n_g, L), lambda bi, i: (bi, 0, i))
    bwd_t = pl.BlockSpec((None, n_g, L), lambda bi, i: (bi, 0, nc - 1 - i))
    state = pltpu.VMEM((ML_N_HEADS // 2, 2 * ML_QK, 2 * ML_V), F32)
    stab = pltpu.VMEM((ML_N_HEADS, LANES), F32)
    return pl.pallas_call(
        _mlstm_kernel,
        out_shape=[jax.ShapeDtypeStruct((b, s, n_v), F32)] * 2,
        grid=(b, nc),
        in_specs=[fwd(n_qk), fwd(n_qk), fwd(n_v), fwd(LANES), fwd_t,
                  bwd(n_qk), bwd(n_qk), bwd(n_v), bwd(LANES), bwd_t],
        out_specs=[fwd(n_v), bwd(n_v)],
        scratch_shapes=[state, state, stab, stab],
        compiler_params=_params(("parallel", "arbitrary")),
        name="mlstm_scan",
    )(q, k, v, gates, gatest, q, k, v, gates, gatest)


def _odd_out_kernel(x_ref, hf_ref, hb_ref, o_ref, g_ref, w_ref, out_ref):
    hs = hf_ref[...] + hb_ref[...]
    normed = []
    for hd in range(ML_N_HEADS):
        seg = hs[:, hd * ML_V:(hd + 1) * ML_V]
        normed.append(seg * lax.rsqrt(jnp.mean(seg * seg, axis=-1, keepdims=True) + EPS))
    gated = (_sigmoid(o_ref[...]) * (jnp.concatenate(normed, axis=1) * g_ref[...])).astype(BF16)
    out_ref[...] = x_ref[...] + _dot(gated, w_ref[...])


def _odd_out(x, hf, hb, o, g, w):
    t, d = x.shape
    n_v = hf.shape[1]
    tm = TOKEN_TILE
    tok = lambda c: pl.BlockSpec((tm, c), lambda i: (i, 0))
    return pl.pallas_call(
        _odd_out_kernel,
        out_shape=jax.ShapeDtypeStruct((t, d), F32),
        grid=(t // tm,),
        in_specs=[tok(d), tok(n_v), tok(n_v), tok(n_v), _const_spec(g.shape), _const_spec(w.shape)],
        out_specs=tok(d),
        compiler_params=_params(("parallel",)),
        name="odd_out",
    )(x, hf, hb, o, g, w)


def _row(v):
    return v.reshape(1, -1).astype(F32)


def _pad_lanes(v, width=LANES):
    return jnp.pad(v, [(0, 0)] * (v.ndim - 1) + [(0, width - v.shape[-1])])


def _prep_even(ev_w_in, ev_w_out, conv_w, conv_b, dt_bias, a_log, d_skip, ssd_norm,
               q_a_norm, w_q_b, kv_a_norm, w_kv_b, q_norm, k_norm):
    d_inner = ssd_norm.shape[0]
    d_xbc = conv_b.shape[0]
    n_heads = a_log.shape[1]
    c0, c1 = d_inner, d_inner + d_xbc
    c2 = c1 + 2 * n_heads
    c3 = c2 + MLA_Q_RANK
    c4 = c3 + MLA_KV_RANK
    small = _pad_lanes(jnp.concatenate([ev_w_in[:, c1:c2], ev_w_in[:, c4:]], axis=1))
    w_in = jnp.concatenate([ev_w_in[:, :c1], ev_w_in[:, c2:c4], small], axis=1).astype(BF16)
    d_qk = MLA_NOPE + MLA_ROPE
    wqb = _pad_lanes(w_q_b.reshape(MLA_Q_RANK, MLA_N_HEADS, d_qk)).reshape(MLA_Q_RANK, -1).astype(BF16)
    wkv = w_kv_b.reshape(MLA_KV_RANK, MLA_N_HEADS, MLA_NOPE + MLA_V)
    wkvb = jnp.concatenate(
        [_pad_lanes(wkv[:, :, :MLA_NOPE]).reshape(MLA_KV_RANK, -1),
         wkv[:, :, MLA_NOPE:].reshape(MLA_KV_RANK, -1)], axis=1).astype(BF16)
    head_of_lane = jnp.arange(d_inner) // SSD_HEAD_DIM
    e_f = (jnp.arange(LANES)[:, None] == head_of_lane[None, :]).astype(BF16)
    e_b = (jnp.arange(LANES)[:, None] == head_of_lane[None, :] + n_heads).astype(BF16)
    return dict(
        d_inner=d_inner, d_xbc=d_xbc, w_in=w_in, w_out=ev_w_out.astype(BF16),
        conv_w=_pad_lanes(conv_w.T, SUBLANES).T.astype(F32), conv_b=_row(conv_b),
        dt_bias=_pad_lanes(_row(dt_bias)), alog_row=_pad_lanes(_row(a_log)),
        alog_col=a_log.reshape(-1, 1).astype(F32), d_skip=_row(jnp.repeat(d_skip, SSD_HEAD_DIM)),
        ssd_norm=_row(ssd_norm), e_f=e_f, e_b=e_b,
        q_a_norm=_row(q_a_norm), wqb=wqb, kv_a_norm=_row(kv_a_norm), wkvb=wkvb,
        q_norm=_pad_lanes(_row(q_norm)), k_norm=_pad_lanes(_row(k_norm)))


def _rope_tables(s):
    half = MLA_ROPE // 2
    pos = jnp.arange(s, dtype=F32)
    inv_freq = jnp.power(ROPE_BASE, -jnp.arange(0, MLA_ROPE, 2, dtype=F32) / MLA_ROPE)
    freqs = pos[:, None] * inv_freq[None, :]
    cos, sin = jnp.cos(freqs), jnp.sin(freqs)
    zeros = jnp.zeros_like(sin)
    lead = jnp.ones((s, MLA_NOPE), F32)
    tail = LANES - MLA_NOPE - MLA_ROPE
    cos_t = jnp.concatenate([lead, cos, cos, jnp.ones((s, tail), F32)], axis=1)
    sin_lo = jnp.concatenate([0 * lead, -sin, zeros, jnp.zeros((s, tail), F32)], axis=1)
    sin_hi = jnp.concatenate([0 * lead, zeros, sin, jnp.zeros((s, tail), F32)], axis=1)
    return cos_t, sin_lo, sin_hi


def _even_mixer(x, norm_g, p):
    b, s, d = x.shape
    z, xbc, dt, dtt, q, k, vt = _even_in(
        x, norm_g, p["w_in"], p["dt_bias"], p["q_a_norm"], p["wqb"], p["kv_a_norm"], p["wkvb"],
        p["q_norm"], p["k_norm"], *_rope_tables(s), d_inner=p["d_inner"], d_xbc=p["d_xbc"])
    xact = _conv(xbc, p["conv_w"], p["conv_b"])
    yf, yb = _ssd(xact, dt, dtt, p["alog_row"], p["alog_col"], p["d_skip"], p["e_f"], p["e_b"],
                  d_inner=p["d_inner"])
    o = _attention(q, k, vt)
    flat = lambda a: a.reshape(b * s, a.shape[-1])
    return _even_out(flat(x), flat(yf), flat(yb), flat(z), flat(o), p["ssd_norm"],
                     p["w_out"]).reshape(b, s, d)


def _prep_odd(od_w_in, od_w_out, ig_bias, fg_bias, ml_norm):
    n_main = 2 * ML_N_HEADS * ML_QK + 2 * ML_N_HEADS * ML_V
    w_in = jnp.concatenate([od_w_in[:, :n_main], _pad_lanes(od_w_in[:, n_main:])], axis=1).astype(BF16)
    gate_bias = _pad_lanes(_row(jnp.concatenate([ig_bias.reshape(-1), fg_bias.reshape(-1)])))
    return dict(w_in=w_in, w_out=od_w_out.astype(BF16), gate_bias=gate_bias, ml_norm=_row(ml_norm))


def _odd_mixer(x, norm_g, p):
    b, s, d = x.shape
    q, k, v, o, gates, gatest = _odd_in(x, norm_g, p["w_in"], p["gate_bias"])
    hf, hb = _mlstm(q, k, v, gates, gatest)
    flat = lambda a: a.reshape(b * s, a.shape[-1])
    return _odd_out(flat(x), flat(hf), flat(hb), flat(o), p["ml_norm"], p["w_out"]).reshape(b, s, d)


def kernel(x_prompt, x_sample, ffn1_norm, ffn1_w_in, ffn1_w_out, mix_norm, ffn2_norm, ffn2_w_in, ffn2_w_out,
           ev_w_in, ev_w_out, ssd_conv_w, ssd_conv_b, ssd_dt_bias, ssd_a_log, ssd_d_skip, ssd_norm,
           mla_q_a_norm, mla_w_q_b, mla_kv_a_norm, mla_w_kv_b, mla_q_norm, mla_k_norm,
           od_w_in, od_w_out, ml_ig_bias, ml_fg_bias, ml_norm):
    depth = ffn1_norm.shape[0]
    layers = []
    for layer in range(depth):
        j = layer // 2
        if layer % 2 == 0:
            mixer = functools.partial(_even_mixer, p=_prep_even(
                ev_w_in[j], ev_w_out[j], ssd_conv_w[j], ssd_conv_b[j], ssd_dt_bias[j], ssd_a_log[j],
                ssd_d_skip[j], ssd_norm[j], mla_q_a_norm[j], mla_w_q_b[j], mla_kv_a_norm[j],
                mla_w_kv_b[j], mla_q_norm[j], mla_k_norm[j]))
        else:
            mixer = functools.partial(_odd_mixer, p=_prep_odd(
                od_w_in[j], od_w_out[j], ml_ig_bias[j], ml_fg_bias[j], ml_norm[j]))
        layers.append(dict(
            mixer=mixer, mix_norm=_row(mix_norm[layer]),
            ffn1=(_row(ffn1_norm[layer]), ffn1_w_in[layer].astype(BF16), ffn1_w_out[layer].astype(BF16)),
            ffn2=(_row(ffn2_norm[layer]), ffn2_w_in[layer].astype(BF16), ffn2_w_out[layer].astype(BF16))))

    def trunk(x):
        b, s, d = x.shape
        for lp in layers:
            x = _ffn(x.reshape(b * s, d), *lp["ffn1"]).reshape(b, s, d)
            x = lp["mixer"](x, lp["mix_norm"])
            x = _ffn(x.reshape(b * s, d), *lp["ffn2"]).reshape(b, s, d)
        return x

    return trunk(x_prompt), trunk(x_sample)
```

```python
import functools

import jax
import jax.numpy as jnp
from jax import lax
from jax.experimental import pallas as pl
from jax.experimental.pallas import tpu as pltpu

F32 = jnp.float32
BF16 = jnp.bfloat16
EPS = 1e-6

LANES = 128
SUBLANES = 8
VMEM_LIMIT = 56 * 1024 * 1024

SSD_HEAD_DIM = 64
SSD_N_GROUPS = 2
SSD_D_STATE = 128
SSD_CONV = 5
MLA_N_HEADS = 8
MLA_Q_RANK = 384
MLA_KV_RANK = 256
MLA_NOPE = 64
MLA_ROPE = 32
MLA_V = 64
ROPE_BASE = 10000.0
LOG2_E = 1.4426950408889634
ML_N_HEADS = 8
ML_QK = 64
ML_V = 128

TOKEN_TILE = 512
SCAN_CHUNK = 128
ATTN_Q_TILE = 512
HALO = SUBLANES


def _params(semantics):
    return pltpu.CompilerParams(dimension_semantics=semantics, vmem_limit_bytes=VMEM_LIMIT)


def _const_spec(shape):
    nd = len(shape)
    return pl.BlockSpec(shape, lambda *_: (0,) * nd, pipeline_mode=pl.Buffered(1))


def _rms(x, g):
    ms = jnp.mean(x * x, axis=-1, keepdims=True)
    return x * lax.rsqrt(ms + EPS) * g


def _sigmoid(x):
    return 1.0 / (1.0 + jnp.exp(-x))


def _softplus(x):
    return jnp.maximum(x, 0.0) + jnp.log1p(jnp.exp(-jnp.abs(x)))


def _dot(a, b):
    return jnp.dot(a, b, preferred_element_type=F32)


def _dot_nt(a, b):
    return lax.dot_general(a, b, (((1,), (1,)), ((), ())), preferred_element_type=F32)


def _dot_exact(a, b):
    return jnp.dot(a, b, preferred_element_type=F32, precision=lax.Precision.HIGHEST)


def _dot_tn(a, b):
    return lax.dot_general(a, b, (((0,), (0,)), ((), ())), preferred_element_type=F32)


def _split_terms(x, terms):
    out = []
    for _ in range(terms):
        piece = x.astype(BF16)
        out.append(piece)
        x = x - piece.astype(F32)
    return out


def _dot_split(x, e_bf16):
    return sum(_dot(t, e_bf16) for t in _split_terms(x, 2))


def _scan_masks(n, reverse):
    row = lax.broadcasted_iota(jnp.int32, (n, n), 0)
    col = lax.broadcasted_iota(jnp.int32, (n, n), 1)
    if reverse:
        return col >= row, col <= row
    return col <= row, col >= row


def _ffn_kernel(x_ref, g_ref, win_ref, wout_ref, o_ref, *, d_ff):
    x = x_ref[...]
    h = _rms(x, g_ref[...]).astype(BF16)
    hw = _dot(h, win_ref[...])
    gate = hw[:, :d_ff]
    up = hw[:, d_ff:]
    a = (gate * _sigmoid(gate) * up).astype(BF16)
    o_ref[...] = x + 0.5 * _dot(a, wout_ref[...])


def _ffn(x, g, w_in, w_out):
    t, d = x.shape
    d_ff = w_out.shape[0]
    tm = TOKEN_TILE
    return pl.pallas_call(
        functools.partial(_ffn_kernel, d_ff=d_ff),
        out_shape=jax.ShapeDtypeStruct((t, d), F32),
        grid=(t // tm,),
        in_specs=[
            pl.BlockSpec((tm, d), lambda i: (i, 0)),
            _const_spec(g.shape), _const_spec(w_in.shape), _const_spec(w_out.shape),
        ],
        out_specs=pl.BlockSpec((tm, d), lambda i: (i, 0)),
        compiler_params=_params(("parallel",)),
        name="ffn",
    )(x, g, w_in, w_out)


def _even_in_kernel(x_ref, g_ref, w_ref, dtb_ref, qan_ref, wq_ref, kvan_ref, wkvb_ref,
                    qgc_ref, qgs_ref, kgc_ref, kgs_ref, ones_ref, cos_ref, sin_ref,
                    z_ref, xbc_ref, dt_ref, dtt_ref, q_ref, k_ref, vt_ref,
                    *, d_inner, d_xbc):
    x = x_ref[...]
    h = _rms(x, g_ref[...]).astype(BF16)
    hw = _dot(h, w_ref[...])
    c0 = d_inner
    c1 = c0 + d_xbc
    c2 = c1 + MLA_Q_RANK
    c3 = c2 + MLA_KV_RANK
    z_ref[...] = hw[:, :c0]
    xbc_ref[...] = hw[:, c0:c1]
    small = hw[:, c3:c3 + LANES]
    kpe_sw = hw[:, c3 + LANES:c3 + 2 * LANES]
    dt = _softplus(small + dtb_ref[...])
    dt_ref[...] = dt
    dtt_ref[...] = dt.T[:dtt_ref.shape[0], :]

    n_k = MLA_N_HEADS * LANES
    qa = _rms(hw[:, c1:c2], qan_ref[...]).astype(BF16)
    qq = _dot(qa, wq_ref[...])
    kva = _rms(hw[:, c2:c3], kvan_ref[...]).astype(BF16)
    kv = _dot(kva, wkvb_ref[...])
    vt_ref[...] = kv[:, n_k:].T.astype(BF16)

    lane = lax.broadcasted_iota(jnp.int32, (1, LANES), 1)
    pe_lanes = (lane >= MLA_NOPE) & (lane < MLA_NOPE + MLA_ROPE)
    kpe = jnp.where(pe_lanes, small, 0.0)
    cos_t, sin_t = cos_ref[...], sin_ref[...]
    q_cos, q_sin = cos_t * qgc_ref[...], sin_t * qgs_ref[...]
    k_cos, k_sin = cos_t * kgc_ref[...], sin_t * kgs_ref[...]
    k_rot = kpe_sw * k_sin
    inv_dim = 1.0 / (MLA_NOPE + MLA_ROPE)

    q_blk = [qq[:, hd * LANES:(hd + 1) * LANES] for hd in range(MLA_N_HEADS)]
    k_blk = [kv[:, hd * LANES:(hd + 1) * LANES] + kpe for hd in range(MLA_N_HEADS)]
    q_ss, k_ss = [], []
    for pr in range(MLA_N_HEADS // 2):
        for blk, out in ((q_blk, q_ss), (k_blk, k_ss)):
            sq = jnp.concatenate([blk[2 * pr] * blk[2 * pr], blk[2 * pr + 1] * blk[2 * pr + 1]], axis=1)
            ss = _dot(sq.astype(BF16), ones_ref[...])
            out.extend([ss[:, :LANES], ss[:, LANES:]])
    for hd in range(MLA_N_HEADS):
        sl = slice(hd * LANES, (hd + 1) * LANES)
        q_rot = qq[:, n_k + hd * LANES:n_k + (hd + 1) * LANES]
        rq = lax.rsqrt(q_ss[hd] * inv_dim + EPS)
        q_ref[:, sl] = ((q_blk[hd] * q_cos + q_rot * q_sin) * rq).astype(BF16)
        rk = lax.rsqrt(k_ss[hd] * inv_dim + EPS)
        k_ref[:, sl] = ((k_blk[hd] * k_cos + k_rot) * rk).astype(BF16)


def _even_in(x, g, p, cos_t, sin_t):
    b, s, d = x.shape
    tm = TOKEN_TILE
    d_inner, d_xbc = p["d_inner"], p["d_xbc"]
    n_q = MLA_N_HEADS * LANES
    n_v = MLA_N_HEADS * MLA_V
    n_dt = 2 * (d_inner // SSD_HEAD_DIM)
    tok = lambda c: pl.BlockSpec((None, tm, c), lambda bi, i: (bi, i, 0))
    tab = pl.BlockSpec((tm, LANES), lambda bi, i: (i, 0))
    consts = [g, p["w_in"], p["dt_bias"], p["q_a_norm"], p["wq"], p["kv_a_norm"], p["wkvb"],
              p["q_gain_cos"], p["q_gain_sin"], p["k_gain_cos"], p["k_gain_sin"], p["pair_ones"]]
    return pl.pallas_call(
        functools.partial(_even_in_kernel, d_inner=d_inner, d_xbc=d_xbc),
        out_shape=[
            jax.ShapeDtypeStruct((b, s, d_inner), F32),
            jax.ShapeDtypeStruct((b, s, d_xbc), F32),
            jax.ShapeDtypeStruct((b, s, LANES), F32),
            jax.ShapeDtypeStruct((b, n_dt, s), F32),
            jax.ShapeDtypeStruct((b, s, n_q), BF16),
            jax.ShapeDtypeStruct((b, s, n_q), BF16),
            jax.ShapeDtypeStruct((b, n_v, s), BF16),
        ],
        grid=(b, s // tm),
        in_specs=[tok(d)] + [_const_spec(c.shape) for c in consts] + [tab, tab],
        out_specs=[
            tok(d_inner), tok(d_xbc), tok(LANES),
            pl.BlockSpec((None, n_dt, tm), lambda bi, i: (bi, 0, i)),
            tok(n_q), tok(n_q),
            pl.BlockSpec((None, n_v, tm), lambda bi, i: (bi, 0, i)),
        ],
        compiler_params=_params(("parallel", "parallel")),
        name="even_in",
    )(x, *consts, cos_t, sin_t)


def _conv_kernel(main_ref, prev_ref, next_ref, w_ref, b_ref, o_ref, ext_ref, *, n_tiles):
    i = pl.program_id(1)
    tc = main_ref.shape[0]
    zero = jnp.zeros(prev_ref.shape, F32)
    ext_ref[0:HALO, :] = jnp.where(i > 0, prev_ref[...], zero)
    ext_ref[HALO:HALO + tc, :] = main_ref[...]
    ext_ref[HALO + tc:, :] = jnp.where(i < n_tiles - 1, next_ref[...], zero)
    pad = SSD_CONV // 2
    acc = b_ref[...] + w_ref[0:1, :] * ext_ref[pl.ds(HALO - pad, tc), :]
    for k in range(1, SSD_CONV):
        acc = acc + w_ref[k:k + 1, :] * ext_ref[pl.ds(HALO - pad + k, tc), :]
    o_ref[...] = acc * _sigmoid(acc)


def _conv(xbc, conv_w, conv_b):
    b, s, c = xbc.shape
    tc = TOKEN_TILE
    n_tiles = s // tc
    per = tc // HALO
    n_halo = s // HALO
    return pl.pallas_call(
        functools.partial(_conv_kernel, n_tiles=n_tiles),
        out_shape=jax.ShapeDtypeStruct((b, s, c), F32),
        grid=(b, n_tiles),
        in_specs=[
            pl.BlockSpec((None, tc, c), lambda bi, i: (bi, i, 0)),
            pl.BlockSpec((None, HALO, c), lambda bi, i: (bi, jnp.maximum(i * per - 1, 0), 0)),
            pl.BlockSpec((None, HALO, c),
                         lambda bi, i: (bi, jnp.minimum((i + 1) * per, n_halo - 1), 0)),
            _const_spec(conv_w.shape), _const_spec(conv_b.shape),
        ],
        out_specs=pl.BlockSpec((None, tc, c), lambda bi, i: (bi, i, 0)),
        scratch_shapes=[pltpu.VMEM((tc + 2 * HALO, c), F32)],
        compiler_params=_params(("parallel", "parallel")),
        name="ssd_conv",
    )(xbc, xbc, xbc, conv_w, conv_b)


def _ssd_direction(xbc, dt, dtt, a_row, a_col, expand, s_ref, d_skip, *, lane_off, reverse, d_inner):
    L = xbc.shape[0]
    n_heads = d_inner // SSD_HEAD_DIM
    hpg = n_heads // SSD_N_GROUPS
    gw = hpg * SSD_HEAD_DIM
    gn = SSD_D_STATE
    xs = xbc[:, :d_inner]
    keep, keep_t = _scan_masks(L, reverse)
    last = 0 if reverse else L - 1

    cum = _dot_exact(keep.astype(F32), dt * a_row)
    cumt = _dot_exact(dtt * a_col, keep_t.astype(F32))
    cum_last = cum[last:last + 1, :]
    narrow = jnp.concatenate(
        [jnp.exp(cum), dt * jnp.exp(cum_last - cum),
         jnp.broadcast_to(jnp.exp(cum_last), (SUBLANES, LANES))], axis=0)
    wide = _dot_split(narrow, expand)
    e_off = wide[:L]
    w_state = wide[L:2 * L]
    chunk_decay = wide[2 * L:2 * L + 1]

    lane = lax.broadcasted_iota(jnp.int32, (1, LANES), 1)
    lo_lanes = lane < SSD_HEAD_DIM
    parts = []
    for g in range(SSD_N_GROUPS):
        gsl = slice(g * gw, (g + 1) * gw)
        b_g = xbc[:, d_inner + g * gn:d_inner + (g + 1) * gn]
        c_off = d_inner + SSD_N_GROUPS * gn
        c_bf = xbc[:, c_off + g * gn:c_off + (g + 1) * gn].astype(BF16)
        cb = _dot_nt(c_bf, b_g.astype(BF16))
        state = s_ref[g]
        y_off = _dot(c_bf, state.astype(BF16)) * e_off[:, gsl]
        xw = (xs[:, gsl] * w_state[:, gsl]).astype(BF16)
        s_ref[g] = state * chunk_decay[:, gsl] + _dot(b_g.T.astype(BF16), xw)
        for pr in range(hpg // 2):
            m_pair = []
            for j in range(2):
                hd = g * hpg + 2 * pr + j
                seg = cum[:, lane_off + hd:lane_off + hd + 1] - cumt[hd:hd + 1, :]
                m_h = jnp.where(keep, cb * jnp.exp(seg) * dtt[hd:hd + 1, :], 0.0)
                m_pair.append(m_h.astype(BF16))
            psl = slice(g * gw + pr * LANES, g * gw + (pr + 1) * LANES)
            x_pair = xs[:, psl]
            rhs = jnp.concatenate([jnp.where(lo_lanes, x_pair, 0.0),
                                   jnp.where(lo_lanes, 0.0, x_pair)], axis=0).astype(BF16)
            y_pair = _dot(jnp.concatenate(m_pair, axis=1), rhs) + y_off[:, pr * LANES:(pr + 1) * LANES]
            if d_skip is not None:
                y_pair = y_pair + x_pair * d_skip[:, psl]
            parts.append(y_pair)
    return jnp.concatenate(parts, axis=1)


def _ssd_kernel(xf_ref, xb_ref, dtf_ref, dtb_ref, dttf_ref, dttb_ref, alog_row_ref, alog_col_ref,
                dskip_ref, ef_ref, eb_ref, yf_ref, yb_ref, sf_ref, sb_ref, *, d_inner):
    @pl.when(pl.program_id(1) == 0)
    def _():
        sf_ref[...] = jnp.zeros(sf_ref.shape, F32)
        sb_ref[...] = jnp.zeros(sb_ref.shape, F32)

    n_heads = d_inner // SSD_HEAD_DIM
    lane = lax.broadcasted_iota(jnp.int32, (1, LANES), 1)
    a_all = -jnp.exp(alog_row_ref[...])
    a_col = -jnp.exp(alog_col_ref[...])
    yf_ref[...] = _ssd_direction(
        xf_ref[...], dtf_ref[...], dttf_ref[0:n_heads, :],
        jnp.where(lane < n_heads, a_all, 0.0), a_col[0:n_heads, :], ef_ref[...], sf_ref,
        dskip_ref[...], lane_off=0, reverse=False, d_inner=d_inner)
    yb_ref[...] = _ssd_direction(
        xb_ref[...], dtb_ref[...], dttb_ref[n_heads:2 * n_heads, :],
        jnp.where((lane >= n_heads) & (lane < 2 * n_heads), a_all, 0.0), a_col[n_heads:2 * n_heads, :],
        eb_ref[...], sb_ref, None, lane_off=n_heads, reverse=True, d_inner=d_inner)


def _ssd(xbc, dt, dtt, alog_row, alog_col, d_skip, e_f, e_b, *, d_inner):
    b, s, c = xbc.shape
    L = SCAN_CHUNK
    nc = s // L
    n_dt = dtt.shape[1]
    hpg_w = d_inner // SSD_N_GROUPS
    fwd = lambda w: pl.BlockSpec((None, L, w), lambda bi, i: (bi, i, 0))
    bwd = lambda w: pl.BlockSpec((None, L, w), lambda bi, i: (bi, nc - 1 - i, 0))
    return pl.pallas_call(
        functools.partial(_ssd_kernel, d_inner=d_inner),
        out_shape=[jax.ShapeDtypeStruct((b, s, d_inner), F32)] * 2,
        grid=(b, nc),
        in_specs=[
            fwd(c), bwd(c), fwd(LANES), bwd(LANES),
            pl.BlockSpec((None, n_dt, L), lambda bi, i: (bi, 0, i)),
            pl.BlockSpec((None, n_dt, L), lambda bi, i: (bi, 0, nc - 1 - i)),
            _const_spec(alog_row.shape), _const_spec(alog_col.shape), _const_spec(d_skip.shape),
            _const_spec(e_f.shape), _const_spec(e_b.shape),
        ],
        out_specs=[fwd(d_inner), bwd(d_inner)],
        scratch_shapes=[pltpu.VMEM((SSD_N_GROUPS, SSD_D_STATE, hpg_w), F32)] * 2,
        compiler_params=_params(("parallel", "arbitrary")),
        name="ssd_scan",
    )(xbc, xbc, dt, dt, dtt, dtt, alog_row, alog_col, d_skip, e_f, e_b)


def _attn_kernel(q_ref, k_ref, vt_ref, o_ref):
    def scores(hd):
        sl = slice(hd * LANES, (hd + 1) * LANES)
        return _dot_nt(k_ref[:, sl], q_ref[:, sl])

    outs = []
    st_next = scores(0)
    for hd in range(MLA_N_HEADS):
        st = st_next
        if hd + 1 < MLA_N_HEADS:
            st_next = scores(hd + 1)
        m = jnp.max(st, axis=0, keepdims=True)
        p = jnp.exp2(st - m)
        l = jnp.sum(p, axis=0, keepdims=True)
        ot = _dot(vt_ref[hd * MLA_V:(hd + 1) * MLA_V, :], p.astype(BF16))
        outs.append(ot / l)
    o_ref[...] = jnp.concatenate(outs, axis=0).T.astype(BF16)


def _attention(q, k, vt):
    b, s, n_q = q.shape
    n_v = vt.shape[1]
    tq = ATTN_Q_TILE
    return pl.pallas_call(
        _attn_kernel,
        out_shape=jax.ShapeDtypeStruct((b, s, n_v), BF16),
        grid=(b, s // tq),
        in_specs=[
            pl.BlockSpec((None, tq, n_q), lambda bi, i: (bi, i, 0)),
            pl.BlockSpec((None, s, n_q), lambda bi, i: (bi, 0, 0), pipeline_mode=pl.Buffered(1)),
            pl.BlockSpec((None, n_v, s), lambda bi, i: (bi, 0, 0), pipeline_mode=pl.Buffered(1)),
        ],
        out_specs=pl.BlockSpec((None, tq, n_v), lambda bi, i: (bi, i, 0)),
        compiler_params=_params(("parallel", "arbitrary")),
        name="mla_attention",
    )(q, k, vt)


def _even_out_kernel(x_ref, yf_ref, yb_ref, z_ref, o_ref, g_ref, w_ref, out_ref, *, d_inner):
    z = z_ref[...]
    y = (yf_ref[...] + yb_ref[...]) * (z * _sigmoid(z))
    gw = d_inner // SSD_N_GROUPS
    normed = []
    for g in range(SSD_N_GROUPS):
        seg = y[:, g * gw:(g + 1) * gw]
        normed.append(seg * lax.rsqrt(jnp.mean(seg * seg, axis=-1, keepdims=True) + EPS))
    yn = (jnp.concatenate(normed, axis=1) * g_ref[...]).astype(BF16)
    out_ref[...] = (x_ref[...] + _dot(yn, w_ref[0:d_inner, :])
                    + _dot(o_ref[...], w_ref[d_inner:, :]))


def _even_out(x, yf, yb, z, o, g, w):
    t, d = x.shape
    d_inner = yf.shape[1]
    tm = TOKEN_TILE
    tok = lambda c: pl.BlockSpec((tm, c), lambda i: (i, 0))
    return pl.pallas_call(
        functools.partial(_even_out_kernel, d_inner=d_inner),
        out_shape=jax.ShapeDtypeStruct((t, d), F32),
        grid=(t // tm,),
        in_specs=[tok(d), tok(d_inner), tok(d_inner), tok(d_inner), tok(o.shape[1]),
                  _const_spec(g.shape), _const_spec(w.shape)],
        out_specs=tok(d),
        compiler_params=_params(("parallel",)),
        name="even_out",
    )(x, yf, yb, z, o, g, w)


def _odd_in_kernel(x_ref, g_ref, w_ref, gb_ref, qt_ref, k_ref, kt_ref, v_ref, o_ref, gates_ref, gatest_ref):
    h = _rms(x_ref[...], g_ref[...]).astype(BF16)
    hw = _dot(h, w_ref[...])
    n_qk = ML_N_HEADS * ML_QK
    n_v = ML_N_HEADS * ML_V
    qt_ref[...] = hw[:, :n_qk].T.astype(BF16)
    k = hw[:, n_qk:2 * n_qk]
    k_ref[...] = k.astype(BF16)
    kt_ref[...] = k.T.astype(BF16)
    v_ref[...] = hw[:, 2 * n_qk:2 * n_qk + n_v].astype(BF16)
    o_ref[...] = hw[:, 2 * n_qk + n_v:2 * n_qk + 2 * n_v]
    pre = hw[:, 2 * n_qk + 2 * n_v:] + gb_ref[...]
    lane = lax.broadcasted_iota(jnp.int32, (1, LANES), 1)
    gates = jnp.where(lane < 2 * ML_N_HEADS, pre, -_softplus(-pre))
    gates_ref[...] = gates
    gatest_ref[...] = gates.T[:gatest_ref.shape[0], :]


def _odd_in(x, g, w, gate_bias):
    b, s, d = x.shape
    tm = TOKEN_TILE
    n_qk = ML_N_HEADS * ML_QK
    n_v = ML_N_HEADS * ML_V
    n_g = 4 * ML_N_HEADS
    tok = lambda c: pl.BlockSpec((None, tm, c), lambda bi, i: (bi, i, 0))
    tok_t = lambda c: pl.BlockSpec((None, c, tm), lambda bi, i: (bi, 0, i))
    return pl.pallas_call(
        _odd_in_kernel,
        out_shape=[
            jax.ShapeDtypeStruct((b, n_qk, s), BF16),
            jax.ShapeDtypeStruct((b, s, n_qk), BF16),
            jax.ShapeDtypeStruct((b, n_qk, s), BF16),
            jax.ShapeDtypeStruct((b, s, n_v), BF16),
            jax.ShapeDtypeStruct((b, s, n_v), F32),
            jax.ShapeDtypeStruct((b, s, LANES), F32),
            jax.ShapeDtypeStruct((b, n_g, s), F32),
        ],
        grid=(b, s // tm),
        in_specs=[tok(d), _const_spec(g.shape), _const_spec(w.shape), _const_spec(gate_bias.shape)],
        out_specs=[tok_t(n_qk), tok(n_qk), tok_t(n_qk), tok(n_v), tok(n_v), tok(LANES), tok_t(n_g)],
        compiler_params=_params(("parallel", "parallel")),
        name="odd_in",
    )(x, g, w, gate_bias)


def _cummax_lanes(u, reverse):
    n = u.shape[1]
    lane = lax.broadcasted_iota(jnp.int32, (1, n), 1)
    d = 1
    while d < n:
        if reverse:
            shifted, valid = pltpu.roll(u, n - d, axis=1), lane < n - d
        else:
            shifted, valid = pltpu.roll(u, d, axis=1), lane >= d
        u = jnp.where(valid, jnp.maximum(u, shifted), u)
        d *= 2
    return u


def _mlstm_direction(k, kt, qt, v, gates, gatest, sel_u, c_ref, m_ref, *, i_off, f_off, reverse):
    L = k.shape[0]
    H = ML_N_HEADS
    assert L == LANES == ML_V and 2 * ML_QK == LANES
    keep, keep_t = _scan_masks(L, reverse)
    keep_bf = keep.astype(F32).astype(BF16)
    keep_t_bf = keep_t.astype(F32).astype(BF16)
    last = 0 if reverse else L - 1

    li_r = gatest[i_off:i_off + H, :]
    bcum_r = sum(_dot(t, keep_t_bf) for t in _split_terms(gatest[f_off:f_off + H, :], 3))
    u_r = li_r - bcum_r
    m_in = m_ref[...]
    mx = jnp.maximum(m_in, _cummax_lanes(u_r, reverse))
    w_inter = jnp.exp(m_in - mx)
    e_negm = jnp.exp(-(bcum_r + mx))
    b_end = jnp.broadcast_to(bcum_r[:, last:last + 1], (H, L))
    g_end = b_end + u_r
    m_new = jnp.maximum(b_end + m_in, jnp.max(g_end, axis=1, keepdims=True))
    w_src = jnp.exp(g_end - m_new)
    decay = jnp.exp(b_end + m_in - m_new)
    m_ref[...] = m_new

    bcum_c = sum(_dot(keep_bf, t) for t in _split_terms(gates, 3))
    u_all = sum(_dot(t, sel_u) for t in _split_terms(jnp.concatenate([gates, bcum_c], axis=1), 2))

    srow = lax.broadcasted_iota(jnp.int32, (LANES, 1), 0)
    first = srow < ML_QK
    ones = jnp.ones((L, ML_V), BF16)
    c_in = [c_ref[pr] for pr in range(H // 2)]
    qt_own, kt_own, st = [], [], []
    for pr in range(H // 2):
        psl = slice(pr * LANES, (pr + 1) * LANES)
        zero = jnp.zeros((LANES, L), BF16)
        qt_own += [jnp.where(first, qt[psl, :], zero), jnp.where(first, zero, qt[psl, :])]
        kt_own += [jnp.where(first, kt[psl, :], zero), jnp.where(first, zero, kt[psl, :])]
        st.append(_dot(k[:, psl], jnp.concatenate(qt_own[-2:], axis=1)))
    lhs_t = []
    for hd in range(H):
        row = lambda a: a[hd:hd + 1, :]
        c_aug = c_in[hd // 2]
        w = jnp.where(keep_t, jnp.exp(u_all[:, hd * LANES:(hd + 1) * LANES] - row(mx)), 0.0)
        sw = st[hd // 2][:, (hd % 2) * L:(hd % 2 + 1) * L] * w
        q_f32 = qt_own[hd].astype(F32)
        nq = jnp.sum(c_aug[:, ML_V:] * q_f32, axis=0, keepdims=True)
        den = jnp.sum(sw, axis=0, keepdims=True) + row(w_inter) * nq
        inv = 1.0 / jnp.maximum(jnp.abs(den), row(e_negm))
        lhs_t.append(jnp.concatenate(
            [(sw * inv).astype(BF16), (q_f32 * (row(w_inter) * inv)).astype(BF16)], axis=0))
    outs = []
    for hd in range(H):
        v_h = v[:, hd * ML_V:(hd + 1) * ML_V]
        rhs = jnp.concatenate([v_h, c_in[hd // 2][:, :ML_V].astype(BF16)], axis=0)
        outs.append(_dot_tn(lhs_t[hd], rhs))
    for pr in range(H // 2):
        update = jnp.zeros(c_in[pr].shape, F32)
        for hd in (2 * pr, 2 * pr + 1):
            wkt = (kt_own[hd].astype(F32) * w_src[hd:hd + 1, :]).astype(BF16)
            v_aug = jnp.concatenate([v[:, hd * ML_V:(hd + 1) * ML_V], ones], axis=1)
            update = update + _dot(wkt, v_aug)
        dec = jnp.where(first, decay[2 * pr:2 * pr + 1, :], decay[2 * pr + 1:2 * pr + 2, :])
        c_ref[pr] = c_in[pr] * jnp.concatenate([dec, dec], axis=1) + update
    return jnp.concatenate(outs, axis=1)


def _mlstm_kernel(kf_ref, ktf_ref, qtf_ref, vf_ref, gf_ref, gtf_ref,
                  kb_ref, ktb_ref, qtb_ref, vb_ref, gb_ref, gtb_ref, self_ref, selb_ref,
                  hf_ref, hb_ref, cf_ref, cb_ref, mf_ref, mb_ref):
    @pl.when(pl.program_id(1) == 0)
    def _():
        for ref in (cf_ref, cb_ref, mf_ref, mb_ref):
            ref[...] = jnp.zeros(ref.shape, F32)

    h = ML_N_HEADS
    hf_ref[...] = _mlstm_direction(
        kf_ref[...], ktf_ref[...], qtf_ref[...], vf_ref[...], gf_ref[...], gtf_ref[...], self_ref[...],
        cf_ref, mf_ref, i_off=0, f_off=2 * h, reverse=False)
    hb_ref[...] = _mlstm_direction(
        kb_ref[...], ktb_ref[...], qtb_ref[...], vb_ref[...], gb_ref[...], gtb_ref[...], selb_ref[...],
        cb_ref, mb_ref, i_off=h, f_off=3 * h, reverse=True)


def _mlstm(qt, k, kt, v, gates, gatest, sel_f, sel_b):
    b, s, n_qk = k.shape
    n_v = v.shape[2]
    n_g = gatest.shape[1]
    L = SCAN_CHUNK
    nc = s // L
    fwd = lambda w: pl.BlockSpec((None, L, w), lambda bi, i: (bi, i, 0))
    bwd = lambda w: pl.BlockSpec((None, L, w), lambda bi, i: (bi, nc - 1 - i, 0))
    fwd_t = lambda c: pl.BlockSpec((None, c, L), lambda bi, i: (bi, 0, i))
    bwd_t = lambda c: pl.BlockSpec((None, c, L), lambda bi, i: (bi, 0, nc - 1 - i))
    state = pltpu.VMEM((ML_N_HEADS // 2, 2 * ML_QK, 2 * ML_V), F32)
    stab = pltpu.VMEM((ML_N_HEADS, LANES), F32)
    return pl.pallas_call(
        _mlstm_kernel,
        out_shape=[jax.ShapeDtypeStruct((b, s, n_v), F32)] * 2,
        grid=(b, nc),
        in_specs=[fwd(n_qk), fwd_t(n_qk), fwd_t(n_qk), fwd(n_v), fwd(LANES), fwd_t(n_g),
                  bwd(n_qk), bwd_t(n_qk), bwd_t(n_qk), bwd(n_v), bwd(LANES), bwd_t(n_g),
                  _const_spec(sel_f.shape), _const_spec(sel_b.shape)],
        out_specs=[fwd(n_v), bwd(n_v)],
        scratch_shapes=[state, state, stab, stab],
        compiler_params=_params(("parallel", "arbitrary")),
        name="mlstm_scan",
    )(k, kt, qt, v, gates, gatest, k, kt, qt, v, gates, gatest, sel_f, sel_b)


def _odd_out_kernel(x_ref, hf_ref, hb_ref, o_ref, g_ref, w_ref, out_ref):
    hs = hf_ref[...] + hb_ref[...]
    normed = []
    for hd in range(ML_N_HEADS):
        seg = hs[:, hd * ML_V:(hd + 1) * ML_V]
        normed.append(seg * lax.rsqrt(jnp.mean(seg * seg, axis=-1, keepdims=True) + EPS))
    gated = (_sigmoid(o_ref[...]) * (jnp.concatenate(normed, axis=1) * g_ref[...])).astype(BF16)
    out_ref[...] = x_ref[...] + _dot(gated, w_ref[...])


def _odd_out(x, hf, hb, o, g, w):
    t, d = x.shape
    n_v = hf.shape[1]
    tm = TOKEN_TILE
    tok = lambda c: pl.BlockSpec((tm, c), lambda i: (i, 0))
    return pl.pallas_call(
        _odd_out_kernel,
        out_shape=jax.ShapeDtypeStruct((t, d), F32),
        grid=(t // tm,),
        in_specs=[tok(d), tok(n_v), tok(n_v), tok(n_v), _const_spec(g.shape), _const_spec(w.shape)],
        out_specs=tok(d),
        compiler_params=_params(("parallel",)),
        name="odd_out",
    )(x, hf, hb, o, g, w)


def _row(v):
    return v.reshape(1, -1).astype(F32)


def _pad_lanes(v, width=LANES):
    return jnp.pad(v, [(0, 0)] * (v.ndim - 1) + [(0, width - v.shape[-1])])


def _swap_halves(v):
    half = v.shape[-1] // 2
    return jnp.concatenate([v[..., half:], v[..., :half]], axis=-1)


def _on_rope_lanes(v):
    return jnp.pad(v, [(0, 0)] * (v.ndim - 1) + [(MLA_NOPE, LANES - MLA_NOPE - v.shape[-1])])


def _prep_even(ev_w_in, ev_w_out, conv_w, conv_b, dt_bias, a_log, d_skip, ssd_norm,
               q_a_norm, w_q_b, kv_a_norm, w_kv_b, q_norm, k_norm):
    d_inner = ssd_norm.shape[0]
    d_xbc = conv_b.shape[0]
    n_heads = a_log.shape[1]
    c0, c1 = d_inner, d_inner + d_xbc
    c2 = c1 + 2 * n_heads
    c3 = c2 + MLA_Q_RANK
    c4 = c3 + MLA_KV_RANK
    w_kpe = ev_w_in[:, c4:]
    small = jnp.concatenate([_pad_lanes(ev_w_in[:, c1:c2], MLA_NOPE), _pad_lanes(w_kpe, LANES - MLA_NOPE)], axis=1)
    w_in = jnp.concatenate([ev_w_in[:, :c1], ev_w_in[:, c2:c4], small,
                            _on_rope_lanes(_swap_halves(w_kpe))], axis=1).astype(BF16)
    d_qk = MLA_NOPE + MLA_ROPE
    wq3 = w_q_b.reshape(MLA_Q_RANK, MLA_N_HEADS, d_qk)
    wq = jnp.concatenate(
        [_pad_lanes(wq3).reshape(MLA_Q_RANK, -1),
         _on_rope_lanes(_swap_halves(wq3[:, :, MLA_NOPE:])).reshape(MLA_Q_RANK, -1)], axis=1).astype(BF16)
    q_scale = d_qk ** -0.5 * LOG2_E
    ones_blk = jnp.ones((LANES, LANES), BF16)
    zero_blk = jnp.zeros((LANES, LANES), BF16)
    pair_ones = jnp.concatenate([jnp.concatenate([ones_blk, zero_blk], axis=1),
                                 jnp.concatenate([zero_blk, ones_blk], axis=1)], axis=0)
    wkv = w_kv_b.reshape(MLA_KV_RANK, MLA_N_HEADS, MLA_NOPE + MLA_V)
    wkvb = jnp.concatenate(
        [_pad_lanes(wkv[:, :, :MLA_NOPE]).reshape(MLA_KV_RANK, -1),
         wkv[:, :, MLA_NOPE:].reshape(MLA_KV_RANK, -1)], axis=1).astype(BF16)
    head_of_lane = jnp.arange(d_inner) // SSD_HEAD_DIM
    e_f = (jnp.arange(LANES)[:, None] == head_of_lane[None, :]).astype(BF16)
    e_b = (jnp.arange(LANES)[:, None] == head_of_lane[None, :] + n_heads).astype(BF16)
    return dict(
        d_inner=d_inner, d_xbc=d_xbc, w_in=w_in, w_out=ev_w_out.astype(BF16),
        conv_w=_pad_lanes(conv_w.T, SUBLANES).T.astype(F32), conv_b=_row(conv_b),
        dt_bias=_pad_lanes(_row(dt_bias)), alog_row=_pad_lanes(_row(a_log)),
        alog_col=a_log.reshape(-1, 1).astype(F32), d_skip=_row(jnp.repeat(d_skip, SSD_HEAD_DIM)),
        ssd_norm=_row(ssd_norm), e_f=e_f, e_b=e_b,
        q_a_norm=_row(q_a_norm), wq=wq, kv_a_norm=_row(kv_a_norm), wkvb=wkvb, pair_ones=pair_ones,
        q_gain_cos=_pad_lanes(_row(q_norm)) * q_scale,
        q_gain_sin=_on_rope_lanes(_swap_halves(_row(q_norm)[:, MLA_NOPE:])) * q_scale,
        k_gain_cos=_pad_lanes(_row(k_norm)),
        k_gain_sin=_on_rope_lanes(_swap_halves(_row(k_norm)[:, MLA_NOPE:])))


def _rope_tables(s):
    pos = jnp.arange(s, dtype=F32)
    inv_freq = jnp.power(ROPE_BASE, -jnp.arange(0, MLA_ROPE, 2, dtype=F32) / MLA_ROPE)
    freqs = pos[:, None] * inv_freq[None, :]
    cos, sin = jnp.cos(freqs), jnp.sin(freqs)
    tail = jnp.ones((s, LANES - MLA_NOPE - MLA_ROPE), F32)
    cos_t = jnp.concatenate([jnp.ones((s, MLA_NOPE), F32), cos, cos, tail], axis=1)
    sin_t = _on_rope_lanes(jnp.concatenate([-sin, sin], axis=1))
    return cos_t, sin_t


def _even_mixer(x, norm_g, p):
    b, s, d = x.shape
    z, xbc, dt, dtt, q, k, vt = _even_in(x, norm_g, p, *_rope_tables(s))
    xact = _conv(xbc, p["conv_w"], p["conv_b"])
    yf, yb = _ssd(xact, dt, dtt, p["alog_row"], p["alog_col"], p["d_skip"], p["e_f"], p["e_b"],
                  d_inner=p["d_inner"])
    o = _attention(q, k, vt)
    flat = lambda a: a.reshape(b * s, a.shape[-1])
    return _even_out(flat(x), flat(yf), flat(yb), flat(z), flat(o), p["ssd_norm"],
                     p["w_out"]).reshape(b, s, d)


def _prep_odd(od_w_in, od_w_out, ig_bias, fg_bias, ml_norm):
    n_qk = ML_N_HEADS * ML_QK
    n_main = 2 * n_qk + 2 * ML_N_HEADS * ML_V
    w_in = jnp.concatenate([od_w_in[:, :n_qk] * ML_QK ** -0.5, od_w_in[:, n_qk:n_main],
                            _pad_lanes(od_w_in[:, n_main:])], axis=1).astype(BF16)
    gate_bias = _pad_lanes(_row(jnp.concatenate([ig_bias.reshape(-1), fg_bias.reshape(-1)])))
    h = ML_N_HEADS
    head_of_col = jnp.arange(h * LANES) // LANES
    rows = jnp.arange(2 * LANES)[:, None]

    def sel(i_off, f_off):
        return ((rows == head_of_col[None, :] + i_off).astype(F32)
                - (rows == head_of_col[None, :] + LANES + f_off).astype(F32)).astype(BF16)

    return dict(w_in=w_in, w_out=od_w_out.astype(BF16), gate_bias=gate_bias, ml_norm=_row(ml_norm),
                sel_f=sel(0, 2 * h), sel_b=sel(h, 3 * h))


def _odd_mixer(x, norm_g, p):
    b, s, d = x.shape
    qt, k, kt, v, o, gates, gatest = _odd_in(x, norm_g, p["w_in"], p["gate_bias"])
    hf, hb = _mlstm(qt, k, kt, v, gates, gatest, p["sel_f"], p["sel_b"])
    flat = lambda a: a.reshape(b * s, a.shape[-1])
    return _odd_out(flat(x), flat(hf), flat(hb), flat(o), p["ml_norm"], p["w_out"]).reshape(b, s, d)


def kernel(x_prompt, x_sample, ffn1_norm, ffn1_w_in, ffn1_w_out, mix_norm, ffn2_norm, ffn2_w_in, ffn2_w_out,
           ev_w_in, ev_w_out, ssd_conv_w, ssd_conv_b, ssd_dt_bias, ssd_a_log, ssd_d_skip, ssd_norm,
           mla_q_a_norm, mla_w_q_b, mla_kv_a_norm, mla_w_kv_b, mla_q_norm, mla_k_norm,
           od_w_in, od_w_out, ml_ig_bias, ml_fg_bias, ml_norm):
    depth = ffn1_norm.shape[0]
    layers = []
    for layer in range(depth):
        j = layer // 2
        if layer % 2 == 0:
            mixer = functools.partial(_even_mixer, p=_prep_even(
                ev_w_in[j], ev_w_out[j], ssd_conv_w[j], ssd_conv_b[j], ssd_dt_bias[j], ssd_a_log[j],
                ssd_d_skip[j], ssd_norm[j], mla_q_a_norm[j], mla_w_q_b[j], mla_kv_a_norm[j],
                mla_w_kv_b[j], mla_q_norm[j], mla_k_norm[j]))
        else:
            mixer = functools.partial(_odd_mixer, p=_prep_odd(
                od_w_in[j], od_w_out[j], ml_ig_bias[j], ml_fg_bias[j], ml_norm[j]))
        layers.append(dict(
            mixer=mixer, mix_norm=_row(mix_norm[layer]),
            ffn1=(_row(ffn1_norm[layer]), ffn1_w_in[layer].astype(BF16), ffn1_w_out[layer].astype(BF16)),
            ffn2=(_row(ffn2_norm[layer]), ffn2_w_in[layer].astype(BF16), ffn2_w_out[layer].astype(BF16))))

    def trunk(x):
        b, s, d = x.shape
        for lp in layers:
            x = _ffn(x.reshape(b * s, d), *lp["ffn1"]).reshape(b, s, d)
            x = lp["mixer"](x, lp["mix_norm"])
            x = _ffn(x.reshape(b * s, d), *lp["ffn2"]).reshape(b, s, d)
        return x

    return trunk(x_prompt), trunk(x_sample)
```

```python
import functools

import jax
import jax.numpy as jnp
from jax import lax
from jax.experimental import pallas as pl
from jax.experimental.pallas import tpu as pltpu

F32 = jnp.float32
BF16 = jnp.bfloat16
EPS = 1e-6

LANES = 128
SUBLANES = 8
VMEM_LIMIT = 56 * 1024 * 1024

SSD_HEAD_DIM = 64
SSD_N_GROUPS = 2
SSD_D_STATE = 128
SSD_CONV = 5
MLA_N_HEADS = 8
MLA_Q_RANK = 384
MLA_KV_RANK = 256
MLA_NOPE = 64
MLA_ROPE = 32
MLA_V = 64
MLA_VT_ROWS = MLA_V + 16
ROPE_BASE = 10000.0
LOG2_E = 1.4426950408889634
ML_N_HEADS = 8
ML_QK = 64
ML_V = 128

TOKEN_TILE = 512
SCAN_CHUNK = 128
ATTN_Q_TILE = 512
HALO = SUBLANES


def _params(semantics):
    return pltpu.CompilerParams(dimension_semantics=semantics, vmem_limit_bytes=VMEM_LIMIT)


def _const_spec(shape):
    nd = len(shape)
    return pl.BlockSpec(shape, lambda *_: (0,) * nd, pipeline_mode=pl.Buffered(1))


def _rms(x, g):
    ms = jnp.mean(x * x, axis=-1, keepdims=True)
    return x * lax.rsqrt(ms + EPS) * g


def _sigmoid(x):
    return 1.0 / (1.0 + jnp.exp(-x))


def _softplus(x):
    return jnp.maximum(x, 0.0) + jnp.log1p(jnp.exp(-jnp.abs(x)))


def _dot(a, b):
    return jnp.dot(a, b, preferred_element_type=F32)


def _dot_nt(a, b):
    return lax.dot_general(a, b, (((1,), (1,)), ((), ())), preferred_element_type=F32)


def _dot_tn(a, b):
    return lax.dot_general(a, b, (((0,), (0,)), ((), ())), preferred_element_type=F32)


def _split_terms(x, terms):
    out = []
    for _ in range(terms):
        piece = x.astype(BF16)
        out.append(piece)
        x = x - piece.astype(F32)
    return out


def _scan_masks(n, reverse):
    row = lax.broadcasted_iota(jnp.int32, (n, n), 0)
    col = lax.broadcasted_iota(jnp.int32, (n, n), 1)
    if reverse:
        return col >= row, col <= row
    return col <= row, col >= row


def _ffn_kernel(x_ref, g_ref, win_ref, wout_ref, o_ref, *, d_ff):
    x = x_ref[...]
    h = _rms(x, g_ref[...]).astype(BF16)
    hw = _dot(h, win_ref[...])
    gate = hw[:, :d_ff]
    up = hw[:, d_ff:]
    a = (gate * _sigmoid(gate) * up).astype(BF16)
    o_ref[...] = x + 0.5 * _dot(a, wout_ref[...])


def _ffn(x, g, w_in, w_out):
    t, d = x.shape
    d_ff = w_out.shape[0]
    tm = TOKEN_TILE
    return pl.pallas_call(
        functools.partial(_ffn_kernel, d_ff=d_ff),
        out_shape=jax.ShapeDtypeStruct((t, d), F32),
        grid=(t // tm,),
        in_specs=[
            pl.BlockSpec((tm, d), lambda i: (i, 0)),
            _const_spec(g.shape), _const_spec(w_in.shape), _const_spec(w_out.shape),
        ],
        out_specs=pl.BlockSpec((tm, d), lambda i: (i, 0)),
        compiler_params=_params(("parallel",)),
        name="ffn",
    )(x, g, w_in, w_out)


def _even_in_kernel(x_ref, g_ref, w_ref, dtb_ref, qan_ref, wq_ref, kvan_ref, wkvb_ref,
                    qgc_ref, qgs_ref, kgc_ref, kgs_ref, ones_ref, cos_ref, sin_ref,
                    z_ref, xbc_ref, dt_ref, dtt_ref, q_ref, k_ref, vt_ref,
                    *, d_inner, d_xbc):
    x = x_ref[...]
    h = _rms(x, g_ref[...]).astype(BF16)
    hw = _dot(h, w_ref[...])
    c0 = d_inner
    c1 = c0 + d_xbc
    c2 = c1 + MLA_Q_RANK
    c3 = c2 + MLA_KV_RANK
    z_ref[...] = hw[:, :c0].astype(BF16)
    xbc_ref[...] = hw[:, c0:c1]
    small = hw[:, c3:c3 + LANES]
    kpe_sw = hw[:, c3 + LANES:c3 + 2 * LANES]
    dt = _softplus(small + dtb_ref[...])
    dt_ref[...] = dt
    dtt_ref[...] = dt.T[:dtt_ref.shape[0], :]

    n_k = MLA_N_HEADS * LANES
    qa = _rms(hw[:, c1:c2], qan_ref[...]).astype(BF16)
    qq = _dot(qa, wq_ref[...])
    kva = _rms(hw[:, c2:c3], kvan_ref[...]).astype(BF16)
    kv = _dot(kva, wkvb_ref[...])
    v_t = kv[:, n_k:].T
    ones_rows = jnp.ones((MLA_VT_ROWS - MLA_V, v_t.shape[1]), F32)
    vt_ref[...] = jnp.concatenate(
        [blk for hd in range(MLA_N_HEADS) for blk in (v_t[hd * MLA_V:(hd + 1) * MLA_V, :], ones_rows)],
        axis=0).astype(BF16)

    lane = lax.broadcasted_iota(jnp.int32, (1, LANES), 1)
    pe_lanes = (lane >= MLA_NOPE) & (lane < MLA_NOPE + MLA_ROPE)
    kpe = jnp.where(pe_lanes, small, 0.0)
    cos_t, sin_t = cos_ref[...], sin_ref[...]
    q_cos, q_sin = cos_t * qgc_ref[...], sin_t * qgs_ref[...]
    k_cos, k_sin = cos_t * kgc_ref[...], sin_t * kgs_ref[...]
    k_rot = kpe_sw * k_sin
    inv_dim = 1.0 / (MLA_NOPE + MLA_ROPE)

    q_blk = [qq[:, hd * LANES:(hd + 1) * LANES] for hd in range(MLA_N_HEADS)]
    k_blk = [kv[:, hd * LANES:(hd + 1) * LANES] + kpe for hd in range(MLA_N_HEADS)]
    q_ss, k_ss = [], []
    for pr in range(MLA_N_HEADS // 2):
        for blk, out in ((q_blk, q_ss), (k_blk, k_ss)):
            sq = jnp.concatenate([blk[2 * pr] * blk[2 * pr], blk[2 * pr + 1] * blk[2 * pr + 1]], axis=1)
            ss = _dot(sq.astype(BF16), ones_ref[...])
            out.extend([ss[:, :LANES], ss[:, LANES:]])
    for hd in range(MLA_N_HEADS):
        sl = slice(hd * LANES, (hd + 1) * LANES)
        q_rot = qq[:, n_k + hd * LANES:n_k + (hd + 1) * LANES]
        rq = lax.rsqrt(q_ss[hd] * inv_dim + EPS)
        q_ref[:, sl] = ((q_blk[hd] * q_cos + q_rot * q_sin) * rq).astype(BF16)
        rk = lax.rsqrt(k_ss[hd] * inv_dim + EPS)
        k_ref[:, sl] = ((k_blk[hd] * k_cos + k_rot) * rk).astype(BF16)


def _even_in(x, g, p, cos_t, sin_t):
    b, s, d = x.shape
    tm = TOKEN_TILE
    d_inner, d_xbc = p["d_inner"], p["d_xbc"]
    n_q = MLA_N_HEADS * LANES
    n_v = MLA_N_HEADS * MLA_VT_ROWS
    n_dt = 2 * (d_inner // SSD_HEAD_DIM)
    tok = lambda c: pl.BlockSpec((None, tm, c), lambda bi, i: (bi, i, 0))
    tab = pl.BlockSpec((tm, LANES), lambda bi, i: (i, 0))
    consts = [g, p["w_in"], p["dt_bias"], p["q_a_norm"], p["wq"], p["kv_a_norm"], p["wkvb"],
              p["q_gain_cos"], p["q_gain_sin"], p["k_gain_cos"], p["k_gain_sin"], p["pair_ones"]]
    return pl.pallas_call(
        functools.partial(_even_in_kernel, d_inner=d_inner, d_xbc=d_xbc),
        out_shape=[
            jax.ShapeDtypeStruct((b, s, d_inner), BF16),
            jax.ShapeDtypeStruct((b, s, d_xbc), F32),
            jax.ShapeDtypeStruct((b, s, LANES), F32),
            jax.ShapeDtypeStruct((b, n_dt, s), F32),
            jax.ShapeDtypeStruct((b, s, n_q), BF16),
            jax.ShapeDtypeStruct((b, s, n_q), BF16),
            jax.ShapeDtypeStruct((b, n_v, s), BF16),
        ],
        grid=(b, s // tm),
        in_specs=[tok(d)] + [_const_spec(c.shape) for c in consts] + [tab, tab],
        out_specs=[
            tok(d_inner), tok(d_xbc), tok(LANES),
            pl.BlockSpec((None, n_dt, tm), lambda bi, i: (bi, 0, i)),
            tok(n_q), tok(n_q),
            pl.BlockSpec((None, n_v, tm), lambda bi, i: (bi, 0, i)),
        ],
        compiler_params=_params(("parallel", "parallel")),
        name="even_in",
    )(x, *consts, cos_t, sin_t)


def _conv_kernel(main_ref, prev_ref, next_ref, w_ref, b_ref, o_ref, ext_ref, *, n_tiles):
    i = pl.program_id(1)
    tc = main_ref.shape[0]
    zero = jnp.zeros(prev_ref.shape, F32)
    ext_ref[0:HALO, :] = jnp.where(i > 0, prev_ref[...], zero)
    ext_ref[HALO:HALO + tc, :] = main_ref[...]
    ext_ref[HALO + tc:, :] = jnp.where(i < n_tiles - 1, next_ref[...], zero)
    pad = SSD_CONV // 2
    ext = ext_ref[...]
    n = tc + 2 * HALO
    acc = b_ref[...] + w_ref[pad:pad + 1, :] * ext[HALO:HALO + tc, :]
    for k in range(SSD_CONV):
        if k != pad:
            acc = acc + w_ref[k:k + 1, :] * pltpu.roll(ext, (pad - k) % n, axis=0)[HALO:HALO + tc, :]
    o_ref[...] = acc * _sigmoid(acc)


def _conv(xbc, conv_w, conv_b):
    b, s, c = xbc.shape
    tc = TOKEN_TILE
    n_tiles = s // tc
    per = tc // HALO
    n_halo = s // HALO
    return pl.pallas_call(
        functools.partial(_conv_kernel, n_tiles=n_tiles),
        out_shape=jax.ShapeDtypeStruct((b, s, c), F32),
        grid=(b, n_tiles),
        in_specs=[
            pl.BlockSpec((None, tc, c), lambda bi, i: (bi, i, 0)),
            pl.BlockSpec((None, HALO, c), lambda bi, i: (bi, jnp.maximum(i * per - 1, 0), 0)),
            pl.BlockSpec((None, HALO, c),
                         lambda bi, i: (bi, jnp.minimum((i + 1) * per, n_halo - 1), 0)),
            _const_spec(conv_w.shape), _const_spec(conv_b.shape),
        ],
        out_specs=pl.BlockSpec((None, tc, c), lambda bi, i: (bi, i, 0)),
        scratch_shapes=[pltpu.VMEM((tc + 2 * HALO, c), F32)],
        compiler_params=_params(("parallel", "parallel")),
        name="ssd_conv",
    )(xbc, xbc, xbc, conv_w, conv_b)


def _ssd_stats(dt, dtt, a_row, a_col, expand, *, reverse):
    L = dt.shape[0]
    keep, keep_t = _scan_masks(L, reverse)
    last = 0 if reverse else L - 1
    keep_bf = keep.astype(F32).astype(BF16)
    keep_t_bf = keep_t.astype(F32).astype(BF16)
    cum = sum(_dot(keep_bf, t) for t in _split_terms(dt * a_row, 3))
    cumt = sum(_dot(t, keep_t_bf) for t in _split_terms(dtt * a_col, 3))
    cum_last = cum[last:last + 1, :]
    narrow = jnp.concatenate(
        [jnp.exp(cum), dt * jnp.exp(cum_last - cum),
         jnp.broadcast_to(jnp.exp(cum_last), (SUBLANES, LANES))], axis=0)
    wide = _dot(narrow.astype(BF16), expand)
    return dict(cum=cum, cumt=cumt, keep=keep, e_off=wide[:L], w_state=wide[L:2 * L],
                chunk_decay=wide[2 * L:2 * L + 1])


def _ssd_main(xbc, dtt, stats, s_ref, d_skip, *, lane_off, d_inner):
    n_heads = d_inner // SSD_HEAD_DIM
    hpg = n_heads // SSD_N_GROUPS
    gw = hpg * SSD_HEAD_DIM
    gn = SSD_D_STATE
    xs = xbc[:, :d_inner]
    cum, cumt, keep = stats["cum"], stats["cumt"], stats["keep"]
    e_off, w_state, chunk_decay = stats["e_off"], stats["w_state"], stats["chunk_decay"]
    lane = lax.broadcasted_iota(jnp.int32, (1, LANES), 1)
    lo_lanes = lane < SSD_HEAD_DIM
    parts = []
    for g in range(SSD_N_GROUPS):
        gsl = slice(g * gw, (g + 1) * gw)
        b_g = xbc[:, d_inner + g * gn:d_inner + (g + 1) * gn]
        c_off = d_inner + SSD_N_GROUPS * gn
        c_bf = xbc[:, c_off + g * gn:c_off + (g + 1) * gn].astype(BF16)
        cb = _dot_nt(c_bf, b_g.astype(BF16))
        state = s_ref[g]
        y_off = _dot(c_bf, state.astype(BF16)) * e_off[:, gsl]
        xw = (xs[:, gsl] * w_state[:, gsl]).astype(BF16)
        s_ref[g] = state * chunk_decay[:, gsl] + _dot(b_g.T.astype(BF16), xw)
        for pr in range(hpg // 2):
            m_pair = []
            for j in range(2):
                hd = g * hpg + 2 * pr + j
                seg = cum[:, lane_off + hd:lane_off + hd + 1] - cumt[hd:hd + 1, :]
                m_h = jnp.where(keep, cb * jnp.exp(seg) * dtt[hd:hd + 1, :], 0.0)
                m_pair.append(m_h.astype(BF16))
            psl = slice(g * gw + pr * LANES, g * gw + (pr + 1) * LANES)
            x_pair = xs[:, psl]
            rhs = jnp.concatenate([jnp.where(lo_lanes, x_pair, 0.0),
                                   jnp.where(lo_lanes, 0.0, x_pair)], axis=0).astype(BF16)
            y_pair = _dot(jnp.concatenate(m_pair, axis=1), rhs) + y_off[:, pr * LANES:(pr + 1) * LANES]
            if d_skip is not None:
                y_pair = y_pair + x_pair * d_skip[:, psl]
            parts.append(y_pair)
    return jnp.concatenate(parts, axis=1).astype(BF16)


def _ssd_kernel(xf_ref, xb_ref, dtf_ref, dtb_ref, dttf_ref, dttb_ref, alog_row_ref, alog_col_ref,
                dskip_ref, ef_ref, eb_ref, yf_ref, yb_ref, sf_ref, sb_ref, *, d_inner):
    @pl.when(pl.program_id(1) == 0)
    def _():
        sf_ref[...] = jnp.zeros(sf_ref.shape, F32)
        sb_ref[...] = jnp.zeros(sb_ref.shape, F32)

    n_heads = d_inner // SSD_HEAD_DIM
    lane = lax.broadcasted_iota(jnp.int32, (1, LANES), 1)
    a_all = -jnp.exp(alog_row_ref[...])
    a_col = -jnp.exp(alog_col_ref[...])
    dtt_f = dttf_ref[0:n_heads, :]
    dtt_b = dttb_ref[n_heads:2 * n_heads, :]
    stats_f = _ssd_stats(dtf_ref[...], dtt_f, jnp.where(lane < n_heads, a_all, 0.0), a_col[0:n_heads, :],
                         ef_ref[...], reverse=False)
    stats_b = _ssd_stats(dtb_ref[...], dtt_b, jnp.where((lane >= n_heads) & (lane < 2 * n_heads), a_all, 0.0),
                         a_col[n_heads:2 * n_heads, :], eb_ref[...], reverse=True)
    yf_ref[...] = _ssd_main(xf_ref[...], dtt_f, stats_f, sf_ref, dskip_ref[...], lane_off=0, d_inner=d_inner)
    yb_ref[...] = _ssd_main(xb_ref[...], dtt_b, stats_b, sb_ref, None, lane_off=n_heads, d_inner=d_inner)


def _ssd(xbc, dt, dtt, alog_row, alog_col, d_skip, e_f, e_b, *, d_inner):
    b, s, c = xbc.shape
    L = SCAN_CHUNK
    nc = s // L
    n_dt = dtt.shape[1]
    hpg_w = d_inner // SSD_N_GROUPS
    fwd = lambda w: pl.BlockSpec((None, L, w), lambda bi, i: (bi, i, 0))
    bwd = lambda w: pl.BlockSpec((None, L, w), lambda bi, i: (bi, nc - 1 - i, 0))
    return pl.pallas_call(
        functools.partial(_ssd_kernel, d_inner=d_inner),
        out_shape=[jax.ShapeDtypeStruct((b, s, d_inner), BF16)] * 2,
        grid=(b, nc),
        in_specs=[
            fwd(c), bwd(c), fwd(LANES), bwd(LANES),
            pl.BlockSpec((None, n_dt, L), lambda bi, i: (bi, 0, i)),
            pl.BlockSpec((None, n_dt, L), lambda bi, i: (bi, 0, nc - 1 - i)),
            _const_spec(alog_row.shape), _const_spec(alog_col.shape), _const_spec(d_skip.shape),
            _const_spec(e_f.shape), _const_spec(e_b.shape),
        ],
        out_specs=[fwd(d_inner), bwd(d_inner)],
        scratch_shapes=[pltpu.VMEM((SSD_N_GROUPS, SSD_D_STATE, hpg_w), F32)] * 2,
        compiler_params=_params(("parallel", "arbitrary")),
        name="ssd_scan",
    )(xbc, xbc, dt, dt, dtt, dtt, alog_row, alog_col, d_skip, e_f, e_b)


def _attn_kernel(q_ref, k_ref, vt_ref, o_ref):
    def scores(hd):
        sl = slice(hd * LANES, (hd + 1) * LANES)
        return _dot_nt(k_ref[:, sl], q_ref[:, sl])

    outs = []
    st_next = scores(0)
    for hd in range(MLA_N_HEADS):
        st = st_next
        if hd + 1 < MLA_N_HEADS:
            st_next = scores(hd + 1)
        m = jnp.max(st, axis=0, keepdims=True)
        p = jnp.exp2(st - m).astype(BF16)
        acc = _dot(vt_ref[hd * MLA_VT_ROWS:(hd + 1) * MLA_VT_ROWS, :], p)
        outs.append(acc[:MLA_V] / acc[MLA_V:MLA_V + 1])
    o_ref[...] = jnp.concatenate(outs, axis=0).T.astype(BF16)


def _attention(q, k, vt):
    b, s, n_q = q.shape
    n_vt = vt.shape[1]
    n_v = MLA_N_HEADS * MLA_V
    tq = ATTN_Q_TILE
    return pl.pallas_call(
        _attn_kernel,
        out_shape=jax.ShapeDtypeStruct((b, s, n_v), BF16),
        grid=(b, s // tq),
        in_specs=[
            pl.BlockSpec((None, tq, n_q), lambda bi, i: (bi, i, 0)),
            pl.BlockSpec((None, s, n_q), lambda bi, i: (bi, 0, 0), pipeline_mode=pl.Buffered(1)),
            pl.BlockSpec((None, n_vt, s), lambda bi, i: (bi, 0, 0), pipeline_mode=pl.Buffered(1)),
        ],
        out_specs=pl.BlockSpec((None, tq, n_v), lambda bi, i: (bi, i, 0)),
        compiler_params=_params(("parallel", "arbitrary")),
        name="mla_attention",
    )(q, k, vt)


def _even_out_kernel(x_ref, yf_ref, yb_ref, z_ref, o_ref, g_ref, w_ref, out_ref, *, d_inner):
    z = z_ref[...].astype(F32)
    y = (yf_ref[...].astype(F32) + yb_ref[...].astype(F32)) * (z * _sigmoid(z))
    gw = d_inner // SSD_N_GROUPS
    normed = []
    for g in range(SSD_N_GROUPS):
        seg = y[:, g * gw:(g + 1) * gw]
        normed.append(seg * lax.rsqrt(jnp.mean(seg * seg, axis=-1, keepdims=True) + EPS))
    yn = (jnp.concatenate(normed, axis=1) * g_ref[...]).astype(BF16)
    out_ref[...] = (x_ref[...] + _dot(yn, w_ref[0:d_inner, :])
                    + _dot(o_ref[...], w_ref[d_inner:, :]))


def _even_out(x, yf, yb, z, o, g, w):
    t, d = x.shape
    d_inner = yf.shape[1]
    tm = TOKEN_TILE
    tok = lambda c: pl.BlockSpec((tm, c), lambda i: (i, 0))
    return pl.pallas_call(
        functools.partial(_even_out_kernel, d_inner=d_inner),
        out_shape=jax.ShapeDtypeStruct((t, d), F32),
        grid=(t // tm,),
        in_specs=[tok(d), tok(d_inner), tok(d_inner), tok(d_inner), tok(o.shape[1]),
                  _const_spec(g.shape), _const_spec(w.shape)],
        out_specs=tok(d),
        compiler_params=_params(("parallel",)),
        name="even_out",
    )(x, yf, yb, z, o, g, w)


def _odd_in_kernel(x_ref, g_ref, w_ref, gb_ref, qt_ref, k_ref, kt_ref, v_ref, o_ref, gates_ref, gatest_ref):
    h = _rms(x_ref[...], g_ref[...]).astype(BF16)
    hw = _dot(h, w_ref[...])
    n_qk = ML_N_HEADS * ML_QK
    n_v = ML_N_HEADS * ML_V
    qt_ref[...] = hw[:, :n_qk].T.astype(BF16)
    k = hw[:, n_qk:2 * n_qk]
    k_ref[...] = k.astype(BF16)
    kt_ref[...] = k.T.astype(BF16)
    v_ref[...] = hw[:, 2 * n_qk:2 * n_qk + n_v].astype(BF16)
    o_ref[...] = hw[:, 2 * n_qk + n_v:2 * n_qk + 2 * n_v].astype(BF16)
    pre = hw[:, 2 * n_qk + 2 * n_v:] + gb_ref[...]
    lane = lax.broadcasted_iota(jnp.int32, (1, LANES), 1)
    gates = jnp.where(lane < 2 * ML_N_HEADS, pre, -_softplus(-pre))
    gates_ref[...] = gates
    gatest_ref[...] = gates.T[:gatest_ref.shape[0], :]


def _odd_in(x, g, w, gate_bias):
    b, s, d = x.shape
    tm = TOKEN_TILE
    n_qk = ML_N_HEADS * ML_QK
    n_v = ML_N_HEADS * ML_V
    n_g = 4 * ML_N_HEADS
    tok = lambda c: pl.BlockSpec((None, tm, c), lambda bi, i: (bi, i, 0))
    tok_t = lambda c: pl.BlockSpec((None, c, tm), lambda bi, i: (bi, 0, i))
    return pl.pallas_call(
        _odd_in_kernel,
        out_shape=[
            jax.ShapeDtypeStruct((b, n_qk, s), BF16),
            jax.ShapeDtypeStruct((b, s, n_qk), BF16),
            jax.ShapeDtypeStruct((b, n_qk, s), BF16),
            jax.ShapeDtypeStruct((b, s, n_v), BF16),
            jax.ShapeDtypeStruct((b, s, n_v), BF16),
            jax.ShapeDtypeStruct((b, s, LANES), F32),
            jax.ShapeDtypeStruct((b, n_g, s), F32),
        ],
        grid=(b, s // tm),
        in_specs=[tok(d), _const_spec(g.shape), _const_spec(w.shape), _const_spec(gate_bias.shape)],
        out_specs=[tok_t(n_qk), tok(n_qk), tok_t(n_qk), tok(n_v), tok(n_v), tok(LANES), tok_t(n_g)],
        compiler_params=_params(("parallel", "parallel")),
        name="odd_in",
    )(x, g, w, gate_bias)


def _cummax_lanes(u, reverse):
    n = u.shape[1]
    lane = lax.broadcasted_iota(jnp.int32, (1, n), 1)
    d = 1
    while d < n:
        if reverse:
            shifted, valid = pltpu.roll(u, n - d, axis=1), lane < n - d
        else:
            shifted, valid = pltpu.roll(u, d, axis=1), lane >= d
        u = jnp.where(valid, jnp.maximum(u, shifted), u)
        d *= 2
    return u


def _mlstm_stats(gates, gatest, sel_u, m_ref, *, i_off, f_off, reverse):
    L = gates.shape[0]
    H = ML_N_HEADS
    assert L == LANES == ML_V and 2 * ML_QK == LANES
    keep, keep_t = _scan_masks(L, reverse)
    keep_bf = keep.astype(F32).astype(BF16)
    keep_t_bf = keep_t.astype(F32).astype(BF16)
    last = 0 if reverse else L - 1

    li_r = gatest[i_off:i_off + H, :]
    bcum_r = sum(_dot(t, keep_t_bf) for t in _split_terms(gatest[f_off:f_off + H, :], 3))
    u_r = li_r - bcum_r
    m_in = m_ref[...]
    mx = jnp.maximum(m_in, _cummax_lanes(u_r, reverse))
    w_inter = jnp.exp(m_in - mx)
    e_negm = jnp.exp(-(bcum_r + mx))
    b_end = jnp.broadcast_to(bcum_r[:, last:last + 1], (H, L))
    g_end = b_end + u_r
    m_new = jnp.maximum(b_end + m_in, jnp.max(g_end, axis=1, keepdims=True))
    w_src = jnp.exp(g_end - m_new)
    decay = jnp.exp(b_end + m_in - m_new)
    m_ref[...] = m_new

    bcum_c = sum(_dot(keep_bf, t) for t in _split_terms(gates, 3))
    u_all = sum(_dot(t, sel_u) for t in _split_terms(jnp.concatenate([gates, bcum_c], axis=1), 2))
    return dict(mx=mx, w_inter=w_inter, e_negm=e_negm, w_src=w_src, decay=decay, u_all=u_all, keep_t=keep_t)


def _mlstm_main(k, kt, qt, v, stats, c_ref):
    L = k.shape[0]
    H = ML_N_HEADS
    mx, w_inter, e_negm, w_src = stats["mx"], stats["w_inter"], stats["e_negm"], stats["w_src"]
    decay, u_all, keep_t = stats["decay"], stats["u_all"], stats["keep_t"]
    srow = lax.broadcasted_iota(jnp.int32, (LANES, 1), 0)
    first = srow < ML_QK
    ones = jnp.ones((L, ML_V), BF16)
    c_in = [c_ref[pr] for pr in range(H // 2)]
    qt_own, kt_own, st = [], [], []
    for pr in range(H // 2):
        psl = slice(pr * LANES, (pr + 1) * LANES)
        zero = jnp.zeros((LANES, L), BF16)
        qt_own += [jnp.where(first, qt[psl, :], zero), jnp.where(first, zero, qt[psl, :])]
        kt_own += [jnp.where(first, kt[psl, :], zero), jnp.where(first, zero, kt[psl, :])]
        st.append(_dot(k[:, psl], jnp.concatenate(qt_own[-2:], axis=1)))
    lhs_t = []
    for hd in range(H):
        row = lambda a: a[hd:hd + 1, :]
        c_aug = c_in[hd // 2]
        w = jnp.where(keep_t, jnp.exp(u_all[:, hd * LANES:(hd + 1) * LANES] - row(mx)), 0.0)
        sw = st[hd // 2][:, (hd % 2) * L:(hd % 2 + 1) * L] * w
        q_f32 = qt_own[hd].astype(F32)
        nq = jnp.sum(c_aug[:, ML_V:] * q_f32, axis=0, keepdims=True)
        den = jnp.sum(sw, axis=0, keepdims=True) + row(w_inter) * nq
        inv = 1.0 / jnp.maximum(jnp.abs(den), row(e_negm))
        lhs_t.append(jnp.concatenate(
            [(sw * inv).astype(BF16), (q_f32 * (row(w_inter) * inv)).astype(BF16)], axis=0))
    outs = []
    for hd in range(H):
        v_h = v[:, hd * ML_V:(hd + 1) * ML_V]
        rhs = jnp.concatenate([v_h, c_in[hd // 2][:, :ML_V].astype(BF16)], axis=0)
        outs.append(_dot_tn(lhs_t[hd], rhs))
    for pr in range(H // 2):
        update = jnp.zeros(c_in[pr].shape, F32)
        for hd in (2 * pr, 2 * pr + 1):
            wkt = (kt_own[hd].astype(F32) * w_src[hd:hd + 1, :]).astype(BF16)
            v_aug = jnp.concatenate([v[:, hd * ML_V:(hd + 1) * ML_V], ones], axis=1)
            update = update + _dot(wkt, v_aug)
        dec = jnp.where(first, decay[2 * pr:2 * pr + 1, :], decay[2 * pr + 1:2 * pr + 2, :])
        c_ref[pr] = c_in[pr] * jnp.concatenate([dec, dec], axis=1) + update
    return jnp.concatenate(outs, axis=1).astype(BF16)


def _mlstm_kernel(kf_ref, ktf_ref, qtf_ref, vf_ref, gf_ref, gtf_ref,
                  kb_ref, ktb_ref, qtb_ref, vb_ref, gb_ref, gtb_ref, self_ref, selb_ref,
                  hf_ref, hb_ref, cf_ref, cb_ref, mf_ref, mb_ref):
    @pl.when(pl.program_id(1) == 0)
    def _():
        for ref in (cf_ref, cb_ref, mf_ref, mb_ref):
            ref[...] = jnp.zeros(ref.shape, F32)

    h = ML_N_HEADS
    stats_f = _mlstm_stats(gf_ref[...], gtf_ref[...], self_ref[...], mf_ref, i_off=0, f_off=2 * h, reverse=False)
    stats_b = _mlstm_stats(gb_ref[...], gtb_ref[...], selb_ref[...], mb_ref, i_off=h, f_off=3 * h, reverse=True)
    hf_ref[...] = _mlstm_main(kf_ref[...], ktf_ref[...], qtf_ref[...], vf_ref[...], stats_f, cf_ref)
    hb_ref[...] = _mlstm_main(kb_ref[...], ktb_ref[...], qtb_ref[...], vb_ref[...], stats_b, cb_ref)


def _mlstm(qt, k, kt, v, gates, gatest, sel_f, sel_b):
    b, s, n_qk = k.shape
    n_v = v.shape[2]
    n_g = gatest.shape[1]
    L = SCAN_CHUNK
    nc = s // L
    fwd = lambda w: pl.BlockSpec((None, L, w), lambda bi, i: (bi, i, 0))
    bwd = lambda w: pl.BlockSpec((None, L, w), lambda bi, i: (bi, nc - 1 - i, 0))
    fwd_t = lambda c: pl.BlockSpec((None, c, L), lambda bi, i: (bi, 0, i))
    bwd_t = lambda c: pl.BlockSpec((None, c, L), lambda bi, i: (bi, 0, nc - 1 - i))
    state = pltpu.VMEM((ML_N_HEADS // 2, 2 * ML_QK, 2 * ML_V), F32)
    stab = pltpu.VMEM((ML_N_HEADS, LANES), F32)
    return pl.pallas_call(
        _mlstm_kernel,
        out_shape=[jax.ShapeDtypeStruct((b, s, n_v), BF16)] * 2,
        grid=(b, nc),
        in_specs=[fwd(n_qk), fwd_t(n_qk), fwd_t(n_qk), fwd(n_v), fwd(LANES), fwd_t(n_g),
                  bwd(n_qk), bwd_t(n_qk), bwd_t(n_qk), bwd(n_v), bwd(LANES), bwd_t(n_g),
                  _const_spec(sel_f.shape), _const_spec(sel_b.shape)],
        out_specs=[fwd(n_v), bwd(n_v)],
        scratch_shapes=[state, state, stab, stab],
        compiler_params=_params(("parallel", "arbitrary")),
        name="mlstm_scan",
    )(k, kt, qt, v, gates, gatest, k, kt, qt, v, gates, gatest, sel_f, sel_b)


def _odd_out_kernel(x_ref, hf_ref, hb_ref, o_ref, g_ref, w_ref, out_ref):
    hs = hf_ref[...].astype(F32) + hb_ref[...].astype(F32)
    normed = []
    for hd in range(ML_N_HEADS):
        seg = hs[:, hd * ML_V:(hd + 1) * ML_V]
        normed.append(seg * lax.rsqrt(jnp.mean(seg * seg, axis=-1, keepdims=True) + EPS))
    gated = (_sigmoid(o_ref[...].astype(F32)) * (jnp.concatenate(normed, axis=1) * g_ref[...])).astype(BF16)
    out_ref[...] = x_ref[...] + _dot(gated, w_ref[...])


def _odd_out(x, hf, hb, o, g, w):
    t, d = x.shape
    n_v = hf.shape[1]
    tm = TOKEN_TILE
    tok = lambda c: pl.BlockSpec((tm, c), lambda i: (i, 0))
    return pl.pallas_call(
        _odd_out_kernel,
        out_shape=jax.ShapeDtypeStruct((t, d), F32),
        grid=(t // tm,),
        in_specs=[tok(d), tok(n_v), tok(n_v), tok(n_v), _const_spec(g.shape), _const_spec(w.shape)],
        out_specs=tok(d),
        compiler_params=_params(("parallel",)),
        name="odd_out",
    )(x, hf, hb, o, g, w)


def _row(v):
    return v.reshape(1, -1).astype(F32)


def _pad_lanes(v, width=LANES):
    return jnp.pad(v, [(0, 0)] * (v.ndim - 1) + [(0, width - v.shape[-1])])


def _swap_halves(v):
    half = v.shape[-1] // 2
    return jnp.concatenate([v[..., half:], v[..., :half]], axis=-1)


def _on_rope_lanes(v):
    return jnp.pad(v, [(0, 0)] * (v.ndim - 1) + [(MLA_NOPE, LANES - MLA_NOPE - v.shape[-1])])


def _prep_even(ev_w_in, ev_w_out, conv_w, conv_b, dt_bias, a_log, d_skip, ssd_norm,
               q_a_norm, w_q_b, kv_a_norm, w_kv_b, q_norm, k_norm):
    d_inner = ssd_norm.shape[0]
    d_xbc = conv_b.shape[0]
    n_heads = a_log.shape[1]
    c0, c1 = d_inner, d_inner + d_xbc
    c2 = c1 + 2 * n_heads
    c3 = c2 + MLA_Q_RANK
    c4 = c3 + MLA_KV_RANK
    w_kpe = ev_w_in[:, c4:]
    small = jnp.concatenate([_pad_lanes(ev_w_in[:, c1:c2], MLA_NOPE), _pad_lanes(w_kpe, LANES - MLA_NOPE)], axis=1)
    w_in = jnp.concatenate([ev_w_in[:, :c1], ev_w_in[:, c2:c4], small,
                            _on_rope_lanes(_swap_halves(w_kpe))], axis=1).astype(BF16)
    d_qk = MLA_NOPE + MLA_ROPE
    wq3 = w_q_b.reshape(MLA_Q_RANK, MLA_N_HEADS, d_qk)
    wq = jnp.concatenate(
        [_pad_lanes(wq3).reshape(MLA_Q_RANK, -1),
         _on_rope_lanes(_swap_halves(wq3[:, :, MLA_NOPE:])).reshape(MLA_Q_RANK, -1)], axis=1).astype(BF16)
    q_scale = d_qk ** -0.5 * LOG2_E
    ones_blk = jnp.ones((LANES, LANES), BF16)
    zero_blk = jnp.zeros((LANES, LANES), BF16)
    pair_ones = jnp.concatenate([jnp.concatenate([ones_blk, zero_blk], axis=1),
                                 jnp.concatenate([zero_blk, ones_blk], axis=1)], axis=0)
    wkv = w_kv_b.reshape(MLA_KV_RANK, MLA_N_HEADS, MLA_NOPE + MLA_V)
    wkvb = jnp.concatenate(
        [_pad_lanes(wkv[:, :, :MLA_NOPE]).reshape(MLA_KV_RANK, -1),
         wkv[:, :, MLA_NOPE:].reshape(MLA_KV_RANK, -1)], axis=1).astype(BF16)
    head_of_lane = jnp.arange(d_inner) // SSD_HEAD_DIM
    e_f = (jnp.arange(LANES)[:, None] == head_of_lane[None, :]).astype(BF16)
    e_b = (jnp.arange(LANES)[:, None] == head_of_lane[None, :] + n_heads).astype(BF16)
    return dict(
        d_inner=d_inner, d_xbc=d_xbc, w_in=w_in, w_out=ev_w_out.astype(BF16),
        conv_w=_pad_lanes(conv_w.T, SUBLANES).T.astype(F32), conv_b=_row(conv_b),
        dt_bias=_pad_lanes(_row(dt_bias)), alog_row=_pad_lanes(_row(a_log)),
        alog_col=a_log.reshape(-1, 1).astype(F32), d_skip=_row(jnp.repeat(d_skip, SSD_HEAD_DIM)),
        ssd_norm=_row(ssd_norm), e_f=e_f, e_b=e_b,
        q_a_norm=_row(q_a_norm), wq=wq, kv_a_norm=_row(kv_a_norm), wkvb=wkvb, pair_ones=pair_ones,
        q_gain_cos=_pad_lanes(_row(q_norm)) * q_scale,
        q_gain_sin=_on_rope_lanes(_swap_halves(_row(q_norm)[:, MLA_NOPE:])) * q_scale,
        k_gain_cos=_pad_lanes(_row(k_norm)),
        k_gain_sin=_on_rope_lanes(_swap_halves(_row(k_norm)[:, MLA_NOPE:])))


def _rope_tables(s):
    pos = jnp.arange(s, dtype=F32)
    inv_freq = jnp.power(ROPE_BASE, -jnp.arange(0, MLA_ROPE, 2, dtype=F32) / MLA_ROPE)
    freqs = pos[:, None] * inv_freq[None, :]
    cos, sin = jnp.cos(freqs), jnp.sin(freqs)
    tail = jnp.ones((s, LANES - MLA_NOPE - MLA_ROPE), F32)
    cos_t = jnp.concatenate([jnp.ones((s, MLA_NOPE), F32), cos, cos, tail], axis=1)
    sin_t = _on_rope_lanes(jnp.concatenate([-sin, sin], axis=1))
    return cos_t, sin_t


def _even_mixer(x, norm_g, p):
    b, s, d = x.shape
    z, xbc, dt, dtt, q, k, vt = _even_in(x, norm_g, p, *_rope_tables(s))
    xact = _conv(xbc, p["conv_w"], p["conv_b"])
    yf, yb = _ssd(xact, dt, dtt, p["alog_row"], p["alog_col"], p["d_skip"], p["e_f"], p["e_b"],
                  d_inner=p["d_inner"])
    o = _attention(q, k, vt)
    flat = lambda a: a.reshape(b * s, a.shape[-1])
    return _even_out(flat(x), flat(yf), flat(yb), flat(z), flat(o), p["ssd_norm"],
                     p["w_out"]).reshape(b, s, d)


def _prep_odd(od_w_in, od_w_out, ig_bias, fg_bias, ml_norm):
    n_qk = ML_N_HEADS * ML_QK
    n_main = 2 * n_qk + 2 * ML_N_HEADS * ML_V
    w_in = jnp.concatenate([od_w_in[:, :n_qk] * ML_QK ** -0.5, od_w_in[:, n_qk:n_main],
                            _pad_lanes(od_w_in[:, n_main:])], axis=1).astype(BF16)
    gate_bias = _pad_lanes(_row(jnp.concatenate([ig_bias.reshape(-1), fg_bias.reshape(-1)])))
    h = ML_N_HEADS
    head_of_col = jnp.arange(h * LANES) // LANES
    rows = jnp.arange(2 * LANES)[:, None]

    def sel(i_off, f_off):
        return ((rows == head_of_col[None, :] + i_off).astype(F32)
                - (rows == head_of_col[None, :] + LANES + f_off).astype(F32)).astype(BF16)

    return dict(w_in=w_in, w_out=od_w_out.astype(BF16), gate_bias=gate_bias, ml_norm=_row(ml_norm),
                sel_f=sel(0, 2 * h), sel_b=sel(h, 3 * h))


def _odd_mixer(x, norm_g, p):
    b, s, d = x.shape
    qt, k, kt, v, o, gates, gatest = _odd_in(x, norm_g, p["w_in"], p["gate_bias"])
    hf, hb = _mlstm(qt, k, kt, v, gates, gatest, p["sel_f"], p["sel_b"])
    flat = lambda a: a.reshape(b * s, a.shape[-1])
    return _odd_out(flat(x), flat(hf), flat(hb), flat(o), p["ml_norm"], p["w_out"]).reshape(b, s, d)


def kernel(x_prompt, x_sample, ffn1_norm, ffn1_w_in, ffn1_w_out, mix_norm, ffn2_norm, ffn2_w_in, ffn2_w_out,
           ev_w_in, ev_w_out, ssd_conv_w, ssd_conv_b, ssd_dt_bias, ssd_a_log, ssd_d_skip, ssd_norm,
           mla_q_a_norm, mla_w_q_b, mla_kv_a_norm, mla_w_kv_b, mla_q_norm, mla_k_norm,
           od_w_in, od_w_out, ml_ig_bias, ml_fg_bias, ml_norm):
    depth = ffn1_norm.shape[0]
    layers = []
    for layer in range(depth):
        j = layer // 2
        if layer % 2 == 0:
            mixer = functools.partial(_even_mixer, p=_prep_even(
                ev_w_in[j], ev_w_out[j], ssd_conv_w[j], ssd_conv_b[j], ssd_dt_bias[j], ssd_a_log[j],
                ssd_d_skip[j], ssd_norm[j], mla_q_a_norm[j], mla_w_q_b[j], mla_kv_a_norm[j],
                mla_w_kv_b[j], mla_q_norm[j], mla_k_norm[j]))
        else:
            mixer = functools.partial(_odd_mixer, p=_prep_odd(
                od_w_in[j], od_w_out[j], ml_ig_bias[j], ml_fg_bias[j], ml_norm[j]))
        layers.append(dict(
            mixer=mixer, mix_norm=_row(mix_norm[layer]),
            ffn1=(_row(ffn1_norm[layer]), ffn1_w_in[layer].astype(BF16), ffn1_w_out[layer].astype(BF16)),
            ffn2=(_row(ffn2_norm[layer]), ffn2_w_in[layer].astype(BF16), ffn2_w_out[layer].astype(BF16))))

    def trunk(x):
        b, s, d = x.shape
        for lp in layers:
            x = _ffn(x.reshape(b * s, d), *lp["ffn1"]).reshape(b, s, d)
            x = lp["mixer"](x, lp["mix_norm"])
            x = _ffn(x.reshape(b * s, d), *lp["ffn2"]).reshape(b, s, d)
        return x

    return trunk(x_prompt), trunk(x_sample)
```

```python
import functools

import jax
import jax.numpy as jnp
from jax import lax
from jax.experimental import pallas as pl
from jax.experimental.pallas import tpu as pltpu

F32 = jnp.float32
BF16 = jnp.bfloat16
EPS = 1e-6

LANES = 128
SUBLANES = 8
VMEM_LIMIT = 56 * 1024 * 1024

SSD_HEAD_DIM = 64
SSD_N_GROUPS = 2
SSD_D_STATE = 128
SSD_CONV = 5
MLA_N_HEADS = 8
MLA_Q_RANK = 384
MLA_KV_RANK = 256
MLA_NOPE = 64
MLA_ROPE = 32
MLA_V = 64
MLA_VT_ROWS = MLA_V + 16
ROPE_BASE = 10000.0
LOG2_E = 1.4426950408889634
ML_N_HEADS = 8
ML_QK = 64
ML_V = 128

TOKEN_TILE = 512
SCAN_CHUNK = 128
SCAN_STEP_CHUNKS = 2
ATTN_Q_TILE = 512
HALO = SUBLANES


def _params(semantics):
    return pltpu.CompilerParams(dimension_semantics=semantics, vmem_limit_bytes=VMEM_LIMIT)


def _const_spec(shape):
    nd = len(shape)
    return pl.BlockSpec(shape, lambda *_: (0,) * nd, pipeline_mode=pl.Buffered(1))


def _rms(x, g):
    ms = jnp.mean(x * x, axis=-1, keepdims=True)
    return x * lax.rsqrt(ms + EPS) * g


def _sigmoid(x):
    return 1.0 / (1.0 + jnp.exp(-x))


def _softplus(x):
    return jnp.maximum(x, 0.0) + jnp.log1p(jnp.exp(-jnp.abs(x)))


def _dot(a, b):
    return jnp.dot(a, b, preferred_element_type=F32)


def _dot_nt(a, b):
    return lax.dot_general(a, b, (((1,), (1,)), ((), ())), preferred_element_type=F32)


def _dot_tn(a, b):
    return lax.dot_general(a, b, (((0,), (0,)), ((), ())), preferred_element_type=F32)


def _split_terms(x, terms):
    out = []
    for _ in range(terms):
        piece = x.astype(BF16)
        out.append(piece)
        x = x - piece.astype(F32)
    return out


def _scan_masks(n, reverse):
    row = lax.broadcasted_iota(jnp.int32, (n, n), 0)
    col = lax.broadcasted_iota(jnp.int32, (n, n), 1)
    if reverse:
        return col >= row, col <= row
    return col <= row, col >= row


def _ffn_apply(x, g_ref, win_ref, wout_ref):
    d_ff = wout_ref.shape[0]
    h = _rms(x, g_ref[...]).astype(BF16)
    hw = _dot(h, win_ref[...])
    gate = hw[:, :d_ff]
    up = hw[:, d_ff:]
    a = (gate * _sigmoid(gate) * up).astype(BF16)
    return x + 0.5 * _dot(a, wout_ref[...])


def _ffn_kernel(x_ref, g_ref, win_ref, wout_ref, o_ref):
    o_ref[...] = _ffn_apply(x_ref[...], g_ref, win_ref, wout_ref)


def _layer_specs(ffn, layer):
    return [pl.BlockSpec((None,) + a.shape[1:], lambda *_, n=a.ndim: (layer,) + (0,) * (n - 1),
                         pipeline_mode=pl.Buffered(1)) for a in ffn]


def _ffn(x, ffn, layer):
    t, d = x.shape
    tm = TOKEN_TILE
    return pl.pallas_call(
        _ffn_kernel,
        out_shape=jax.ShapeDtypeStruct((t, d), F32),
        grid=(t // tm,),
        in_specs=[pl.BlockSpec((tm, d), lambda i: (i, 0))] + _layer_specs(ffn, layer),
        out_specs=pl.BlockSpec((tm, d), lambda i: (i, 0)),
        compiler_params=_params(("parallel",)),
        name="ffn",
    )(x, *ffn)


def _even_in_kernel(x_ref, g_ref, w_ref, dtb_ref, qan_ref, wq_ref, kvan_ref, wkvb_ref,
                    qgc_ref, qgs_ref, kgc_ref, kgs_ref, ones_ref, cos_ref, sin_ref,
                    z_ref, xbc_ref, dt_ref, dtt_ref, q_ref, k_ref, vt_ref,
                    *, d_inner, d_xbc):
    x = x_ref[...]
    h = _rms(x, g_ref[...]).astype(BF16)
    hw = _dot(h, w_ref[...])
    c0 = d_inner
    c1 = c0 + d_xbc
    c2 = c1 + MLA_Q_RANK
    c3 = c2 + MLA_KV_RANK
    z_ref[...] = hw[:, :c0].astype(BF16)
    xbc_ref[...] = hw[:, c0:c1]
    small = hw[:, c3:c3 + LANES]
    kpe_sw = hw[:, c3 + LANES:c3 + 2 * LANES]
    dt = _softplus(small + dtb_ref[...])
    dt_ref[...] = dt
    dtt_ref[...] = dt.T[:dtt_ref.shape[0], :]

    n_k = MLA_N_HEADS * LANES
    qa = _rms(hw[:, c1:c2], qan_ref[...]).astype(BF16)
    qq = _dot(qa, wq_ref[...])
    kva = _rms(hw[:, c2:c3], kvan_ref[...]).astype(BF16)
    kv = _dot(kva, wkvb_ref[...])
    v_t = kv[:, n_k:].T
    ones_rows = jnp.ones((MLA_VT_ROWS - MLA_V, v_t.shape[1]), F32)
    vt_ref[...] = jnp.concatenate(
        [blk for hd in range(MLA_N_HEADS) for blk in (v_t[hd * MLA_V:(hd + 1) * MLA_V, :], ones_rows)],
        axis=0).astype(BF16)

    lane = lax.broadcasted_iota(jnp.int32, (1, LANES), 1)
    pe_lanes = (lane >= MLA_NOPE) & (lane < MLA_NOPE + MLA_ROPE)
    kpe = jnp.where(pe_lanes, small, 0.0)
    cos_t, sin_t = cos_ref[...], sin_ref[...]
    q_cos, q_sin = cos_t * qgc_ref[...], sin_t * qgs_ref[...]
    k_cos, k_sin = cos_t * kgc_ref[...], sin_t * kgs_ref[...]
    k_rot = kpe_sw * k_sin
    inv_dim = 1.0 / (MLA_NOPE + MLA_ROPE)

    q_blk = [qq[:, hd * LANES:(hd + 1) * LANES] for hd in range(MLA_N_HEADS)]
    k_blk = [kv[:, hd * LANES:(hd + 1) * LANES] + kpe for hd in range(MLA_N_HEADS)]
    q_ss, k_ss = [], []
    for pr in range(MLA_N_HEADS // 2):
        for blk, out in ((q_blk, q_ss), (k_blk, k_ss)):
            sq = jnp.concatenate([blk[2 * pr] * blk[2 * pr], blk[2 * pr + 1] * blk[2 * pr + 1]], axis=1)
            ss = _dot(sq.astype(BF16), ones_ref[...])
            out.extend([ss[:, :LANES], ss[:, LANES:]])
    for hd in range(MLA_N_HEADS):
        sl = slice(hd * LANES, (hd + 1) * LANES)
        q_rot = qq[:, n_k + hd * LANES:n_k + (hd + 1) * LANES]
        rq = lax.rsqrt(q_ss[hd] * inv_dim + EPS)
        q_ref[:, sl] = ((q_blk[hd] * q_cos + q_rot * q_sin) * rq).astype(BF16)
        rk = lax.rsqrt(k_ss[hd] * inv_dim + EPS)
        k_ref[:, sl] = ((k_blk[hd] * k_cos + k_rot) * rk).astype(BF16)


def _even_in(x, g, p, cos_t, sin_t):
    b, s, d = x.shape
    tm = TOKEN_TILE
    d_inner, d_xbc = p["d_inner"], p["d_xbc"]
    n_q = MLA_N_HEADS * LANES
    n_v = MLA_N_HEADS * MLA_VT_ROWS
    n_dt = 2 * (d_inner // SSD_HEAD_DIM)
    tok = lambda c: pl.BlockSpec((None, tm, c), lambda bi, i: (bi, i, 0))
    tab = pl.BlockSpec((tm, LANES), lambda bi, i: (i, 0))
    consts = [g, p["w_in"], p["dt_bias"], p["q_a_norm"], p["wq"], p["kv_a_norm"], p["wkvb"],
              p["q_gain_cos"], p["q_gain_sin"], p["k_gain_cos"], p["k_gain_sin"], p["pair_ones"]]
    return pl.pallas_call(
        functools.partial(_even_in_kernel, d_inner=d_inner, d_xbc=d_xbc),
        out_shape=[
            jax.ShapeDtypeStruct((b, s, d_inner), BF16),
            jax.ShapeDtypeStruct((b, s, d_xbc), F32),
            jax.ShapeDtypeStruct((b, s, LANES), F32),
            jax.ShapeDtypeStruct((b, n_dt, s), F32),
            jax.ShapeDtypeStruct((b, s, n_q), BF16),
            jax.ShapeDtypeStruct((b, s, n_q), BF16),
            jax.ShapeDtypeStruct((b, n_v, s), BF16),
        ],
        grid=(b, s // tm),
        in_specs=[tok(d)] + [_const_spec(c.shape) for c in consts] + [tab, tab],
        out_specs=[
            tok(d_inner), tok(d_xbc), tok(LANES),
            pl.BlockSpec((None, n_dt, tm), lambda bi, i: (bi, 0, i)),
            tok(n_q), tok(n_q),
            pl.BlockSpec((None, n_v, tm), lambda bi, i: (bi, 0, i)),
        ],
        compiler_params=_params(("parallel", "parallel")),
        name="even_in",
    )(x, *consts, cos_t, sin_t)


def _conv_kernel(main_ref, prev_ref, next_ref, w_ref, b_ref, o_ref, ext_ref, *, n_tiles):
    i = pl.program_id(1)
    tc = main_ref.shape[0]
    zero = jnp.zeros(prev_ref.shape, F32)
    ext_ref[0:HALO, :] = jnp.where(i > 0, prev_ref[...], zero)
    ext_ref[HALO:HALO + tc, :] = main_ref[...]
    ext_ref[HALO + tc:, :] = jnp.where(i < n_tiles - 1, next_ref[...], zero)
    pad = SSD_CONV // 2
    ext = ext_ref[...]
    n = tc + 2 * HALO
    acc = b_ref[...] + w_ref[pad:pad + 1, :] * ext[HALO:HALO + tc, :]
    for k in range(SSD_CONV):
        if k != pad:
            acc = acc + w_ref[k:k + 1, :] * pltpu.roll(ext, (pad - k) % n, axis=0)[HALO:HALO + tc, :]
    o_ref[...] = acc * _sigmoid(acc)


def _conv(xbc, conv_w, conv_b):
    b, s, c = xbc.shape
    tc = TOKEN_TILE
    n_tiles = s // tc
    per = tc // HALO
    n_halo = s // HALO
    return pl.pallas_call(
        functools.partial(_conv_kernel, n_tiles=n_tiles),
        out_shape=jax.ShapeDtypeStruct((b, s, c), F32),
        grid=(b, n_tiles),
        in_specs=[
            pl.BlockSpec((None, tc, c), lambda bi, i: (bi, i, 0)),
            pl.BlockSpec((None, HALO, c), lambda bi, i: (bi, jnp.maximum(i * per - 1, 0), 0)),
            pl.BlockSpec((None, HALO, c),
                         lambda bi, i: (bi, jnp.minimum((i + 1) * per, n_halo - 1), 0)),
            _const_spec(conv_w.shape), _const_spec(conv_b.shape),
        ],
        out_specs=pl.BlockSpec((None, tc, c), lambda bi, i: (bi, i, 0)),
        scratch_shapes=[pltpu.VMEM((tc + 2 * HALO, c), F32)],
        compiler_params=_params(("parallel", "parallel")),
        name="ssd_conv",
    )(xbc, xbc, xbc, conv_w, conv_b)


def _ssd_stats(dt, dtt, a_row, a_col, expand, *, reverse):
    L = dt.shape[0]
    keep, keep_t = _scan_masks(L, reverse)
    last = 0 if reverse else L - 1
    keep_bf = keep.astype(F32).astype(BF16)
    keep_t_bf = keep_t.astype(F32).astype(BF16)
    cum = sum(_dot(keep_bf, t) for t in _split_terms(dt * a_row, 3))
    cumt = sum(_dot(t, keep_t_bf) for t in _split_terms(dtt * a_col, 3))
    cum_last = cum[last:last + 1, :]
    narrow = jnp.concatenate(
        [jnp.exp(cum), dt * jnp.exp(cum_last - cum),
         jnp.broadcast_to(jnp.exp(cum_last), (SUBLANES, LANES))], axis=0)
    wide = _dot(narrow.astype(BF16), expand)
    return dict(cum=cum, cumt=cumt, keep=keep, e_off=wide[:L], w_state=wide[L:2 * L],
                chunk_decay=wide[2 * L:2 * L + 1])


def _ssd_main(xbc, dtt, stats, s_ref, d_skip, *, lane_off, d_inner):
    n_heads = d_inner // SSD_HEAD_DIM
    hpg = n_heads // SSD_N_GROUPS
    gw = hpg * SSD_HEAD_DIM
    gn = SSD_D_STATE
    xs = xbc[:, :d_inner]
    cum, cumt, keep = stats["cum"], stats["cumt"], stats["keep"]
    e_off, w_state, chunk_decay = stats["e_off"], stats["w_state"], stats["chunk_decay"]
    lane = lax.broadcasted_iota(jnp.int32, (1, LANES), 1)
    lo_lanes = lane < SSD_HEAD_DIM
    parts = []
    for g in range(SSD_N_GROUPS):
        gsl = slice(g * gw, (g + 1) * gw)
        b_g = xbc[:, d_inner + g * gn:d_inner + (g + 1) * gn]
        c_off = d_inner + SSD_N_GROUPS * gn
        c_bf = xbc[:, c_off + g * gn:c_off + (g + 1) * gn].astype(BF16)
        cb = _dot_nt(c_bf, b_g.astype(BF16))
        state = s_ref[g]
        y_off = _dot(c_bf, state.astype(BF16)) * e_off[:, gsl]
        xw = (xs[:, gsl] * w_state[:, gsl]).astype(BF16)
        s_ref[g] = state * chunk_decay[:, gsl] + _dot(b_g.T.astype(BF16), xw)
        for pr in range(hpg // 2):
            m_pair = []
            for j in range(2):
                hd = g * hpg + 2 * pr + j
                seg = cum[:, lane_off + hd:lane_off + hd + 1] - cumt[hd:hd + 1, :]
                m_h = jnp.where(keep, cb * jnp.exp(seg) * dtt[hd:hd + 1, :], 0.0)
                m_pair.append(m_h.astype(BF16))
            psl = slice(g * gw + pr * LANES, g * gw + (pr + 1) * LANES)
            x_pair = xs[:, psl]
            rhs = jnp.concatenate([jnp.where(lo_lanes, x_pair, 0.0),
                                   jnp.where(lo_lanes, 0.0, x_pair)], axis=0).astype(BF16)
            y_pair = _dot(jnp.concatenate(m_pair, axis=1), rhs) + y_off[:, pr * LANES:(pr + 1) * LANES]
            if d_skip is not None:
                y_pair = y_pair + x_pair * d_skip[:, psl]
            parts.append(y_pair)
    return jnp.concatenate(parts, axis=1).astype(BF16)


def _ssd_kernel(xf_ref, xb_ref, dtf_ref, dtb_ref, dttf_ref, dttb_ref, alog_row_ref, alog_col_ref,
                dskip_ref, ef_ref, eb_ref, yf_ref, yb_ref, sf_ref, sb_ref, *, d_inner):
    @pl.when(pl.program_id(1) == 0)
    def _():
        sf_ref[...] = jnp.zeros(sf_ref.shape, F32)
        sb_ref[...] = jnp.zeros(sb_ref.shape, F32)

    n_heads = d_inner // SSD_HEAD_DIM
    lane = lax.broadcasted_iota(jnp.int32, (1, LANES), 1)
    a_all = -jnp.exp(alog_row_ref[...])
    a_col = -jnp.exp(alog_col_ref[...])
    a_f = jnp.where(lane < n_heads, a_all, 0.0)
    a_b = jnp.where((lane >= n_heads) & (lane < 2 * n_heads), a_all, 0.0)
    chunks = [(slice(c * SCAN_CHUNK, (c + 1) * SCAN_CHUNK),
               slice((SCAN_STEP_CHUNKS - 1 - c) * SCAN_CHUNK, (SCAN_STEP_CHUNKS - c) * SCAN_CHUNK))
              for c in range(SCAN_STEP_CHUNKS)]
    stats = [(_ssd_stats(dtf_ref[fs, :], dttf_ref[0:n_heads, fs], a_f, a_col[0:n_heads, :], ef_ref[...],
                         reverse=False),
              _ssd_stats(dtb_ref[bs, :], dttb_ref[n_heads:2 * n_heads, bs], a_b, a_col[n_heads:2 * n_heads, :],
                         eb_ref[...], reverse=True)) for fs, bs in chunks]
    for (fs, bs), (stats_f, stats_b) in zip(chunks, stats):
        yf_ref[fs, :] = _ssd_main(xf_ref[fs, :], dttf_ref[0:n_heads, fs], stats_f, sf_ref, dskip_ref[...],
                                  lane_off=0, d_inner=d_inner)
        yb_ref[bs, :] = _ssd_main(xb_ref[bs, :], dttb_ref[n_heads:2 * n_heads, bs], stats_b, sb_ref, None,
                                  lane_off=n_heads, d_inner=d_inner)


def _ssd(xbc, dt, dtt, alog_row, alog_col, d_skip, e_f, e_b, *, d_inner):
    b, s, c = xbc.shape
    L = SCAN_CHUNK * SCAN_STEP_CHUNKS
    nc = s // L
    n_dt = dtt.shape[1]
    hpg_w = d_inner // SSD_N_GROUPS
    fwd = lambda w: pl.BlockSpec((None, L, w), lambda bi, i: (bi, i, 0))
    bwd = lambda w: pl.BlockSpec((None, L, w), lambda bi, i: (bi, nc - 1 - i, 0))
    return pl.pallas_call(
        functools.partial(_ssd_kernel, d_inner=d_inner),
        out_shape=[jax.ShapeDtypeStruct((b, s, d_inner), BF16)] * 2,
        grid=(b, nc),
        in_specs=[
            fwd(c), bwd(c), fwd(LANES), bwd(LANES),
            pl.BlockSpec((None, n_dt, L), lambda bi, i: (bi, 0, i)),
            pl.BlockSpec((None, n_dt, L), lambda bi, i: (bi, 0, nc - 1 - i)),
            _const_spec(alog_row.shape), _const_spec(alog_col.shape), _const_spec(d_skip.shape),
            _const_spec(e_f.shape), _const_spec(e_b.shape),
        ],
        out_specs=[fwd(d_inner), bwd(d_inner)],
        scratch_shapes=[pltpu.VMEM((SSD_N_GROUPS, SSD_D_STATE, hpg_w), F32)] * 2,
        compiler_params=_params(("parallel", "arbitrary")),
        name="ssd_scan",
    )(xbc, xbc, dt, dt, dtt, dtt, alog_row, alog_col, d_skip, e_f, e_b)


def _attn_kernel(q_ref, k_ref, vt_ref, o_ref):
    def scores(hd):
        sl = slice(hd * LANES, (hd + 1) * LANES)
        return _dot_nt(k_ref[:, sl], q_ref[:, sl])

    outs = []
    st_next = scores(0)
    for hd in range(MLA_N_HEADS):
        st = st_next
        if hd + 1 < MLA_N_HEADS:
            st_next = scores(hd + 1)
        m = jnp.max(st, axis=0, keepdims=True)
        p = jnp.exp2(st - m).astype(BF16)
        acc = _dot(vt_ref[hd * MLA_VT_ROWS:(hd + 1) * MLA_VT_ROWS, :], p)
        outs.append(acc[:MLA_V] / acc[MLA_V:MLA_V + 1])
    o_ref[...] = jnp.concatenate(outs, axis=0).T.astype(BF16)


def _attention(q, k, vt):
    b, s, n_q = q.shape
    n_vt = vt.shape[1]
    n_v = MLA_N_HEADS * MLA_V
    tq = ATTN_Q_TILE
    return pl.pallas_call(
        _attn_kernel,
        out_shape=jax.ShapeDtypeStruct((b, s, n_v), BF16),
        grid=(b, s // tq),
        in_specs=[
            pl.BlockSpec((None, tq, n_q), lambda bi, i: (bi, i, 0)),
            pl.BlockSpec((None, s, n_q), lambda bi, i: (bi, 0, 0), pipeline_mode=pl.Buffered(1)),
            pl.BlockSpec((None, n_vt, s), lambda bi, i: (bi, 0, 0), pipeline_mode=pl.Buffered(1)),
        ],
        out_specs=pl.BlockSpec((None, tq, n_v), lambda bi, i: (bi, i, 0)),
        compiler_params=_params(("parallel", "arbitrary")),
        name="mla_attention",
    )(q, k, vt)


def _even_out_kernel(x_ref, yf_ref, yb_ref, z_ref, o_ref, g_ref, w_ref, fg_ref, fwin_ref, fwout_ref,
                     out_ref, *, d_inner):
    z = z_ref[...].astype(F32)
    y = (yf_ref[...].astype(F32) + yb_ref[...].astype(F32)) * (z * _sigmoid(z))
    gw = d_inner // SSD_N_GROUPS
    normed = []
    for g in range(SSD_N_GROUPS):
        seg = y[:, g * gw:(g + 1) * gw]
        normed.append(seg * lax.rsqrt(jnp.mean(seg * seg, axis=-1, keepdims=True) + EPS))
    yn = (jnp.concatenate(normed, axis=1) * g_ref[...]).astype(BF16)
    x = x_ref[...] + _dot(yn, w_ref[0:d_inner, :]) + _dot(o_ref[...], w_ref[d_inner:, :])
    out_ref[...] = _ffn_apply(x, fg_ref, fwin_ref, fwout_ref)


def _even_out(x, yf, yb, z, o, g, w, ffn, layer):
    t, d = x.shape
    d_inner = yf.shape[1]
    tm = TOKEN_TILE
    tok = lambda c: pl.BlockSpec((tm, c), lambda i: (i, 0))
    return pl.pallas_call(
        functools.partial(_even_out_kernel, d_inner=d_inner),
        out_shape=jax.ShapeDtypeStruct((t, d), F32),
        grid=(t // tm,),
        in_specs=[tok(d), tok(d_inner), tok(d_inner), tok(d_inner), tok(o.shape[1]),
                  _const_spec(g.shape), _const_spec(w.shape)] + _layer_specs(ffn, layer),
        out_specs=tok(d),
        compiler_params=_params(("parallel",)),
        name="even_out_ffn",
    )(x, yf, yb, z, o, g, w, *ffn)


def _odd_in_kernel(x_ref, g_ref, w_ref, gb_ref, qt_ref, k_ref, kt_ref, v_ref, o_ref, gates_ref, gatest_ref):
    h = _rms(x_ref[...], g_ref[...]).astype(BF16)
    hw = _dot(h, w_ref[...])
    n_qk = ML_N_HEADS * ML_QK
    n_v = ML_N_HEADS * ML_V
    qt_ref[...] = hw[:, :n_qk].T.astype(BF16)
    k = hw[:, n_qk:2 * n_qk]
    k_ref[...] = k.astype(BF16)
    kt_ref[...] = k.T.astype(BF16)
    v_ref[...] = hw[:, 2 * n_qk:2 * n_qk + n_v].astype(BF16)
    o_ref[...] = hw[:, 2 * n_qk + n_v:2 * n_qk + 2 * n_v].astype(BF16)
    pre = hw[:, 2 * n_qk + 2 * n_v:] + gb_ref[...]
    lane = lax.broadcasted_iota(jnp.int32, (1, LANES), 1)
    gates = jnp.where(lane < 2 * ML_N_HEADS, pre, -_softplus(-pre))
    gates_ref[...] = gates
    gatest_ref[...] = gates.T[:gatest_ref.shape[0], :]


def _odd_in(x, g, w, gate_bias):
    b, s, d = x.shape
    tm = TOKEN_TILE
    n_qk = ML_N_HEADS * ML_QK
    n_v = ML_N_HEADS * ML_V
    n_g = 4 * ML_N_HEADS
    tok = lambda c: pl.BlockSpec((None, tm, c), lambda bi, i: (bi, i, 0))
    tok_t = lambda c: pl.BlockSpec((None, c, tm), lambda bi, i: (bi, 0, i))
    return pl.pallas_call(
        _odd_in_kernel,
        out_shape=[
            jax.ShapeDtypeStruct((b, n_qk, s), BF16),
            jax.ShapeDtypeStruct((b, s, n_qk), BF16),
            jax.ShapeDtypeStruct((b, n_qk, s), BF16),
            jax.ShapeDtypeStruct((b, s, n_v), BF16),
            jax.ShapeDtypeStruct((b, s, n_v), BF16),
            jax.ShapeDtypeStruct((b, s, LANES), F32),
            jax.ShapeDtypeStruct((b, n_g, s), F32),
        ],
        grid=(b, s // tm),
        in_specs=[tok(d), _const_spec(g.shape), _const_spec(w.shape), _const_spec(gate_bias.shape)],
        out_specs=[tok_t(n_qk), tok(n_qk), tok_t(n_qk), tok(n_v), tok(n_v), tok(LANES), tok_t(n_g)],
        compiler_params=_params(("parallel", "parallel")),
        name="odd_in",
    )(x, g, w, gate_bias)


def _cummax_lanes(u, reverse):
    n = u.shape[1]
    lane = lax.broadcasted_iota(jnp.int32, (1, n), 1)
    d = 1
    while d < n:
        if reverse:
            shifted, valid = pltpu.roll(u, n - d, axis=1), lane < n - d
        else:
            shifted, valid = pltpu.roll(u, d, axis=1), lane >= d
        u = jnp.where(valid, jnp.maximum(u, shifted), u)
        d *= 2
    return u


def _mlstm_stats(gates, gatest, sel_u, m_ref, *, i_off, f_off, reverse):
    L = gates.shape[0]
    H = ML_N_HEADS
    assert L == LANES == ML_V and 2 * ML_QK == LANES
    keep, keep_t = _scan_masks(L, reverse)
    keep_bf = keep.astype(F32).astype(BF16)
    keep_t_bf = keep_t.astype(F32).astype(BF16)
    last = 0 if reverse else L - 1

    li_r = gatest[i_off:i_off + H, :]
    bcum_r = sum(_dot(t, keep_t_bf) for t in _split_terms(gatest[f_off:f_off + H, :], 3))
    u_r = li_r - bcum_r
    m_in = m_ref[...]
    mx = jnp.maximum(m_in, _cummax_lanes(u_r, reverse))
    w_inter = jnp.exp(m_in - mx)
    e_negm = jnp.exp(-(bcum_r + mx))
    b_end = jnp.broadcast_to(bcum_r[:, last:last + 1], (H, L))
    g_end = b_end + u_r
    m_new = jnp.maximum(b_end + m_in, jnp.max(g_end, axis=1, keepdims=True))
    w_src = jnp.exp(g_end - m_new)
    decay = jnp.exp(b_end + m_in - m_new)
    m_ref[...] = m_new

    bcum_c = sum(_dot(keep_bf, t) for t in _split_terms(gates, 3))
    u_all = sum(_dot(t, sel_u) for t in _split_terms(jnp.concatenate([gates, bcum_c], axis=1), 2))
    return dict(mx=mx, w_inter=w_inter, e_negm=e_negm, w_src=w_src, decay=decay, u_all=u_all, keep_t=keep_t)


def _mlstm_main(k, kt, qt, v, stats, c_ref):
    L = k.shape[0]
    H = ML_N_HEADS
    mx, w_inter, e_negm, w_src = stats["mx"], stats["w_inter"], stats["e_negm"], stats["w_src"]
    decay, u_all, keep_t = stats["decay"], stats["u_all"], stats["keep_t"]
    srow = lax.broadcasted_iota(jnp.int32, (LANES, 1), 0)
    first = srow < ML_QK
    ones = jnp.ones((L, ML_V), BF16)
    c_in = [c_ref[pr] for pr in range(H // 2)]
    qt_own, kt_own, st = [], [], []
    for pr in range(H // 2):
        psl = slice(pr * LANES, (pr + 1) * LANES)
        zero = jnp.zeros((LANES, L), BF16)
        qt_own += [jnp.where(first, qt[psl, :], zero), jnp.where(first, zero, qt[psl, :])]
        kt_own += [jnp.where(first, kt[psl, :], zero), jnp.where(first, zero, kt[psl, :])]
        st.append(_dot(k[:, psl], jnp.concatenate(qt_own[-2:], axis=1)))
    lhs_t = []
    for hd in range(H):
        row = lambda a: a[hd:hd + 1, :]
        c_aug = c_in[hd // 2]
        w = jnp.where(keep_t, jnp.exp(u_all[:, hd * LANES:(hd + 1) * LANES] - row(mx)), 0.0)
        sw = st[hd // 2][:, (hd % 2) * L:(hd % 2 + 1) * L] * w
        q_f32 = qt_own[hd].astype(F32)
        nq = jnp.sum(c_aug[:, ML_V:] * q_f32, axis=0, keepdims=True)
        den = jnp.sum(sw, axis=0, keepdims=True) + row(w_inter) * nq
        inv = 1.0 / jnp.maximum(jnp.abs(den), row(e_negm))
        lhs_t.append(jnp.concatenate(
            [(sw * inv).astype(BF16), (q_f32 * (row(w_inter) * inv)).astype(BF16)], axis=0))
    outs = []
    for hd in range(H):
        v_h = v[:, hd * ML_V:(hd + 1) * ML_V]
        rhs = jnp.concatenate([v_h, c_in[hd // 2][:, :ML_V].astype(BF16)], axis=0)
        outs.append(_dot_tn(lhs_t[hd], rhs))
    for pr in range(H // 2):
        update = jnp.zeros(c_in[pr].shape, F32)
        for hd in (2 * pr, 2 * pr + 1):
            wkt = (kt_own[hd].astype(F32) * w_src[hd:hd + 1, :]).astype(BF16)
            v_aug = jnp.concatenate([v[:, hd * ML_V:(hd + 1) * ML_V], ones], axis=1)
            update = update + _dot(wkt, v_aug)
        dec = jnp.where(first, decay[2 * pr:2 * pr + 1, :], decay[2 * pr + 1:2 * pr + 2, :])
        c_ref[pr] = c_in[pr] * jnp.concatenate([dec, dec], axis=1) + update
    return jnp.concatenate(outs, axis=1).astype(BF16)


def _mlstm_kernel(kf_ref, ktf_ref, qtf_ref, vf_ref, gf_ref, gtf_ref,
                  kb_ref, ktb_ref, qtb_ref, vb_ref, gb_ref, gtb_ref, self_ref, selb_ref,
                  hf_ref, hb_ref, cf_ref, cb_ref, mf_ref, mb_ref):
    @pl.when(pl.program_id(1) == 0)
    def _():
        for ref in (cf_ref, cb_ref, mf_ref, mb_ref):
            ref[...] = jnp.zeros(ref.shape, F32)

    h = ML_N_HEADS
    chunks = [(slice(c * SCAN_CHUNK, (c + 1) * SCAN_CHUNK),
               slice((SCAN_STEP_CHUNKS - 1 - c) * SCAN_CHUNK, (SCAN_STEP_CHUNKS - c) * SCAN_CHUNK))
              for c in range(SCAN_STEP_CHUNKS)]
    stats = [(_mlstm_stats(gf_ref[fs, :], gtf_ref[:, fs], self_ref[...], mf_ref, i_off=0, f_off=2 * h,
                           reverse=False),
              _mlstm_stats(gb_ref[bs, :], gtb_ref[:, bs], selb_ref[...], mb_ref, i_off=h, f_off=3 * h,
                           reverse=True)) for fs, bs in chunks]
    for (fs, bs), (stats_f, stats_b) in zip(chunks, stats):
        hf_ref[fs, :] = _mlstm_main(kf_ref[fs, :], ktf_ref[:, fs], qtf_ref[:, fs], vf_ref[fs, :], stats_f, cf_ref)
        hb_ref[bs, :] = _mlstm_main(kb_ref[bs, :], ktb_ref[:, bs], qtb_ref[:, bs], vb_ref[bs, :], stats_b, cb_ref)


def _mlstm(qt, k, kt, v, gates, gatest, sel_f, sel_b):
    b, s, n_qk = k.shape
    n_v = v.shape[2]
    n_g = gatest.shape[1]
    L = SCAN_CHUNK * SCAN_STEP_CHUNKS
    nc = s // L
    fwd = lambda w: pl.BlockSpec((None, L, w), lambda bi, i: (bi, i, 0))
    bwd = lambda w: pl.BlockSpec((None, L, w), lambda bi, i: (bi, nc - 1 - i, 0))
    fwd_t = lambda c: pl.BlockSpec((None, c, L), lambda bi, i: (bi, 0, i))
    bwd_t = lambda c: pl.BlockSpec((None, c, L), lambda bi, i: (bi, 0, nc - 1 - i))
    state = pltpu.VMEM((ML_N_HEADS // 2, 2 * ML_QK, 2 * ML_V), F32)
    stab = pltpu.VMEM((ML_N_HEADS, LANES), F32)
    return pl.pallas_call(
        _mlstm_kernel,
        out_shape=[jax.ShapeDtypeStruct((b, s, n_v), BF16)] * 2,
        grid=(b, nc),
        in_specs=[fwd(n_qk), fwd_t(n_qk), fwd_t(n_qk), fwd(n_v), fwd(LANES), fwd_t(n_g),
                  bwd(n_qk), bwd_t(n_qk), bwd_t(n_qk), bwd(n_v), bwd(LANES), bwd_t(n_g),
                  _const_spec(sel_f.shape), _const_spec(sel_b.shape)],
        out_specs=[fwd(n_v), bwd(n_v)],
        scratch_shapes=[state, state, stab, stab],
        compiler_params=_params(("parallel", "arbitrary")),
        name="mlstm_scan",
    )(k, kt, qt, v, gates, gatest, k, kt, qt, v, gates, gatest, sel_f, sel_b)


def _odd_out_kernel(x_ref, hf_ref, hb_ref, o_ref, g_ref, w_ref, fg_ref, fwin_ref, fwout_ref, out_ref):
    hs = hf_ref[...].astype(F32) + hb_ref[...].astype(F32)
    normed = []
    for hd in range(ML_N_HEADS):
        seg = hs[:, hd * ML_V:(hd + 1) * ML_V]
        normed.append(seg * lax.rsqrt(jnp.mean(seg * seg, axis=-1, keepdims=True) + EPS))
    gated = (_sigmoid(o_ref[...].astype(F32)) * (jnp.concatenate(normed, axis=1) * g_ref[...])).astype(BF16)
    x = x_ref[...] + _dot(gated, w_ref[...])
    out_ref[...] = _ffn_apply(x, fg_ref, fwin_ref, fwout_ref)


def _odd_out(x, hf, hb, o, g, w, ffn, layer):
    t, d = x.shape
    n_v = hf.shape[1]
    tm = TOKEN_TILE
    tok = lambda c: pl.BlockSpec((tm, c), lambda i: (i, 0))
    return pl.pallas_call(
        _odd_out_kernel,
        out_shape=jax.ShapeDtypeStruct((t, d), F32),
        grid=(t // tm,),
        in_specs=[tok(d), tok(n_v), tok(n_v), tok(n_v), _const_spec(g.shape), _const_spec(w.shape)]
        + _layer_specs(ffn, layer),
        out_specs=tok(d),
        compiler_params=_params(("parallel",)),
        name="odd_out_ffn",
    )(x, hf, hb, o, g, w, *ffn)


def _row(v):
    return v.reshape(1, -1).astype(F32)


def _pad_lanes(v, width=LANES):
    return jnp.pad(v, [(0, 0)] * (v.ndim - 1) + [(0, width - v.shape[-1])])


def _swap_halves(v):
    half = v.shape[-1] // 2
    return jnp.concatenate([v[..., half:], v[..., :half]], axis=-1)


def _on_rope_lanes(v):
    return jnp.pad(v, [(0, 0)] * (v.ndim - 1) + [(MLA_NOPE, LANES - MLA_NOPE - v.shape[-1])])


def _prep_even(ev_w_in, ev_w_out, conv_w, conv_b, dt_bias, a_log, d_skip, ssd_norm,
               q_a_norm, w_q_b, kv_a_norm, w_kv_b, q_norm, k_norm):
    d_inner = ssd_norm.shape[0]
    d_xbc = conv_b.shape[0]
    n_heads = a_log.shape[1]
    c0, c1 = d_inner, d_inner + d_xbc
    c2 = c1 + 2 * n_heads
    c3 = c2 + MLA_Q_RANK
    c4 = c3 + MLA_KV_RANK
    w_kpe = ev_w_in[:, c4:]
    small = jnp.concatenate([_pad_lanes(ev_w_in[:, c1:c2], MLA_NOPE), _pad_lanes(w_kpe, LANES - MLA_NOPE)], axis=1)
    w_in = jnp.concatenate([ev_w_in[:, :c1], ev_w_in[:, c2:c4], small,
                            _on_rope_lanes(_swap_halves(w_kpe))], axis=1).astype(BF16)
    d_qk = MLA_NOPE + MLA_ROPE
    wq3 = w_q_b.reshape(MLA_Q_RANK, MLA_N_HEADS, d_qk)
    wq = jnp.concatenate(
        [_pad_lanes(wq3).reshape(MLA_Q_RANK, -1),
         _on_rope_lanes(_swap_halves(wq3[:, :, MLA_NOPE:])).reshape(MLA_Q_RANK, -1)], axis=1).astype(BF16)
    q_scale = d_qk ** -0.5 * LOG2_E
    ones_blk = jnp.ones((LANES, LANES), BF16)
    zero_blk = jnp.zeros((LANES, LANES), BF16)
    pair_ones = jnp.concatenate([jnp.concatenate([ones_blk, zero_blk], axis=1),
                                 jnp.concatenate([zero_blk, ones_blk], axis=1)], axis=0)
    wkv = w_kv_b.reshape(MLA_KV_RANK, MLA_N_HEADS, MLA_NOPE + MLA_V)
    wkvb = jnp.concatenate(
        [_pad_lanes(wkv[:, :, :MLA_NOPE]).reshape(MLA_KV_RANK, -1),
         wkv[:, :, MLA_NOPE:].reshape(MLA_KV_RANK, -1)], axis=1).astype(BF16)
    head_of_lane = jnp.arange(d_inner) // SSD_HEAD_DIM
    e_f = (jnp.arange(LANES)[:, None] == head_of_lane[None, :]).astype(BF16)
    e_b = (jnp.arange(LANES)[:, None] == head_of_lane[None, :] + n_heads).astype(BF16)
    return dict(
        d_inner=d_inner, d_xbc=d_xbc, w_in=w_in, w_out=ev_w_out.astype(BF16),
        conv_w=_pad_lanes(conv_w.T, SUBLANES).T.astype(F32), conv_b=_row(conv_b),
        dt_bias=_pad_lanes(_row(dt_bias)), alog_row=_pad_lanes(_row(a_log)),
        alog_col=a_log.reshape(-1, 1).astype(F32), d_skip=_row(jnp.repeat(d_skip, SSD_HEAD_DIM)),
        ssd_norm=_row(ssd_norm), e_f=e_f, e_b=e_b,
        q_a_norm=_row(q_a_norm), wq=wq, kv_a_norm=_row(kv_a_norm), wkvb=wkvb, pair_ones=pair_ones,
        q_gain_cos=_pad_lanes(_row(q_norm)) * q_scale,
        q_gain_sin=_on_rope_lanes(_swap_halves(_row(q_norm)[:, MLA_NOPE:])) * q_scale,
        k_gain_cos=_pad_lanes(_row(k_norm)),
        k_gain_sin=_on_rope_lanes(_swap_halves(_row(k_norm)[:, MLA_NOPE:])))


def _rope_tables(s):
    pos = jnp.arange(s, dtype=F32)
    inv_freq = jnp.power(ROPE_BASE, -jnp.arange(0, MLA_ROPE, 2, dtype=F32) / MLA_ROPE)
    freqs = pos[:, None] * inv_freq[None, :]
    cos, sin = jnp.cos(freqs), jnp.sin(freqs)
    tail = jnp.ones((s, LANES - MLA_NOPE - MLA_ROPE), F32)
    cos_t = jnp.concatenate([jnp.ones((s, MLA_NOPE), F32), cos, cos, tail], axis=1)
    sin_t = _on_rope_lanes(jnp.concatenate([-sin, sin], axis=1))
    return cos_t, sin_t


def _even_mixer(x, norm_g, p, ffn2, layer):
    b, s, d = x.shape
    z, xbc, dt, dtt, q, k, vt = _even_in(x, norm_g, p, *_rope_tables(s))
    xact = _conv(xbc, p["conv_w"], p["conv_b"])
    yf, yb = _ssd(xact, dt, dtt, p["alog_row"], p["alog_col"], p["d_skip"], p["e_f"], p["e_b"],
                  d_inner=p["d_inner"])
    o = _attention(q, k, vt)
    flat = lambda a: a.reshape(b * s, a.shape[-1])
    return _even_out(flat(x), flat(yf), flat(yb), flat(z), flat(o), p["ssd_norm"],
                     p["w_out"], ffn2, layer).reshape(b, s, d)


def _prep_odd(od_w_in, od_w_out, ig_bias, fg_bias, ml_norm):
    n_qk = ML_N_HEADS * ML_QK
    n_main = 2 * n_qk + 2 * ML_N_HEADS * ML_V
    w_in = jnp.concatenate([od_w_in[:, :n_qk] * ML_QK ** -0.5, od_w_in[:, n_qk:n_main],
                            _pad_lanes(od_w_in[:, n_main:])], axis=1).astype(BF16)
    gate_bias = _pad_lanes(_row(jnp.concatenate([ig_bias.reshape(-1), fg_bias.reshape(-1)])))
    h = ML_N_HEADS
    head_of_col = jnp.arange(h * LANES) // LANES
    rows = jnp.arange(2 * LANES)[:, None]

    def sel(i_off, f_off):
        return ((rows == head_of_col[None, :] + i_off).astype(F32)
                - (rows == head_of_col[None, :] + LANES + f_off).astype(F32)).astype(BF16)

    return dict(w_in=w_in, w_out=od_w_out.astype(BF16), gate_bias=gate_bias, ml_norm=_row(ml_norm),
                sel_f=sel(0, 2 * h), sel_b=sel(h, 3 * h))


def _odd_mixer(x, norm_g, p, ffn2, layer):
    b, s, d = x.shape
    qt, k, kt, v, o, gates, gatest = _odd_in(x, norm_g, p["w_in"], p["gate_bias"])
    hf, hb = _mlstm(qt, k, kt, v, gates, gatest, p["sel_f"], p["sel_b"])
    flat = lambda a: a.reshape(b * s, a.shape[-1])
    return _odd_out(flat(x), flat(hf), flat(hb), flat(o), p["ml_norm"], p["w_out"], ffn2, layer).reshape(b, s, d)


def kernel(x_prompt, x_sample, ffn1_norm, ffn1_w_in, ffn1_w_out, mix_norm, ffn2_norm, ffn2_w_in, ffn2_w_out,
           ev_w_in, ev_w_out, ssd_conv_w, ssd_conv_b, ssd_dt_bias, ssd_a_log, ssd_d_skip, ssd_norm,
           mla_q_a_norm, mla_w_q_b, mla_kv_a_norm, mla_w_kv_b, mla_q_norm, mla_k_norm,
           od_w_in, od_w_out, ml_ig_bias, ml_fg_bias, ml_norm):
    depth = ffn1_norm.shape[0]
    layers = []
    for layer in range(depth):
        j = layer // 2
        if layer % 2 == 0:
            mixer = functools.partial(_even_mixer, p=_prep_even(
                ev_w_in[j], ev_w_out[j], ssd_conv_w[j], ssd_conv_b[j], ssd_dt_bias[j], ssd_a_log[j],
                ssd_d_skip[j], ssd_norm[j], mla_q_a_norm[j], mla_w_q_b[j], mla_kv_a_norm[j],
                mla_w_kv_b[j], mla_q_norm[j], mla_k_norm[j]))
        else:
            mixer = functools.partial(_odd_mixer, p=_prep_odd(
                od_w_in[j], od_w_out[j], ml_ig_bias[j], ml_fg_bias[j], ml_norm[j]))
        layers.append(dict(mixer=mixer, mix_norm=_row(mix_norm[layer])))
    ffn1 = (ffn1_norm[:, None, :].astype(F32), ffn1_w_in.astype(BF16), ffn1_w_out.astype(BF16))
    ffn2 = (ffn2_norm[:, None, :].astype(F32), ffn2_w_in.astype(BF16), ffn2_w_out.astype(BF16))

    def trunk(x):
        b, s, d = x.shape
        for layer, lp in enumerate(layers):
            x = _ffn(x.reshape(b * s, d), ffn1, layer).reshape(b, s, d)
            x = lp["mixer"](x, lp["mix_norm"], ffn2=ffn2, layer=layer)
        return x

    return trunk(x_prompt), trunk(x_sample)
```

```python
import functools

import jax
import jax.numpy as jnp
from jax import lax
from jax.experimental import pallas as pl
from jax.experimental.pallas import tpu as pltpu

F32 = jnp.float32
BF16 = jnp.bfloat16
EPS = 1e-6

LANES = 128
SUBLANES = 8
VMEM_LIMIT = 56 * 1024 * 1024

SSD_HEAD_DIM = 64
SSD_N_GROUPS = 2
SSD_D_STATE = 128
SSD_CONV = 5
MLA_N_HEADS = 8
MLA_Q_RANK = 384
MLA_KV_RANK = 256
MLA_NOPE = 64
MLA_ROPE = 32
MLA_V = 64
MLA_VT_ROWS = MLA_V + 16
ROPE_BASE = 10000.0
LOG2_E = 1.4426950408889634
ML_N_HEADS = 8
ML_QK = 64
ML_V = 128

TOKEN_TILE = 512
FFN_TOKEN_TILE = 1024
SCAN_CHUNK = 128
SCAN_STEP_CHUNKS = 2
ATTN_Q_TILE = 512
HALO = SUBLANES


def _params(semantics):
    return pltpu.CompilerParams(dimension_semantics=semantics, vmem_limit_bytes=VMEM_LIMIT)


def _const_spec(shape):
    nd = len(shape)
    return pl.BlockSpec(shape, lambda *_: (0,) * nd, pipeline_mode=pl.Buffered(1))


def _rms(x, g):
    ms = jnp.mean(x * x, axis=-1, keepdims=True)
    return x * lax.rsqrt(ms + EPS) * g


def _sigmoid(x):
    return 1.0 / (1.0 + jnp.exp(-x))


def _softplus(x):
    return jnp.maximum(x, 0.0) + jnp.log1p(jnp.exp(-jnp.abs(x)))


def _dot(a, b):
    return jnp.dot(a, b, preferred_element_type=F32)


def _dot_nt(a, b):
    return lax.dot_general(a, b, (((1,), (1,)), ((), ())), preferred_element_type=F32)


def _dot_tn(a, b):
    return lax.dot_general(a, b, (((0,), (0,)), ((), ())), preferred_element_type=F32)


def _split_terms(x, terms):
    out = []
    for _ in range(terms):
        piece = x.astype(BF16)
        out.append(piece)
        x = x - piece.astype(F32)
    return out


def _scan_masks(n, reverse):
    row = lax.broadcasted_iota(jnp.int32, (n, n), 0)
    col = lax.broadcasted_iota(jnp.int32, (n, n), 1)
    if reverse:
        return col >= row, col <= row
    return col <= row, col >= row


def _ffn_apply(x, g_ref, win_ref, wout_ref):
    d_ff = wout_ref.shape[0]
    h = _rms(x, g_ref[...]).astype(BF16)
    hw = _dot(h, win_ref[...])
    gate = hw[:, :d_ff]
    up = hw[:, d_ff:]
    a = (gate * _sigmoid(gate) * up).astype(BF16)
    return x + 0.5 * _dot(a, wout_ref[...])


def _ffn_kernel(x_ref, g_ref, win_ref, wout_ref, o_ref):
    o_ref[...] = _ffn_apply(x_ref[...], g_ref, win_ref, wout_ref)


def _layer_specs(ffn, layer):
    return [pl.BlockSpec((None,) + a.shape[1:], lambda *_, n=a.ndim: (layer,) + (0,) * (n - 1),
                         pipeline_mode=pl.Buffered(1)) for a in ffn]


def _ffn(x, ffn, layer):
    t, d = x.shape
    tm = FFN_TOKEN_TILE
    return pl.pallas_call(
        _ffn_kernel,
        out_shape=jax.ShapeDtypeStruct((t, d), F32),
        grid=(t // tm,),
        in_specs=[pl.BlockSpec((tm, d), lambda i: (i, 0))] + _layer_specs(ffn, layer),
        out_specs=pl.BlockSpec((tm, d), lambda i: (i, 0)),
        compiler_params=_params(("parallel",)),
        name="ffn",
    )(x, *ffn)


def _conv_silu(ext, w_ref, b_ref):
    n = ext.shape[0]
    tc = n - 2 * HALO
    pad = SSD_CONV // 2
    acc = b_ref[...] + w_ref[pad:pad + 1, :] * ext[HALO:HALO + tc, :]
    for k in range(SSD_CONV):
        if k != pad:
            acc = acc + w_ref[k:k + 1, :] * pltpu.roll(ext, (pad - k) % n, axis=0)[HALO:HALO + tc, :]
    return acc * _sigmoid(acc)


def _even_in_kernel(x_ref, xp_ref, xn_ref, g_ref, w_ref, dtb_ref, qan_ref, wq_ref, kvan_ref, wkvb_ref,
                    qgc_ref, qgs_ref, kgc_ref, kgs_ref, ones_ref, cw_ref, cb_ref, cos_ref, sin_ref,
                    z_ref, xbc_ref, dt_ref, dtt_ref, q_ref, k_ref, vt_ref,
                    *, d_inner, d_xbc, n_tiles):
    i = pl.program_id(1)
    tm = x_ref.shape[0]
    xe = jnp.concatenate([xp_ref[...], x_ref[...], xn_ref[...]], axis=0)
    hwe = _dot(_rms(xe, g_ref[...]).astype(BF16), w_ref[...])
    hw = hwe[HALO:HALO + tm, :]
    c0 = d_inner
    c1 = c0 + d_xbc
    c2 = c1 + MLA_Q_RANK
    c3 = c2 + MLA_KV_RANK
    z_ref[...] = hw[:, :c0].astype(BF16)
    zero = jnp.zeros((HALO, d_xbc), F32)
    ext = jnp.concatenate([jnp.where(i > 0, hwe[:HALO, c0:c1], zero), hw[:, c0:c1],
                           jnp.where(i < n_tiles - 1, hwe[HALO + tm:, c0:c1], zero)], axis=0)
    xbc_ref[...] = _conv_silu(ext, cw_ref, cb_ref)
    small = hw[:, c3:c3 + LANES]
    kpe_sw = hw[:, c3 + LANES:c3 + 2 * LANES]
    dt = _softplus(small + dtb_ref[...])
    dt_ref[...] = dt
    dtt_ref[...] = dt.T[:dtt_ref.shape[0], :]

    n_k = MLA_N_HEADS * LANES
    qa = _rms(hw[:, c1:c2], qan_ref[...]).astype(BF16)
    qq = _dot(qa, wq_ref[...])
    kva = _rms(hw[:, c2:c3], kvan_ref[...]).astype(BF16)
    kv = _dot(kva, wkvb_ref[...])
    v_t = kv[:, n_k:].T
    ones_rows = jnp.ones((MLA_VT_ROWS - MLA_V, v_t.shape[1]), F32)
    vt_ref[...] = jnp.concatenate(
        [blk for hd in range(MLA_N_HEADS) for blk in (v_t[hd * MLA_V:(hd + 1) * MLA_V, :], ones_rows)],
        axis=0).astype(BF16)

    lane = lax.broadcasted_iota(jnp.int32, (1, LANES), 1)
    pe_lanes = (lane >= MLA_NOPE) & (lane < MLA_NOPE + MLA_ROPE)
    kpe = jnp.where(pe_lanes, small, 0.0)
    cos_t, sin_t = cos_ref[...], sin_ref[...]
    q_cos, q_sin = cos_t * qgc_ref[...], sin_t * qgs_ref[...]
    k_cos, k_sin = cos_t * kgc_ref[...], sin_t * kgs_ref[...]
    k_rot = kpe_sw * k_sin
    inv_dim = 1.0 / (MLA_NOPE + MLA_ROPE)

    q_blk = [qq[:, hd * LANES:(hd + 1) * LANES] for hd in range(MLA_N_HEADS)]
    k_blk = [kv[:, hd * LANES:(hd + 1) * LANES] + kpe for hd in range(MLA_N_HEADS)]
    q_ss, k_ss = [], []
    for pr in range(MLA_N_HEADS // 2):
        for blk, out in ((q_blk, q_ss), (k_blk, k_ss)):
            sq = jnp.concatenate([blk[2 * pr] * blk[2 * pr], blk[2 * pr + 1] * blk[2 * pr + 1]], axis=1)
            ss = _dot(sq.astype(BF16), ones_ref[...])
            out.extend([ss[:, :LANES], ss[:, LANES:]])
    for hd in range(MLA_N_HEADS):
        sl = slice(hd * LANES, (hd + 1) * LANES)
        q_rot = qq[:, n_k + hd * LANES:n_k + (hd + 1) * LANES]
        rq = lax.rsqrt(q_ss[hd] * inv_dim + EPS)
        q_ref[:, sl] = ((q_blk[hd] * q_cos + q_rot * q_sin) * rq).astype(BF16)
        rk = lax.rsqrt(k_ss[hd] * inv_dim + EPS)
        k_ref[:, sl] = ((k_blk[hd] * k_cos + k_rot) * rk).astype(BF16)


def _even_in(x, g, p, cos_t, sin_t):
    b, s, d = x.shape
    tm = TOKEN_TILE
    d_inner, d_xbc = p["d_inner"], p["d_xbc"]
    n_q = MLA_N_HEADS * LANES
    n_v = MLA_N_HEADS * MLA_VT_ROWS
    n_dt = 2 * (d_inner // SSD_HEAD_DIM)
    tok = lambda c: pl.BlockSpec((None, tm, c), lambda bi, i: (bi, i, 0))
    tab = pl.BlockSpec((tm, LANES), lambda bi, i: (i, 0))
    consts = [g, p["w_in"], p["dt_bias"], p["q_a_norm"], p["wq"], p["kv_a_norm"], p["wkvb"],
              p["q_gain_cos"], p["q_gain_sin"], p["k_gain_cos"], p["k_gain_sin"], p["pair_ones"],
              p["conv_w"], p["conv_b"]]
    n_tiles = s // tm
    per = tm // HALO
    n_halo = s // HALO
    halo_prev = pl.BlockSpec((None, HALO, d), lambda bi, i: (bi, jnp.maximum(i * per - 1, 0), 0))
    halo_next = pl.BlockSpec((None, HALO, d), lambda bi, i: (bi, jnp.minimum((i + 1) * per, n_halo - 1), 0))
    return pl.pallas_call(
        functools.partial(_even_in_kernel, d_inner=d_inner, d_xbc=d_xbc, n_tiles=n_tiles),
        out_shape=[
            jax.ShapeDtypeStruct((b, s, d_inner), BF16),
            jax.ShapeDtypeStruct((b, s, d_xbc), F32),
            jax.ShapeDtypeStruct((b, s, LANES), F32),
            jax.ShapeDtypeStruct((b, n_dt, s), F32),
            jax.ShapeDtypeStruct((b, s, n_q), BF16),
            jax.ShapeDtypeStruct((b, s, n_q), BF16),
            jax.ShapeDtypeStruct((b, n_v, s), BF16),
        ],
        grid=(b, s // tm),
        in_specs=[tok(d), halo_prev, halo_next] + [_const_spec(c.shape) for c in consts] + [tab, tab],
        out_specs=[
            tok(d_inner), tok(d_xbc), tok(LANES),
            pl.BlockSpec((None, n_dt, tm), lambda bi, i: (bi, 0, i)),
            tok(n_q), tok(n_q),
            pl.BlockSpec((None, n_v, tm), lambda bi, i: (bi, 0, i)),
        ],
        compiler_params=_params(("parallel", "parallel")),
        name="even_in",
    )(x, x, x, *consts, cos_t, sin_t)


def _ssd_stats(dt, dtt, a_row, a_col, expand, *, reverse):
    L = dt.shape[0]
    keep, keep_t = _scan_masks(L, reverse)
    last = 0 if reverse else L - 1
    keep_bf = keep.astype(F32).astype(BF16)
    keep_t_bf = keep_t.astype(F32).astype(BF16)
    cum = sum(_dot(keep_bf, t) for t in _split_terms(dt * a_row, 3))
    cumt = sum(_dot(t, keep_t_bf) for t in _split_terms(dtt * a_col, 3))
    cum_last = cum[last:last + 1, :]
    narrow = jnp.concatenate(
        [jnp.exp(cum), dt * jnp.exp(cum_last - cum),
         jnp.broadcast_to(jnp.exp(cum_last), (SUBLANES, LANES))], axis=0)
    wide = _dot(narrow.astype(BF16), expand)
    return dict(cum=cum, cumt=cumt, keep=keep, e_off=wide[:L], w_state=wide[L:2 * L],
                chunk_decay=wide[2 * L:2 * L + 1])


def _ssd_main(xbc, dtt, stats, s_ref, d_skip, *, lane_off, d_inner):
    n_heads = d_inner // SSD_HEAD_DIM
    hpg = n_heads // SSD_N_GROUPS
    gw = hpg * SSD_HEAD_DIM
    gn = SSD_D_STATE
    xs = xbc[:, :d_inner]
    cum, cumt, keep = stats["cum"], stats["cumt"], stats["keep"]
    e_off, w_state, chunk_decay = stats["e_off"], stats["w_state"], stats["chunk_decay"]
    lane = lax.broadcasted_iota(jnp.int32, (1, LANES), 1)
    lo_lanes = lane < SSD_HEAD_DIM
    parts = []
    for g in range(SSD_N_GROUPS):
        gsl = slice(g * gw, (g + 1) * gw)
        b_g = xbc[:, d_inner + g * gn:d_inner + (g + 1) * gn]
        c_off = d_inner + SSD_N_GROUPS * gn
        c_bf = xbc[:, c_off + g * gn:c_off + (g + 1) * gn].astype(BF16)
        cb = _dot_nt(c_bf, b_g.astype(BF16))
        state = s_ref[g]
        y_off = _dot(c_bf, state.astype(BF16)) * e_off[:, gsl]
        xw = (xs[:, gsl] * w_state[:, gsl]).astype(BF16)
        s_ref[g] = state * chunk_decay[:, gsl] + _dot(b_g.T.astype(BF16), xw)
        for pr in range(hpg // 2):
            m_pair = []
            for j in range(2):
                hd = g * hpg + 2 * pr + j
                seg = cum[:, lane_off + hd:lane_off + hd + 1] - cumt[hd:hd + 1, :]
                m_h = jnp.where(keep, cb * jnp.exp(seg) * dtt[hd:hd + 1, :], 0.0)
                m_pair.append(m_h.astype(BF16))
            psl = slice(g * gw + pr * LANES, g * gw + (pr + 1) * LANES)
            x_pair = xs[:, psl]
            rhs = jnp.concatenate([jnp.where(lo_lanes, x_pair, 0.0),
                                   jnp.where(lo_lanes, 0.0, x_pair)], axis=0).astype(BF16)
            y_pair = _dot(jnp.concatenate(m_pair, axis=1), rhs) + y_off[:, pr * LANES:(pr + 1) * LANES]
            if d_skip is not None:
                y_pair = y_pair + x_pair * d_skip[:, psl]
            parts.append(y_pair)
    return jnp.concatenate(parts, axis=1).astype(BF16)


def _ssd_kernel(xf_ref, xb_ref, dtf_ref, dtb_ref, dttf_ref, dttb_ref, alog_row_ref, alog_col_ref,
                dskip_ref, ef_ref, eb_ref, yf_ref, yb_ref, sf_ref, sb_ref, *, d_inner):
    @pl.when(pl.program_id(1) == 0)
    def _():
        sf_ref[...] = jnp.zeros(sf_ref.shape, F32)
        sb_ref[...] = jnp.zeros(sb_ref.shape, F32)

    n_heads = d_inner // SSD_HEAD_DIM
    lane = lax.broadcasted_iota(jnp.int32, (1, LANES), 1)
    a_all = -jnp.exp(alog_row_ref[...])
    a_col = -jnp.exp(alog_col_ref[...])
    a_f = jnp.where(lane < n_heads, a_all, 0.0)
    a_b = jnp.where((lane >= n_heads) & (lane < 2 * n_heads), a_all, 0.0)
    chunks = [(slice(c * SCAN_CHUNK, (c + 1) * SCAN_CHUNK),
               slice((SCAN_STEP_CHUNKS - 1 - c) * SCAN_CHUNK, (SCAN_STEP_CHUNKS - c) * SCAN_CHUNK))
              for c in range(SCAN_STEP_CHUNKS)]
    stats = [(_ssd_stats(dtf_ref[fs, :], dttf_ref[0:n_heads, fs], a_f, a_col[0:n_heads, :], ef_ref[...],
                         reverse=False),
              _ssd_stats(dtb_ref[bs, :], dttb_ref[n_heads:2 * n_heads, bs], a_b, a_col[n_heads:2 * n_heads, :],
                         eb_ref[...], reverse=True)) for fs, bs in chunks]
    for (fs, bs), (stats_f, stats_b) in zip(chunks, stats):
        yf_ref[fs, :] = _ssd_main(xf_ref[fs, :], dttf_ref[0:n_heads, fs], stats_f, sf_ref, dskip_ref[...],
                                  lane_off=0, d_inner=d_inner)
        yb_ref[bs, :] = _ssd_main(xb_ref[bs, :], dttb_ref[n_heads:2 * n_heads, bs], stats_b, sb_ref, None,
                                  lane_off=n_heads, d_inner=d_inner)


def _ssd(xbc, dt, dtt, alog_row, alog_col, d_skip, e_f, e_b, *, d_inner):
    b, s, c = xbc.shape
    L = SCAN_CHUNK * SCAN_STEP_CHUNKS
    nc = s // L
    n_dt = dtt.shape[1]
    hpg_w = d_inner // SSD_N_GROUPS
    fwd = lambda w: pl.BlockSpec((None, L, w), lambda bi, i: (bi, i, 0))
    bwd = lambda w: pl.BlockSpec((None, L, w), lambda bi, i: (bi, nc - 1 - i, 0))
    return pl.pallas_call(
        functools.partial(_ssd_kernel, d_inner=d_inner),
        out_shape=[jax.ShapeDtypeStruct((b, s, d_inner), BF16)] * 2,
        grid=(b, nc),
        in_specs=[
            fwd(c), bwd(c), fwd(LANES), bwd(LANES),
            pl.BlockSpec((None, n_dt, L), lambda bi, i: (bi, 0, i)),
            pl.BlockSpec((None, n_dt, L), lambda bi, i: (bi, 0, nc - 1 - i)),
            _const_spec(alog_row.shape), _const_spec(alog_col.shape), _const_spec(d_skip.shape),
            _const_spec(e_f.shape), _const_spec(e_b.shape),
        ],
        out_specs=[fwd(d_inner), bwd(d_inner)],
        scratch_shapes=[pltpu.VMEM((SSD_N_GROUPS, SSD_D_STATE, hpg_w), F32)] * 2,
        compiler_params=_params(("parallel", "arbitrary")),
        name="ssd_scan",
    )(xbc, xbc, dt, dt, dtt, dtt, alog_row, alog_col, d_skip, e_f, e_b)


def _attn_kernel(q_ref, k_ref, vt_ref, o_ref):
    def scores(hd):
        sl = slice(hd * LANES, (hd + 1) * LANES)
        return _dot_nt(k_ref[:, sl], q_ref[:, sl])

    outs = []
    st_next = scores(0)
    for hd in range(MLA_N_HEADS):
        st = st_next
        if hd + 1 < MLA_N_HEADS:
            st_next = scores(hd + 1)
        m = jnp.max(st, axis=0, keepdims=True)
        p = jnp.exp2(st - m).astype(BF16)
        acc = _dot(vt_ref[hd * MLA_VT_ROWS:(hd + 1) * MLA_VT_ROWS, :], p)
        outs.append(acc[:MLA_V] / acc[MLA_V:MLA_V + 1])
    o_ref[...] = jnp.concatenate(outs, axis=0).T.astype(BF16)


def _attention(q, k, vt):
    b, s, n_q = q.shape
    n_vt = vt.shape[1]
    n_v = MLA_N_HEADS * MLA_V
    tq = ATTN_Q_TILE
    return pl.pallas_call(
        _attn_kernel,
        out_shape=jax.ShapeDtypeStruct((b, s, n_v), BF16),
        grid=(b, s // tq),
        in_specs=[
            pl.BlockSpec((None, tq, n_q), lambda bi, i: (bi, i, 0)),
            pl.BlockSpec((None, s, n_q), lambda bi, i: (bi, 0, 0), pipeline_mode=pl.Buffered(1)),
            pl.BlockSpec((None, n_vt, s), lambda bi, i: (bi, 0, 0), pipeline_mode=pl.Buffered(1)),
        ],
        out_specs=pl.BlockSpec((None, tq, n_v), lambda bi, i: (bi, i, 0)),
        compiler_params=_params(("parallel", "arbitrary")),
        name="mla_attention",
    )(q, k, vt)


def _even_out_kernel(x_ref, yf_ref, yb_ref, z_ref, o_ref, g_ref, w_ref, fg_ref, fwin_ref, fwout_ref,
                     out_ref, *, d_inner):
    z = z_ref[...].astype(F32)
    y = (yf_ref[...].astype(F32) + yb_ref[...].astype(F32)) * (z * _sigmoid(z))
    gw = d_inner // SSD_N_GROUPS
    normed = []
    for g in range(SSD_N_GROUPS):
        seg = y[:, g * gw:(g + 1) * gw]
        normed.append(seg * lax.rsqrt(jnp.mean(seg * seg, axis=-1, keepdims=True) + EPS))
    yn = (jnp.concatenate(normed, axis=1) * g_ref[...]).astype(BF16)
    x = x_ref[...] + _dot(yn, w_ref[0:d_inner, :]) + _dot(o_ref[...], w_ref[d_inner:, :])
    out_ref[...] = _ffn_apply(x, fg_ref, fwin_ref, fwout_ref)


def _even_out(x, yf, yb, z, o, g, w, ffn, layer):
    t, d = x.shape
    d_inner = yf.shape[1]
    tm = TOKEN_TILE
    tok = lambda c: pl.BlockSpec((tm, c), lambda i: (i, 0))
    return pl.pallas_call(
        functools.partial(_even_out_kernel, d_inner=d_inner),
        out_shape=jax.ShapeDtypeStruct((t, d), F32),
        grid=(t // tm,),
        in_specs=[tok(d), tok(d_inner), tok(d_inner), tok(d_inner), tok(o.shape[1]),
                  _const_spec(g.shape), _const_spec(w.shape)] + _layer_specs(ffn, layer),
        out_specs=tok(d),
        compiler_params=_params(("parallel",)),
        name="even_out_ffn",
    )(x, yf, yb, z, o, g, w, *ffn)


def _odd_in_kernel(x_ref, g_ref, w_ref, gb_ref, qt_ref, k_ref, kt_ref, v_ref, o_ref, gates_ref, gatest_ref):
    h = _rms(x_ref[...], g_ref[...]).astype(BF16)
    hw = _dot(h, w_ref[...])
    n_qk = ML_N_HEADS * ML_QK
    n_v = ML_N_HEADS * ML_V
    qt_ref[...] = hw[:, :n_qk].T.astype(BF16)
    k = hw[:, n_qk:2 * n_qk]
    k_ref[...] = k.astype(BF16)
    kt_ref[...] = k.T.astype(BF16)
    v_ref[...] = hw[:, 2 * n_qk:2 * n_qk + n_v].astype(BF16)
    o_ref[...] = hw[:, 2 * n_qk + n_v:2 * n_qk + 2 * n_v].astype(BF16)
    pre = hw[:, 2 * n_qk + 2 * n_v:] + gb_ref[...]
    lane = lax.broadcasted_iota(jnp.int32, (1, LANES), 1)
    gates = jnp.where(lane < 2 * ML_N_HEADS, pre, -_softplus(-pre))
    gates_ref[...] = gates
    gatest_ref[...] = gates.T[:gatest_ref.shape[0], :]


def _odd_in(x, g, w, gate_bias):
    b, s, d = x.shape
    tm = TOKEN_TILE
    n_qk = ML_N_HEADS * ML_QK
    n_v = ML_N_HEADS * ML_V
    n_g = 4 * ML_N_HEADS
    tok = lambda c: pl.BlockSpec((None, tm, c), lambda bi, i: (bi, i, 0))
    tok_t = lambda c: pl.BlockSpec((None, c, tm), lambda bi, i: (bi, 0, i))
    return pl.pallas_call(
        _odd_in_kernel,
        out_shape=[
            jax.ShapeDtypeStruct((b, n_qk, s), BF16),
            jax.ShapeDtypeStruct((b, s, n_qk), BF16),
            jax.ShapeDtypeStruct((b, n_qk, s), BF16),
            jax.ShapeDtypeStruct((b, s, n_v), BF16),
            jax.ShapeDtypeStruct((b, s, n_v), BF16),
            jax.ShapeDtypeStruct((b, s, LANES), F32),
            jax.ShapeDtypeStruct((b, n_g, s), F32),
        ],
        grid=(b, s // tm),
        in_specs=[tok(d), _const_spec(g.shape), _const_spec(w.shape), _const_spec(gate_bias.shape)],
        out_specs=[tok_t(n_qk), tok(n_qk), tok_t(n_qk), tok(n_v), tok(n_v), tok(LANES), tok_t(n_g)],
        compiler_params=_params(("parallel", "parallel")),
        name="odd_in",
    )(x, g, w, gate_bias)


def _cummax_lanes(u, reverse):
    n = u.shape[1]
    lane = lax.broadcasted_iota(jnp.int32, (1, n), 1)
    d = 1
    while d < n:
        if reverse:
            shifted, valid = pltpu.roll(u, n - d, axis=1), lane < n - d
        else:
            shifted, valid = pltpu.roll(u, d, axis=1), lane >= d
        u = jnp.where(valid, jnp.maximum(u, shifted), u)
        d *= 2
    return u


def _mlstm_stats(gates, gatest, sel_u, m_ref, *, i_off, f_off, reverse):
    L = gates.shape[0]
    H = ML_N_HEADS
    assert L == LANES == ML_V and 2 * ML_QK == LANES
    keep, keep_t = _scan_masks(L, reverse)
    keep_bf = keep.astype(F32).astype(BF16)
    keep_t_bf = keep_t.astype(F32).astype(BF16)
    last = 0 if reverse else L - 1

    li_r = gatest[i_off:i_off + H, :]
    bcum_r = sum(_dot(t, keep_t_bf) for t in _split_terms(gatest[f_off:f_off + H, :], 3))
    u_r = li_r - bcum_r
    m_in = m_ref[...]
    mx = jnp.maximum(m_in, _cummax_lanes(u_r, reverse))
    w_inter = jnp.exp(m_in - mx)
    e_negm = jnp.exp(-(bcum_r + mx))
    b_end = jnp.broadcast_to(bcum_r[:, last:last + 1], (H, L))
    g_end = b_end + u_r
    m_new = jnp.maximum(b_end + m_in, jnp.max(g_end, axis=1, keepdims=True))
    w_src = jnp.exp(g_end - m_new)
    decay = jnp.exp(b_end + m_in - m_new)
    m_ref[...] = m_new

    bcum_c = sum(_dot(keep_bf, t) for t in _split_terms(gates, 3))
    u_all = sum(_dot(t, sel_u) for t in _split_terms(jnp.concatenate([gates, bcum_c], axis=1), 2))
    return dict(mx=mx, w_inter=w_inter, e_negm=e_negm, w_src=w_src, decay=decay, u_all=u_all, keep_t=keep_t)


def _mlstm_main(k, kt, qt, v, stats, c_ref):
    L = k.shape[0]
    H = ML_N_HEADS
    mx, w_inter, e_negm, w_src = stats["mx"], stats["w_inter"], stats["e_negm"], stats["w_src"]
    decay, u_all, keep_t = stats["decay"], stats["u_all"], stats["keep_t"]
    srow = lax.broadcasted_iota(jnp.int32, (LANES, 1), 0)
    first = srow < ML_QK
    ones = jnp.ones((L, ML_V), BF16)
    c_in = [c_ref[pr] for pr in range(H // 2)]
    qt_own, kt_own, st = [], [], []
    for pr in range(H // 2):
        psl = slice(pr * LANES, (pr + 1) * LANES)
        zero = jnp.zeros((LANES, L), BF16)
        qt_own += [jnp.where(first, qt[psl, :], zero), jnp.where(first, zero, qt[psl, :])]
        kt_own += [jnp.where(first, kt[psl, :], zero), jnp.where(first, zero, kt[psl, :])]
        st.append(_dot(k[:, psl], jnp.concatenate(qt_own[-2:], axis=1)))
    lhs_t = []
    for hd in range(H):
        row = lambda a: a[hd:hd + 1, :]
        c_aug = c_in[hd // 2]
        w = jnp.where(keep_t, jnp.exp(u_all[:, hd * LANES:(hd + 1) * LANES] - row(mx)), 0.0)
        sw = st[hd // 2][:, (hd % 2) * L:(hd % 2 + 1) * L] * w
        q_f32 = qt_own[hd].astype(F32)
        nq = jnp.sum(c_aug[:, ML_V:] * q_f32, axis=0, keepdims=True)
        den = jnp.sum(sw, axis=0, keepdims=True) + row(w_inter) * nq
        inv = 1.0 / jnp.maximum(jnp.abs(den), row(e_negm))
        lhs_t.append(jnp.concatenate(
            [(sw * inv).astype(BF16), (q_f32 * (row(w_inter) * inv)).astype(BF16)], axis=0))
    outs = []
    for hd in range(H):
        v_h = v[:, hd * ML_V:(hd + 1) * ML_V]
        rhs = jnp.concatenate([v_h, c_in[hd // 2][:, :ML_V].astype(BF16)], axis=0)
        outs.append(_dot_tn(lhs_t[hd], rhs))
    for pr in range(H // 2):
        update = jnp.zeros(c_in[pr].shape, F32)
        for hd in (2 * pr, 2 * pr + 1):
            wkt = (kt_own[hd].astype(F32) * w_src[hd:hd + 1, :]).astype(BF16)
            v_aug = jnp.concatenate([v[:, hd * ML_V:(hd + 1) * ML_V], ones], axis=1)
            update = update + _dot(wkt, v_aug)
        dec = jnp.where(first, decay[2 * pr:2 * pr + 1, :], decay[2 * pr + 1:2 * pr + 2, :])
        c_ref[pr] = c_in[pr] * jnp.concatenate([dec, dec], axis=1) + update
    return jnp.concatenate(outs, axis=1).astype(BF16)


def _mlstm_kernel(kf_ref, ktf_ref, qtf_ref, vf_ref, gf_ref, gtf_ref,
                  kb_ref, ktb_ref, qtb_ref, vb_ref, gb_ref, gtb_ref, self_ref, selb_ref,
                  hf_ref, hb_ref, cf_ref, cb_ref, mf_ref, mb_ref):
    @pl.when(pl.program_id(1) == 0)
    def _():
        for ref in (cf_ref, cb_ref, mf_ref, mb_ref):
            ref[...] = jnp.zeros(ref.shape, F32)

    h = ML_N_HEADS
    chunks = [(slice(c * SCAN_CHUNK, (c + 1) * SCAN_CHUNK),
               slice((SCAN_STEP_CHUNKS - 1 - c) * SCAN_CHUNK, (SCAN_STEP_CHUNKS - c) * SCAN_CHUNK))
              for c in range(SCAN_STEP_CHUNKS)]
    stats = [(_mlstm_stats(gf_ref[fs, :], gtf_ref[:, fs], self_ref[...], mf_ref, i_off=0, f_off=2 * h,
                           reverse=False),
              _mlstm_stats(gb_ref[bs, :], gtb_ref[:, bs], selb_ref[...], mb_ref, i_off=h, f_off=3 * h,
                           reverse=True)) for fs, bs in chunks]
    for (fs, bs), (stats_f, stats_b) in zip(chunks, stats):
        hf_ref[fs, :] = _mlstm_main(kf_ref[fs, :], ktf_ref[:, fs], qtf_ref[:, fs], vf_ref[fs, :], stats_f, cf_ref)
        hb_ref[bs, :] = _mlstm_main(kb_ref[bs, :], ktb_ref[:, bs], qtb_ref[:, bs], vb_ref[bs, :], stats_b, cb_ref)


def _mlstm(qt, k, kt, v, gates, gatest, sel_f, sel_b):
    b, s, n_qk = k.shape
    n_v = v.shape[2]
    n_g = gatest.shape[1]
    L = SCAN_CHUNK * SCAN_STEP_CHUNKS
    nc = s // L
    fwd = lambda w: pl.BlockSpec((None, L, w), lambda bi, i: (bi, i, 0))
    bwd = lambda w: pl.BlockSpec((None, L, w), lambda bi, i: (bi, nc - 1 - i, 0))
    fwd_t = lambda c: pl.BlockSpec((None, c, L), lambda bi, i: (bi, 0, i))
    bwd_t = lambda c: pl.BlockSpec((None, c, L), lambda bi, i: (bi, 0, nc - 1 - i))
    state = pltpu.VMEM((ML_N_HEADS // 2, 2 * ML_QK, 2 * ML_V), F32)
    stab = pltpu.VMEM((ML_N_HEADS, LANES), F32)
    return pl.pallas_call(
        _mlstm_kernel,
        out_shape=[jax.ShapeDtypeStruct((b, s, n_v), BF16)] * 2,
        grid=(b, nc),
        in_specs=[fwd(n_qk), fwd_t(n_qk), fwd_t(n_qk), fwd(n_v), fwd(LANES), fwd_t(n_g),
                  bwd(n_qk), bwd_t(n_qk), bwd_t(n_qk), bwd(n_v), bwd(LANES), bwd_t(n_g),
                  _const_spec(sel_f.shape), _const_spec(sel_b.shape)],
        out_specs=[fwd(n_v), bwd(n_v)],
        scratch_shapes=[state, state, stab, stab],
        compiler_params=_params(("parallel", "arbitrary")),
        name="mlstm_scan",
    )(k, kt, qt, v, gates, gatest, k, kt, qt, v, gates, gatest, sel_f, sel_b)


def _odd_out_kernel(x_ref, hf_ref, hb_ref, o_ref, g_ref, w_ref, fg_ref, fwin_ref, fwout_ref, out_ref):
    hs = hf_ref[...].astype(F32) + hb_ref[...].astype(F32)
    normed = []
    for hd in range(ML_N_HEADS):
        seg = hs[:, hd * ML_V:(hd + 1) * ML_V]
        normed.append(seg * lax.rsqrt(jnp.mean(seg * seg, axis=-1, keepdims=True) + EPS))
    gated = (_sigmoid(o_ref[...].astype(F32)) * (jnp.concatenate(normed, axis=1) * g_ref[...])).astype(BF16)
    x = x_ref[...] + _dot(gated, w_ref[...])
    out_ref[...] = _ffn_apply(x, fg_ref, fwin_ref, fwout_ref)


def _odd_out(x, hf, hb, o, g, w, ffn, layer):
    t, d = x.shape
    n_v = hf.shape[1]
    tm = TOKEN_TILE
    tok = lambda c: pl.BlockSpec((tm, c), lambda i: (i, 0))
    return pl.pallas_call(
        _odd_out_kernel,
        out_shape=jax.ShapeDtypeStruct((t, d), F32),
        grid=(t // tm,),
        in_specs=[tok(d), tok(n_v), tok(n_v), tok(n_v), _const_spec(g.shape), _const_spec(w.shape)]
        + _layer_specs(ffn, layer),
        out_specs=tok(d),
        compiler_params=_params(("parallel",)),
        name="odd_out_ffn",
    )(x, hf, hb, o, g, w, *ffn)


def _row(v):
    return v.reshape(1, -1).astype(F32)


def _pad_lanes(v, width=LANES):
    return jnp.pad(v, [(0, 0)] * (v.ndim - 1) + [(0, width - v.shape[-1])])


def _swap_halves(v):
    half = v.shape[-1] // 2
    return jnp.concatenate([v[..., half:], v[..., :half]], axis=-1)


def _on_rope_lanes(v):
    return jnp.pad(v, [(0, 0)] * (v.ndim - 1) + [(MLA_NOPE, LANES - MLA_NOPE - v.shape[-1])])


def _prep_even(ev_w_in, ev_w_out, conv_w, conv_b, dt_bias, a_log, d_skip, ssd_norm,
               q_a_norm, w_q_b, kv_a_norm, w_kv_b, q_norm, k_norm):
    d_inner = ssd_norm.shape[0]
    d_xbc = conv_b.shape[0]
    n_heads = a_log.shape[1]
    c0, c1 = d_inner, d_inner + d_xbc
    c2 = c1 + 2 * n_heads
    c3 = c2 + MLA_Q_RANK
    c4 = c3 + MLA_KV_RANK
    w_kpe = ev_w_in[:, c4:]
    small = jnp.concatenate([_pad_lanes(ev_w_in[:, c1:c2], MLA_NOPE), _pad_lanes(w_kpe, LANES - MLA_NOPE)], axis=1)
    w_in = jnp.concatenate([ev_w_in[:, :c1], ev_w_in[:, c2:c4], small,
                            _on_rope_lanes(_swap_halves(w_kpe))], axis=1).astype(BF16)
    d_qk = MLA_NOPE + MLA_ROPE
    wq3 = w_q_b.reshape(MLA_Q_RANK, MLA_N_HEADS, d_qk)
    wq = jnp.concatenate(
        [_pad_lanes(wq3).reshape(MLA_Q_RANK, -1),
         _on_rope_lanes(_swap_halves(wq3[:, :, MLA_NOPE:])).reshape(MLA_Q_RANK, -1)], axis=1).astype(BF16)
    q_scale = d_qk ** -0.5 * LOG2_E
    ones_blk = jnp.ones((LANES, LANES), BF16)
    zero_blk = jnp.zeros((LANES, LANES), BF16)
    pair_ones = jnp.concatenate([jnp.concatenate([ones_blk, zero_blk], axis=1),
                                 jnp.concatenate([zero_blk, ones_blk], axis=1)], axis=0)
    wkv = w_kv_b.reshape(MLA_KV_RANK, MLA_N_HEADS, MLA_NOPE + MLA_V)
    wkvb = jnp.concatenate(
        [_pad_lanes(wkv[:, :, :MLA_NOPE]).reshape(MLA_KV_RANK, -1),
         wkv[:, :, MLA_NOPE:].reshape(MLA_KV_RANK, -1)], axis=1).astype(BF16)
    head_of_lane = jnp.arange(d_inner) // SSD_HEAD_DIM
    e_f = (jnp.arange(LANES)[:, None] == head_of_lane[None, :]).astype(BF16)
    e_b = (jnp.arange(LANES)[:, None] == head_of_lane[None, :] + n_heads).astype(BF16)
    return dict(
        d_inner=d_inner, d_xbc=d_xbc, w_in=w_in, w_out=ev_w_out.astype(BF16),
        conv_w=_pad_lanes(conv_w.T, SUBLANES).T.astype(F32), conv_b=_row(conv_b),
        dt_bias=_pad_lanes(_row(dt_bias)), alog_row=_pad_lanes(_row(a_log)),
        alog_col=a_log.reshape(-1, 1).astype(F32), d_skip=_row(jnp.repeat(d_skip, SSD_HEAD_DIM)),
        ssd_norm=_row(ssd_norm), e_f=e_f, e_b=e_b,
        q_a_norm=_row(q_a_norm), wq=wq, kv_a_norm=_row(kv_a_norm), wkvb=wkvb, pair_ones=pair_ones,
        q_gain_cos=_pad_lanes(_row(q_norm)) * q_scale,
        q_gain_sin=_on_rope_lanes(_swap_halves(_row(q_norm)[:, MLA_NOPE:])) * q_scale,
        k_gain_cos=_pad_lanes(_row(k_norm)),
        k_gain_sin=_on_rope_lanes(_swap_halves(_row(k_norm)[:, MLA_NOPE:])))


def _rope_tables(s):
    pos = jnp.arange(s, dtype=F32)
    inv_freq = jnp.power(ROPE_BASE, -jnp.arange(0, MLA_ROPE, 2, dtype=F32) / MLA_ROPE)
    freqs = pos[:, None] * inv_freq[None, :]
    cos, sin = jnp.cos(freqs), jnp.sin(freqs)
    tail = jnp.ones((s, LANES - MLA_NOPE - MLA_ROPE), F32)
    cos_t = jnp.concatenate([jnp.ones((s, MLA_NOPE), F32), cos, cos, tail], axis=1)
    sin_t = _on_rope_lanes(jnp.concatenate([-sin, sin], axis=1))
    return cos_t, sin_t


def _even_mixer(x, norm_g, p, ffn2, layer):
    b, s, d = x.shape
    z, xbc, dt, dtt, q, k, vt = _even_in(x, norm_g, p, *_rope_tables(s))
    yf, yb = _ssd(xbc, dt, dtt, p["alog_row"], p["alog_col"], p["d_skip"], p["e_f"], p["e_b"],
                  d_inner=p["d_inner"])
    o = _attention(q, k, vt)
    flat = lambda a: a.reshape(b * s, a.shape[-1])
    return _even_out(flat(x), flat(yf), flat(yb), flat(z), flat(o), p["ssd_norm"],
                     p["w_out"], ffn2, layer).reshape(b, s, d)


def _prep_odd(od_w_in, od_w_out, ig_bias, fg_bias, ml_norm):
    n_qk = ML_N_HEADS * ML_QK
    n_main = 2 * n_qk + 2 * ML_N_HEADS * ML_V
    w_in = jnp.concatenate([od_w_in[:, :n_qk] * ML_QK ** -0.5, od_w_in[:, n_qk:n_main],
                            _pad_lanes(od_w_in[:, n_main:])], axis=1).astype(BF16)
    gate_bias = _pad_lanes(_row(jnp.concatenate([ig_bias.reshape(-1), fg_bias.reshape(-1)])))
    h = ML_N_HEADS
    head_of_col = jnp.arange(h * LANES) // LANES
    rows = jnp.arange(2 * LANES)[:, None]

    def sel(i_off, f_off):
        return ((rows == head_of_col[None, :] + i_off).astype(F32)
                - (rows == head_of_col[None, :] + LANES + f_off).astype(F32)).astype(BF16)

    return dict(w_in=w_in, w_out=od_w_out.astype(BF16), gate_bias=gate_bias, ml_norm=_row(ml_norm),
                sel_f=sel(0, 2 * h), sel_b=sel(h, 3 * h))


def _odd_mixer(x, norm_g, p, ffn2, layer):
    b, s, d = x.shape
    qt, k, kt, v, o, gates, gatest = _odd_in(x, norm_g, p["w_in"], p["gate_bias"])
    hf, hb = _mlstm(qt, k, kt, v, gates, gatest, p["sel_f"], p["sel_b"])
    flat = lambda a: a.reshape(b * s, a.shape[-1])
    return _odd_out(flat(x), flat(hf), flat(hb), flat(o), p["ml_norm"], p["w_out"], ffn2, layer).reshape(b, s, d)


def kernel(x_prompt, x_sample, ffn1_norm, ffn1_w_in, ffn1_w_out, mix_norm, ffn2_norm, ffn2_w_in, ffn2_w_out,
           ev_w_in, ev_w_out, ssd_conv_w, ssd_conv_b, ssd_dt_bias, ssd_a_log, ssd_d_skip, ssd_norm,
           mla_q_a_norm, mla_w_q_b, mla_kv_a_norm, mla_w_kv_b, mla_q_norm, mla_k_norm,
           od_w_in, od_w_out, ml_ig_bias, ml_fg_bias, ml_norm):
    depth = ffn1_norm.shape[0]
    layers = []
    for layer in range(depth):
        j = layer // 2
        if layer % 2 == 0:
            mixer = functools.partial(_even_mixer, p=_prep_even(
                ev_w_in[j], ev_w_out[j], ssd_conv_w[j], ssd_conv_b[j], ssd_dt_bias[j], ssd_a_log[j],
                ssd_d_skip[j], ssd_norm[j], mla_q_a_norm[j], mla_w_q_b[j], mla_kv_a_norm[j],
                mla_w_kv_b[j], mla_q_norm[j], mla_k_norm[j]))
        else:
            mixer = functools.partial(_odd_mixer, p=_prep_odd(
                od_w_in[j], od_w_out[j], ml_ig_bias[j], ml_fg_bias[j], ml_norm[j]))
        layers.append(dict(mixer=mixer, mix_norm=_row(mix_norm[layer])))
    ffn1 = (ffn1_norm[:, None, :].astype(F32), ffn1_w_in.astype(BF16), ffn1_w_out.astype(BF16))
    ffn2 = (ffn2_norm[:, None, :].astype(F32), ffn2_w_in.astype(BF16), ffn2_w_out.astype(BF16))

    def trunk(x):
        b, s, d = x.shape
        for layer, lp in enumerate(layers):
            x = _ffn(x.reshape(b * s, d), ffn1, layer).reshape(b, s, d)
            x = lp["mixer"](x, lp["mix_norm"], ffn2=ffn2, layer=layer)
        return x

    return trunk(x_prompt), trunk(x_sample)
```

```python
import functools

import jax
import jax.numpy as jnp
from jax import lax
from jax.experimental import pallas as pl
from jax.experimental.pallas import tpu as pltpu

F32 = jnp.float32
BF16 = jnp.bfloat16
EPS = 1e-6

LANES = 128
SUBLANES = 8
VMEM_LIMIT = 56 * 1024 * 1024

SSD_HEAD_DIM = 64
SSD_N_GROUPS = 2
SSD_D_STATE = 128
SSD_CONV = 5
MLA_N_HEADS = 8
MLA_Q_RANK = 384
MLA_KV_RANK = 256
MLA_NOPE = 64
MLA_ROPE = 32
MLA_V = 64
MLA_VT_ROWS = MLA_V + 16
ROPE_BASE = 10000.0
LOG2_E = 1.4426950408889634
SAFE_LOG2_SCORE = 60.0
ML_N_HEADS = 8
ML_QK = 64
ML_V = 128

TOKEN_TILE = 512
FFN_TOKEN_TILE = 1024
SCAN_CHUNK = 128
SCAN_STEP_CHUNKS = 2
ATTN_Q_TILE = 512
HALO = SUBLANES


def _params(semantics):
    return pltpu.CompilerParams(dimension_semantics=semantics, vmem_limit_bytes=VMEM_LIMIT)


def _const_spec(shape):
    nd = len(shape)
    return pl.BlockSpec(shape, lambda *_: (0,) * nd, pipeline_mode=pl.Buffered(1))


def _rms(x, g):
    ms = jnp.mean(x * x, axis=-1, keepdims=True)
    return x * lax.rsqrt(ms + EPS) * g


def _sigmoid(x):
    return 1.0 / (1.0 + jnp.exp(-x))


def _softplus(x):
    return jnp.maximum(x, 0.0) + jnp.log1p(jnp.exp(-jnp.abs(x)))


def _dot(a, b):
    return jnp.dot(a, b, preferred_element_type=F32)


def _dot_nt(a, b):
    return lax.dot_general(a, b, (((1,), (1,)), ((), ())), preferred_element_type=F32)


def _dot_tn(a, b):
    return lax.dot_general(a, b, (((0,), (0,)), ((), ())), preferred_element_type=F32)


def _split_terms(x, terms):
    out = []
    for _ in range(terms):
        piece = x.astype(BF16)
        out.append(piece)
        x = x - piece.astype(F32)
    return out


def _scan_masks(n, reverse):
    row = lax.broadcasted_iota(jnp.int32, (n, n), 0)
    col = lax.broadcasted_iota(jnp.int32, (n, n), 1)
    if reverse:
        return col >= row, col <= row
    return col <= row, col >= row


def _ffn_apply(x, g_ref, win_ref, wout_ref):
    d_ff = wout_ref.shape[0]
    h = _rms(x, g_ref[...]).astype(BF16)
    hw = _dot(h, win_ref[...])
    gate = hw[:, :d_ff]
    up = hw[:, d_ff:]
    a = (gate * _sigmoid(gate) * up).astype(BF16)
    return x + 0.5 * _dot(a, wout_ref[...])


def _ffn_kernel(x_ref, g_ref, win_ref, wout_ref, o_ref):
    o_ref[...] = _ffn_apply(x_ref[...], g_ref, win_ref, wout_ref)


def _layer_specs(ffn, layer):
    return [pl.BlockSpec((None,) + a.shape[1:], lambda *_, n=a.ndim: (layer,) + (0,) * (n - 1),
                         pipeline_mode=pl.Buffered(1)) for a in ffn]


def _ffn(x, ffn, layer):
    t, d = x.shape
    tm = FFN_TOKEN_TILE
    return pl.pallas_call(
        _ffn_kernel,
        out_shape=jax.ShapeDtypeStruct((t, d), F32),
        grid=(t // tm,),
        in_specs=[pl.BlockSpec((tm, d), lambda i: (i, 0))] + _layer_specs(ffn, layer),
        out_specs=pl.BlockSpec((tm, d), lambda i: (i, 0)),
        compiler_params=_params(("parallel",)),
        name="ffn",
    )(x, *ffn)


def _conv_silu(ext, w_ref, b_ref):
    n = ext.shape[0]
    tc = n - 2 * HALO
    pad = SSD_CONV // 2
    acc = b_ref[...] + w_ref[pad:pad + 1, :] * ext[HALO:HALO + tc, :]
    for k in range(SSD_CONV):
        if k != pad:
            acc = acc + w_ref[k:k + 1, :] * pltpu.roll(ext, (pad - k) % n, axis=0)[HALO:HALO + tc, :]
    return acc * _sigmoid(acc)


def _even_in_kernel(x_ref, xp_ref, xn_ref, g_ref, w_ref, dtb_ref, qan_ref, wq_ref, kvan_ref, wkvb_ref,
                    qgc_ref, qgs_ref, kgc_ref, kgs_ref, ones_ref, cw_ref, cb_ref, cos_ref, sin_ref,
                    z_ref, xbc_ref, dt_ref, dtt_ref, q_ref, k_ref, vt_ref,
                    *, d_inner, d_xbc, n_tiles):
    i = pl.program_id(1)
    tm = x_ref.shape[0]
    xe = jnp.concatenate([xp_ref[...], x_ref[...], xn_ref[...]], axis=0)
    hwe = _dot(_rms(xe, g_ref[...]).astype(BF16), w_ref[...])
    hw = hwe[HALO:HALO + tm, :]
    c0 = d_inner
    c1 = c0 + d_xbc
    c2 = c1 + MLA_Q_RANK
    c3 = c2 + MLA_KV_RANK
    z_ref[...] = hw[:, :c0].astype(BF16)
    zero = jnp.zeros((HALO, d_xbc), F32)
    ext = jnp.concatenate([jnp.where(i > 0, hwe[:HALO, c0:c1], zero), hw[:, c0:c1],
                           jnp.where(i < n_tiles - 1, hwe[HALO + tm:, c0:c1], zero)], axis=0)
    xbc_ref[...] = _conv_silu(ext, cw_ref, cb_ref)
    small = hw[:, c3:c3 + LANES]
    kpe_sw = hw[:, c3 + LANES:c3 + 2 * LANES]
    dt = _softplus(small + dtb_ref[...])
    dt_ref[...] = dt
    dtt_ref[...] = dt.T[:dtt_ref.shape[0], :]

    n_k = MLA_N_HEADS * LANES
    qa = _rms(hw[:, c1:c2], qan_ref[...]).astype(BF16)
    qq = _dot(qa, wq_ref[...])
    kva = _rms(hw[:, c2:c3], kvan_ref[...]).astype(BF16)
    kv = _dot(kva, wkvb_ref[...])
    v_t = kv[:, n_k:].T
    ones_rows = jnp.ones((MLA_VT_ROWS - MLA_V, v_t.shape[1]), F32)
    vt_ref[...] = jnp.concatenate(
        [blk for hd in range(MLA_N_HEADS) for blk in (v_t[hd * MLA_V:(hd + 1) * MLA_V, :], ones_rows)],
        axis=0).astype(BF16)

    lane = lax.broadcasted_iota(jnp.int32, (1, LANES), 1)
    pe_lanes = (lane >= MLA_NOPE) & (lane < MLA_NOPE + MLA_ROPE)
    kpe = jnp.where(pe_lanes, small, 0.0)
    cos_t, sin_t = cos_ref[...], sin_ref[...]
    q_cos, q_sin = cos_t * qgc_ref[...], sin_t * qgs_ref[...]
    k_cos, k_sin = cos_t * kgc_ref[...], sin_t * kgs_ref[...]
    k_rot = kpe_sw * k_sin
    inv_dim = 1.0 / (MLA_NOPE + MLA_ROPE)

    q_blk = [qq[:, hd * LANES:(hd + 1) * LANES] for hd in range(MLA_N_HEADS)]
    k_blk = [kv[:, hd * LANES:(hd + 1) * LANES] + kpe for hd in range(MLA_N_HEADS)]
    q_ss, k_ss = [], []
    for pr in range(MLA_N_HEADS // 2):
        for blk, out in ((q_blk, q_ss), (k_blk, k_ss)):
            sq = jnp.concatenate([blk[2 * pr] * blk[2 * pr], blk[2 * pr + 1] * blk[2 * pr + 1]], axis=1)
            ss = _dot(sq.astype(BF16), ones_ref[...])
            out.extend([ss[:, :LANES], ss[:, LANES:]])
    for hd in range(MLA_N_HEADS):
        sl = slice(hd * LANES, (hd + 1) * LANES)
        q_rot = qq[:, n_k + hd * LANES:n_k + (hd + 1) * LANES]
        rq = lax.rsqrt(q_ss[hd] * inv_dim + EPS)
        q_ref[:, sl] = ((q_blk[hd] * q_cos + q_rot * q_sin) * rq).astype(BF16)
        rk = lax.rsqrt(k_ss[hd] * inv_dim + EPS)
        k_ref[:, sl] = ((k_blk[hd] * k_cos + k_rot) * rk).astype(BF16)


def _even_in(x, g, p, cos_t, sin_t):
    b, s, d = x.shape
    tm = TOKEN_TILE
    d_inner, d_xbc = p["d_inner"], p["d_xbc"]
    n_q = MLA_N_HEADS * LANES
    n_v = MLA_N_HEADS * MLA_VT_ROWS
    n_dt = 2 * (d_inner // SSD_HEAD_DIM)
    tok = lambda c: pl.BlockSpec((None, tm, c), lambda bi, i: (bi, i, 0))
    tab = pl.BlockSpec((tm, LANES), lambda bi, i: (i, 0))
    consts = [g, p["w_in"], p["dt_bias"], p["q_a_norm"], p["wq"], p["kv_a_norm"], p["wkvb"],
              p["q_gain_cos"], p["q_gain_sin"], p["k_gain_cos"], p["k_gain_sin"], p["pair_ones"],
              p["conv_w"], p["conv_b"]]
    n_tiles = s // tm
    per = tm // HALO
    n_halo = s // HALO
    halo_prev = pl.BlockSpec((None, HALO, d), lambda bi, i: (bi, jnp.maximum(i * per - 1, 0), 0))
    halo_next = pl.BlockSpec((None, HALO, d), lambda bi, i: (bi, jnp.minimum((i + 1) * per, n_halo - 1), 0))
    return pl.pallas_call(
        functools.partial(_even_in_kernel, d_inner=d_inner, d_xbc=d_xbc, n_tiles=n_tiles),
        out_shape=[
            jax.ShapeDtypeStruct((b, s, d_inner), BF16),
            jax.ShapeDtypeStruct((b, s, d_xbc), F32),
            jax.ShapeDtypeStruct((b, s, LANES), F32),
            jax.ShapeDtypeStruct((b, n_dt, s), F32),
            jax.ShapeDtypeStruct((b, s, n_q), BF16),
            jax.ShapeDtypeStruct((b, s, n_q), BF16),
            jax.ShapeDtypeStruct((b, n_v, s), BF16),
        ],
        grid=(b, s // tm),
        in_specs=[tok(d), halo_prev, halo_next] + [_const_spec(c.shape) for c in consts] + [tab, tab],
        out_specs=[
            tok(d_inner), tok(d_xbc), tok(LANES),
            pl.BlockSpec((None, n_dt, tm), lambda bi, i: (bi, 0, i)),
            tok(n_q), tok(n_q),
            pl.BlockSpec((None, n_v, tm), lambda bi, i: (bi, 0, i)),
        ],
        compiler_params=_params(("parallel", "parallel")),
        name="even_in",
    )(x, x, x, *consts, cos_t, sin_t)


def _ssd_stats(dt, dtt, a_row, a_col, expand, *, reverse):
    L = dt.shape[0]
    keep, keep_t = _scan_masks(L, reverse)
    last = 0 if reverse else L - 1
    keep_bf = keep.astype(F32).astype(BF16)
    keep_t_bf = keep_t.astype(F32).astype(BF16)
    cum = sum(_dot(keep_bf, t) for t in _split_terms(dt * a_row, 3))
    cumt = sum(_dot(t, keep_t_bf) for t in _split_terms(dtt * a_col, 3))
    cum_last = cum[last:last + 1, :]
    narrow = jnp.concatenate(
        [jnp.exp(cum), dt * jnp.exp(cum_last - cum),
         jnp.broadcast_to(jnp.exp(cum_last), (SUBLANES, LANES))], axis=0)
    wide = _dot(narrow.astype(BF16), expand)
    return dict(cum=cum, cumt=cumt, keep=keep, e_off=wide[:L], w_state=wide[L:2 * L],
                chunk_decay=wide[2 * L:2 * L + 1])


def _ssd_main(xbc, dtt, stats, s_ref, d_skip, *, lane_off, d_inner):
    n_heads = d_inner // SSD_HEAD_DIM
    hpg = n_heads // SSD_N_GROUPS
    gw = hpg * SSD_HEAD_DIM
    gn = SSD_D_STATE
    xs = xbc[:, :d_inner]
    cum, cumt, keep = stats["cum"], stats["cumt"], stats["keep"]
    e_off, w_state, chunk_decay = stats["e_off"], stats["w_state"], stats["chunk_decay"]
    lane = lax.broadcasted_iota(jnp.int32, (1, LANES), 1)
    lo_lanes = lane < SSD_HEAD_DIM
    parts = []
    for g in range(SSD_N_GROUPS):
        gsl = slice(g * gw, (g + 1) * gw)
        b_g = xbc[:, d_inner + g * gn:d_inner + (g + 1) * gn]
        c_off = d_inner + SSD_N_GROUPS * gn
        c_bf = xbc[:, c_off + g * gn:c_off + (g + 1) * gn].astype(BF16)
        cb = _dot_nt(c_bf, b_g.astype(BF16))
        state = s_ref[g]
        y_off = _dot(c_bf, state.astype(BF16)) * e_off[:, gsl]
        xw = (xs[:, gsl] * w_state[:, gsl]).astype(BF16)
        s_ref[g] = state * chunk_decay[:, gsl] + _dot(b_g.T.astype(BF16), xw)
        for pr in range(hpg // 2):
            m_pair = []
            for j in range(2):
                hd = g * hpg + 2 * pr + j
                seg = cum[:, lane_off + hd:lane_off + hd + 1] - cumt[hd:hd + 1, :]
                m_h = jnp.where(keep, cb * jnp.exp(seg) * dtt[hd:hd + 1, :], 0.0)
                m_pair.append(m_h.astype(BF16))
            psl = slice(g * gw + pr * LANES, g * gw + (pr + 1) * LANES)
            x_pair = xs[:, psl]
            rhs = jnp.concatenate([jnp.where(lo_lanes, x_pair, 0.0),
                                   jnp.where(lo_lanes, 0.0, x_pair)], axis=0).astype(BF16)
            y_pair = _dot(jnp.concatenate(m_pair, axis=1), rhs) + y_off[:, pr * LANES:(pr + 1) * LANES]
            if d_skip is not None:
                y_pair = y_pair + x_pair * d_skip[:, psl]
            parts.append(y_pair)
    return jnp.concatenate(parts, axis=1).astype(BF16)


def _ssd_kernel(xf_ref, xb_ref, dtf_ref, dtb_ref, dttf_ref, dttb_ref, alog_row_ref, alog_col_ref,
                dskip_ref, ef_ref, eb_ref, yf_ref, yb_ref, sf_ref, sb_ref, *, d_inner):
    @pl.when(pl.program_id(1) == 0)
    def _():
        sf_ref[...] = jnp.zeros(sf_ref.shape, F32)
        sb_ref[...] = jnp.zeros(sb_ref.shape, F32)

    n_heads = d_inner // SSD_HEAD_DIM
    lane = lax.broadcasted_iota(jnp.int32, (1, LANES), 1)
    a_all = -jnp.exp(alog_row_ref[...])
    a_col = -jnp.exp(alog_col_ref[...])
    a_f = jnp.where(lane < n_heads, a_all, 0.0)
    a_b = jnp.where((lane >= n_heads) & (lane < 2 * n_heads), a_all, 0.0)
    chunks = [(slice(c * SCAN_CHUNK, (c + 1) * SCAN_CHUNK),
               slice((SCAN_STEP_CHUNKS - 1 - c) * SCAN_CHUNK, (SCAN_STEP_CHUNKS - c) * SCAN_CHUNK))
              for c in range(SCAN_STEP_CHUNKS)]
    stats = [(_ssd_stats(dtf_ref[fs, :], dttf_ref[0:n_heads, fs], a_f, a_col[0:n_heads, :], ef_ref[...],
                         reverse=False),
              _ssd_stats(dtb_ref[bs, :], dttb_ref[n_heads:2 * n_heads, bs], a_b, a_col[n_heads:2 * n_heads, :],
                         eb_ref[...], reverse=True)) for fs, bs in chunks]
    for (fs, bs), (stats_f, stats_b) in zip(chunks, stats):
        yf_ref[fs, :] = _ssd_main(xf_ref[fs, :], dttf_ref[0:n_heads, fs], stats_f, sf_ref, dskip_ref[...],
                                  lane_off=0, d_inner=d_inner)
        yb_ref[bs, :] = _ssd_main(xb_ref[bs, :], dttb_ref[n_heads:2 * n_heads, bs], stats_b, sb_ref, None,
                                  lane_off=n_heads, d_inner=d_inner)


def _ssd(xbc, dt, dtt, alog_row, alog_col, d_skip, e_f, e_b, *, d_inner):
    b, s, c = xbc.shape
    L = SCAN_CHUNK * SCAN_STEP_CHUNKS
    nc = s // L
    n_dt = dtt.shape[1]
    hpg_w = d_inner // SSD_N_GROUPS
    fwd = lambda w: pl.BlockSpec((None, L, w), lambda bi, i: (bi, i, 0))
    bwd = lambda w: pl.BlockSpec((None, L, w), lambda bi, i: (bi, nc - 1 - i, 0))
    return pl.pallas_call(
        functools.partial(_ssd_kernel, d_inner=d_inner),
        out_shape=[jax.ShapeDtypeStruct((b, s, d_inner), BF16)] * 2,
        grid=(b, nc),
        in_specs=[
            fwd(c), bwd(c), fwd(LANES), bwd(LANES),
            pl.BlockSpec((None, n_dt, L), lambda bi, i: (bi, 0, i)),
            pl.BlockSpec((None, n_dt, L), lambda bi, i: (bi, 0, nc - 1 - i)),
            _const_spec(alog_row.shape), _const_spec(alog_col.shape), _const_spec(d_skip.shape),
            _const_spec(e_f.shape), _const_spec(e_b.shape),
        ],
        out_specs=[fwd(d_inner), bwd(d_inner)],
        scratch_shapes=[pltpu.VMEM((SSD_N_GROUPS, SSD_D_STATE, hpg_w), F32)] * 2,
        compiler_params=_params(("parallel", "arbitrary")),
        name="ssd_scan",
    )(xbc, xbc, dt, dt, dtt, dtt, alog_row, alog_col, d_skip, e_f, e_b)


def _attn_kernel(bounded_ref, q_ref, k_ref, vt_ref, o_ref):
    def scores(hd):
        sl = slice(hd * LANES, (hd + 1) * LANES)
        return _dot_nt(k_ref[:, sl], q_ref[:, sl])

    def attend(subtract_max):
        outs = []
        st_next = scores(0)
        for hd in range(MLA_N_HEADS):
            st = st_next
            if hd + 1 < MLA_N_HEADS:
                st_next = scores(hd + 1)
            if subtract_max:
                st = st - jnp.max(st, axis=0, keepdims=True)
            p = jnp.exp2(st).astype(BF16)
            acc = _dot(vt_ref[hd * MLA_VT_ROWS:(hd + 1) * MLA_VT_ROWS, :], p)
            outs.append(acc[:MLA_V] / acc[MLA_V:MLA_V + 1])
        o_ref[...] = jnp.concatenate(outs, axis=0).T.astype(BF16)

    @pl.when(bounded_ref[0, 0] != 0)
    def _():
        attend(subtract_max=False)

    @pl.when(bounded_ref[0, 0] == 0)
    def _():
        attend(subtract_max=True)


def _attention(bounded, q, k, vt):
    b, s, n_q = q.shape
    n_vt = vt.shape[1]
    n_v = MLA_N_HEADS * MLA_V
    tq = ATTN_Q_TILE
    return pl.pallas_call(
        _attn_kernel,
        out_shape=jax.ShapeDtypeStruct((b, s, n_v), BF16),
        grid=(b, s // tq),
        in_specs=[
            pl.BlockSpec(memory_space=pltpu.SMEM),
            pl.BlockSpec((None, tq, n_q), lambda bi, i: (bi, i, 0)),
            pl.BlockSpec((None, s, n_q), lambda bi, i: (bi, 0, 0), pipeline_mode=pl.Buffered(1)),
            pl.BlockSpec((None, n_vt, s), lambda bi, i: (bi, 0, 0), pipeline_mode=pl.Buffered(1)),
        ],
        out_specs=pl.BlockSpec((None, tq, n_v), lambda bi, i: (bi, i, 0)),
        compiler_params=_params(("parallel", "arbitrary")),
        name="mla_attention",
    )(bounded, q, k, vt)


def _even_out_kernel(x_ref, yf_ref, yb_ref, z_ref, o_ref, g_ref, w_ref, fg_ref, fwin_ref, fwout_ref,
                     out_ref, *, d_inner):
    z = z_ref[...].astype(F32)
    y = (yf_ref[...].astype(F32) + yb_ref[...].astype(F32)) * (z * _sigmoid(z))
    gw = d_inner // SSD_N_GROUPS
    normed = []
    for g in range(SSD_N_GROUPS):
        seg = y[:, g * gw:(g + 1) * gw]
        normed.append(seg * lax.rsqrt(jnp.mean(seg * seg, axis=-1, keepdims=True) + EPS))
    yn = (jnp.concatenate(normed, axis=1) * g_ref[...]).astype(BF16)
    x = x_ref[...] + _dot(yn, w_ref[0:d_inner, :]) + _dot(o_ref[...], w_ref[d_inner:, :])
    out_ref[...] = _ffn_apply(x, fg_ref, fwin_ref, fwout_ref)


def _even_out(x, yf, yb, z, o, g, w, ffn, layer):
    t, d = x.shape
    d_inner = yf.shape[1]
    tm = TOKEN_TILE
    tok = lambda c: pl.BlockSpec((tm, c), lambda i: (i, 0))
    return pl.pallas_call(
        functools.partial(_even_out_kernel, d_inner=d_inner),
        out_shape=jax.ShapeDtypeStruct((t, d), F32),
        grid=(t // tm,),
        in_specs=[tok(d), tok(d_inner), tok(d_inner), tok(d_inner), tok(o.shape[1]),
                  _const_spec(g.shape), _const_spec(w.shape)] + _layer_specs(ffn, layer),
        out_specs=tok(d),
        compiler_params=_params(("parallel",)),
        name="even_out_ffn",
    )(x, yf, yb, z, o, g, w, *ffn)


def _odd_in_kernel(x_ref, g_ref, w_ref, gb_ref, qt_ref, k_ref, kt_ref, v_ref, o_ref, gates_ref, gatest_ref):
    h = _rms(x_ref[...], g_ref[...]).astype(BF16)
    hw = _dot(h, w_ref[...])
    n_qk = ML_N_HEADS * ML_QK
    n_v = ML_N_HEADS * ML_V
    qt_ref[...] = hw[:, :n_qk].T.astype(BF16)
    k = hw[:, n_qk:2 * n_qk]
    k_ref[...] = k.astype(BF16)
    kt_ref[...] = k.T.astype(BF16)
    v_ref[...] = hw[:, 2 * n_qk:2 * n_qk + n_v].astype(BF16)
    o_ref[...] = hw[:, 2 * n_qk + n_v:2 * n_qk + 2 * n_v].astype(BF16)
    pre = hw[:, 2 * n_qk + 2 * n_v:] + gb_ref[...]
    lane = lax.broadcasted_iota(jnp.int32, (1, LANES), 1)
    gates = jnp.where(lane < 2 * ML_N_HEADS, pre, -_softplus(-pre))
    gates_ref[...] = gates
    gatest_ref[...] = gates.T[:gatest_ref.shape[0], :]


def _odd_in(x, g, w, gate_bias):
    b, s, d = x.shape
    tm = TOKEN_TILE
    n_qk = ML_N_HEADS * ML_QK
    n_v = ML_N_HEADS * ML_V
    n_g = 4 * ML_N_HEADS
    tok = lambda c: pl.BlockSpec((None, tm, c), lambda bi, i: (bi, i, 0))
    tok_t = lambda c: pl.BlockSpec((None, c, tm), lambda bi, i: (bi, 0, i))
    return pl.pallas_call(
        _odd_in_kernel,
        out_shape=[
            jax.ShapeDtypeStruct((b, n_qk, s), BF16),
            jax.ShapeDtypeStruct((b, s, n_qk), BF16),
            jax.ShapeDtypeStruct((b, n_qk, s), BF16),
            jax.ShapeDtypeStruct((b, s, n_v), BF16),
            jax.ShapeDtypeStruct((b, s, n_v), BF16),
            jax.ShapeDtypeStruct((b, s, LANES), F32),
            jax.ShapeDtypeStruct((b, n_g, s), F32),
        ],
        grid=(b, s // tm),
        in_specs=[tok(d), _const_spec(g.shape), _const_spec(w.shape), _const_spec(gate_bias.shape)],
        out_specs=[tok_t(n_qk), tok(n_qk), tok_t(n_qk), tok(n_v), tok(n_v), tok(LANES), tok_t(n_g)],
        compiler_params=_params(("parallel", "parallel")),
        name="odd_in",
    )(x, g, w, gate_bias)


def _cummax_lanes(u, reverse):
    n = u.shape[1]
    lane = lax.broadcasted_iota(jnp.int32, (1, n), 1)
    d = 1
    while d < n:
        if reverse:
            shifted, valid = pltpu.roll(u, n - d, axis=1), lane < n - d
        else:
            shifted, valid = pltpu.roll(u, d, axis=1), lane >= d
        u = jnp.where(valid, jnp.maximum(u, shifted), u)
        d *= 2
    return u


def _mlstm_stats(gates, gatest, sel_u, m_ref, *, i_off, f_off, reverse):
    L = gates.shape[0]
    H = ML_N_HEADS
    assert L == LANES == ML_V and 2 * ML_QK == LANES
    keep, keep_t = _scan_masks(L, reverse)
    keep_bf = keep.astype(F32).astype(BF16)
    keep_t_bf = keep_t.astype(F32).astype(BF16)
    last = 0 if reverse else L - 1

    li_r = gatest[i_off:i_off + H, :]
    bcum_r = sum(_dot(t, keep_t_bf) for t in _split_terms(gatest[f_off:f_off + H, :], 3))
    u_r = li_r - bcum_r
    m_in = m_ref[...]
    mx = jnp.maximum(m_in, _cummax_lanes(u_r, reverse))
    w_inter = jnp.exp(m_in - mx)
    e_negm = jnp.exp(-(bcum_r + mx))
    b_end = jnp.broadcast_to(bcum_r[:, last:last + 1], (H, L))
    g_end = b_end + u_r
    m_new = jnp.maximum(b_end + m_in, jnp.max(g_end, axis=1, keepdims=True))
    w_src = jnp.exp(g_end - m_new)
    decay = jnp.exp(b_end + m_in - m_new)
    m_ref[...] = m_new

    bcum_c = sum(_dot(keep_bf, t) for t in _split_terms(gates, 3))
    u_all = sum(_dot(t, sel_u) for t in _split_terms(jnp.concatenate([gates, bcum_c], axis=1), 2))
    return dict(mx=mx, w_inter=w_inter, e_negm=e_negm, w_src=w_src, decay=decay, u_all=u_all, keep_t=keep_t)


def _mlstm_main(k, kt, qt, v, stats, c_ref):
    L = k.shape[0]
    H = ML_N_HEADS
    mx, w_inter, e_negm, w_src = stats["mx"], stats["w_inter"], stats["e_negm"], stats["w_src"]
    decay, u_all, keep_t = stats["decay"], stats["u_all"], stats["keep_t"]
    srow = lax.broadcasted_iota(jnp.int32, (LANES, 1), 0)
    first = srow < ML_QK
    ones = jnp.ones((L, ML_V), BF16)
    c_in = [c_ref[pr] for pr in range(H // 2)]
    qt_own, kt_own, st = [], [], []
    for pr in range(H // 2):
        psl = slice(pr * LANES, (pr + 1) * LANES)
        zero = jnp.zeros((LANES, L), BF16)
        qt_own += [jnp.where(first, qt[psl, :], zero), jnp.where(first, zero, qt[psl, :])]
        kt_own += [jnp.where(first, kt[psl, :], zero), jnp.where(first, zero, kt[psl, :])]
        st.append(_dot(k[:, psl], jnp.concatenate(qt_own[-2:], axis=1)))
    lhs_t = []
    for hd in range(H):
        row = lambda a: a[hd:hd + 1, :]
        c_aug = c_in[hd // 2]
        w = jnp.where(keep_t, jnp.exp(u_all[:, hd * LANES:(hd + 1) * LANES] - row(mx)), 0.0)
        sw = st[hd // 2][:, (hd % 2) * L:(hd % 2 + 1) * L] * w
        q_f32 = qt_own[hd].astype(F32)
        nq = jnp.sum(c_aug[:, ML_V:] * q_f32, axis=0, keepdims=True)
        den = jnp.sum(sw, axis=0, keepdims=True) + row(w_inter) * nq
        inv = 1.0 / jnp.maximum(jnp.abs(den), row(e_negm))
        lhs_t.append(jnp.concatenate(
            [(sw * inv).astype(BF16), (q_f32 * (row(w_inter) * inv)).astype(BF16)], axis=0))
    outs = []
    for hd in range(H):
        v_h = v[:, hd * ML_V:(hd + 1) * ML_V]
        rhs = jnp.concatenate([v_h, c_in[hd // 2][:, :ML_V].astype(BF16)], axis=0)
        outs.append(_dot_tn(lhs_t[hd], rhs))
    for pr in range(H // 2):
        update = jnp.zeros(c_in[pr].shape, F32)
        for hd in (2 * pr, 2 * pr + 1):
            wkt = (kt_own[hd].astype(F32) * w_src[hd:hd + 1, :]).astype(BF16)
            v_aug = jnp.concatenate([v[:, hd * ML_V:(hd + 1) * ML_V], ones], axis=1)
            update = update + _dot(wkt, v_aug)
        dec = jnp.where(first, decay[2 * pr:2 * pr + 1, :], decay[2 * pr + 1:2 * pr + 2, :])
        c_ref[pr] = c_in[pr] * jnp.concatenate([dec, dec], axis=1) + update
    return jnp.concatenate(outs, axis=1).astype(BF16)


def _mlstm_kernel(kf_ref, ktf_ref, qtf_ref, vf_ref, gf_ref, gtf_ref,
                  kb_ref, ktb_ref, qtb_ref, vb_ref, gb_ref, gtb_ref, self_ref, selb_ref,
                  hf_ref, hb_ref, cf_ref, cb_ref, mf_ref, mb_ref):
    @pl.when(pl.program_id(1) == 0)
    def _():
        for ref in (cf_ref, cb_ref, mf_ref, mb_ref):
            ref[...] = jnp.zeros(ref.shape, F32)

    h = ML_N_HEADS
    chunks = [(slice(c * SCAN_CHUNK, (c + 1) * SCAN_CHUNK),
               slice((SCAN_STEP_CHUNKS - 1 - c) * SCAN_CHUNK, (SCAN_STEP_CHUNKS - c) * SCAN_CHUNK))
              for c in range(SCAN_STEP_CHUNKS)]
    stats = [(_mlstm_stats(gf_ref[fs, :], gtf_ref[:, fs], self_ref[...], mf_ref, i_off=0, f_off=2 * h,
                           reverse=False),
              _mlstm_stats(gb_ref[bs, :], gtb_ref[:, bs], selb_ref[...], mb_ref, i_off=h, f_off=3 * h,
                           reverse=True)) for fs, bs in chunks]
    for (fs, bs), (stats_f, stats_b) in zip(chunks, stats):
        hf_ref[fs, :] = _mlstm_main(kf_ref[fs, :], ktf_ref[:, fs], qtf_ref[:, fs], vf_ref[fs, :], stats_f, cf_ref)
        hb_ref[bs, :] = _mlstm_main(kb_ref[bs, :], ktb_ref[:, bs], qtb_ref[:, bs], vb_ref[bs, :], stats_b, cb_ref)


def _mlstm(qt, k, kt, v, gates, gatest, sel_f, sel_b):
    b, s, n_qk = k.shape
    n_v = v.shape[2]
    n_g = gatest.shape[1]
    L = SCAN_CHUNK * SCAN_STEP_CHUNKS
    nc = s // L
    fwd = lambda w: pl.BlockSpec((None, L, w), lambda bi, i: (bi, i, 0))
    bwd = lambda w: pl.BlockSpec((None, L, w), lambda bi, i: (bi, nc - 1 - i, 0))
    fwd_t = lambda c: pl.BlockSpec((None, c, L), lambda bi, i: (bi, 0, i))
    bwd_t = lambda c: pl.BlockSpec((None, c, L), lambda bi, i: (bi, 0, nc - 1 - i))
    state = pltpu.VMEM((ML_N_HEADS // 2, 2 * ML_QK, 2 * ML_V), F32)
    stab = pltpu.VMEM((ML_N_HEADS, LANES), F32)
    return pl.pallas_call(
        _mlstm_kernel,
        out_shape=[jax.ShapeDtypeStruct((b, s, n_v), BF16)] * 2,
        grid=(b, nc),
        in_specs=[fwd(n_qk), fwd_t(n_qk), fwd_t(n_qk), fwd(n_v), fwd(LANES), fwd_t(n_g),
                  bwd(n_qk), bwd_t(n_qk), bwd_t(n_qk), bwd(n_v), bwd(LANES), bwd_t(n_g),
                  _const_spec(sel_f.shape), _const_spec(sel_b.shape)],
        out_specs=[fwd(n_v), bwd(n_v)],
        scratch_shapes=[state, state, stab, stab],
        compiler_params=_params(("parallel", "arbitrary")),
        name="mlstm_scan",
    )(k, kt, qt, v, gates, gatest, k, kt, qt, v, gates, gatest, sel_f, sel_b)


def _odd_out_kernel(x_ref, hf_ref, hb_ref, o_ref, g_ref, w_ref, fg_ref, fwin_ref, fwout_ref, out_ref):
    hs = hf_ref[...].astype(F32) + hb_ref[...].astype(F32)
    normed = []
    for hd in range(ML_N_HEADS):
        seg = hs[:, hd * ML_V:(hd + 1) * ML_V]
        normed.append(seg * lax.rsqrt(jnp.mean(seg * seg, axis=-1, keepdims=True) + EPS))
    gated = (_sigmoid(o_ref[...].astype(F32)) * (jnp.concatenate(normed, axis=1) * g_ref[...])).astype(BF16)
    x = x_ref[...] + _dot(gated, w_ref[...])
    out_ref[...] = _ffn_apply(x, fg_ref, fwin_ref, fwout_ref)


def _odd_out(x, hf, hb, o, g, w, ffn, layer):
    t, d = x.shape
    n_v = hf.shape[1]
    tm = TOKEN_TILE
    tok = lambda c: pl.BlockSpec((tm, c), lambda i: (i, 0))
    return pl.pallas_call(
        _odd_out_kernel,
        out_shape=jax.ShapeDtypeStruct((t, d), F32),
        grid=(t // tm,),
        in_specs=[tok(d), tok(n_v), tok(n_v), tok(n_v), _const_spec(g.shape), _const_spec(w.shape)]
        + _layer_specs(ffn, layer),
        out_specs=tok(d),
        compiler_params=_params(("parallel",)),
        name="odd_out_ffn",
    )(x, hf, hb, o, g, w, *ffn)


def _row(v):
    return v.reshape(1, -1).astype(F32)


def _pad_lanes(v, width=LANES):
    return jnp.pad(v, [(0, 0)] * (v.ndim - 1) + [(0, width - v.shape[-1])])


def _swap_halves(v):
    half = v.shape[-1] // 2
    return jnp.concatenate([v[..., half:], v[..., :half]], axis=-1)


def _on_rope_lanes(v):
    return jnp.pad(v, [(0, 0)] * (v.ndim - 1) + [(MLA_NOPE, LANES - MLA_NOPE - v.shape[-1])])


def _prep_even(ev_w_in, ev_w_out, conv_w, conv_b, dt_bias, a_log, d_skip, ssd_norm,
               q_a_norm, w_q_b, kv_a_norm, w_kv_b, q_norm, k_norm):
    d_inner = ssd_norm.shape[0]
    d_xbc = conv_b.shape[0]
    n_heads = a_log.shape[1]
    c0, c1 = d_inner, d_inner + d_xbc
    c2 = c1 + 2 * n_heads
    c3 = c2 + MLA_Q_RANK
    c4 = c3 + MLA_KV_RANK
    w_kpe = ev_w_in[:, c4:]
    small = jnp.concatenate([_pad_lanes(ev_w_in[:, c1:c2], MLA_NOPE), _pad_lanes(w_kpe, LANES - MLA_NOPE)], axis=1)
    w_in = jnp.concatenate([ev_w_in[:, :c1], ev_w_in[:, c2:c4], small,
                            _on_rope_lanes(_swap_halves(w_kpe))], axis=1).astype(BF16)
    d_qk = MLA_NOPE + MLA_ROPE
    wq3 = w_q_b.reshape(MLA_Q_RANK, MLA_N_HEADS, d_qk)
    wq = jnp.concatenate(
        [_pad_lanes(wq3).reshape(MLA_Q_RANK, -1),
         _on_rope_lanes(_swap_halves(wq3[:, :, MLA_NOPE:])).reshape(MLA_Q_RANK, -1)], axis=1).astype(BF16)
    q_scale = d_qk ** -0.5 * LOG2_E
    score_bound = 1.02 * d_qk * q_scale * jnp.max(jnp.abs(q_norm)) * jnp.max(jnp.abs(k_norm))
    scores_bounded = (score_bound <= SAFE_LOG2_SCORE).astype(jnp.int32).reshape(1, 1)
    ones_blk = jnp.ones((LANES, LANES), BF16)
    zero_blk = jnp.zeros((LANES, LANES), BF16)
    pair_ones = jnp.concatenate([jnp.concatenate([ones_blk, zero_blk], axis=1),
                                 jnp.concatenate([zero_blk, ones_blk], axis=1)], axis=0)
    wkv = w_kv_b.reshape(MLA_KV_RANK, MLA_N_HEADS, MLA_NOPE + MLA_V)
    wkvb = jnp.concatenate(
        [_pad_lanes(wkv[:, :, :MLA_NOPE]).reshape(MLA_KV_RANK, -1),
         wkv[:, :, MLA_NOPE:].reshape(MLA_KV_RANK, -1)], axis=1).astype(BF16)
    head_of_lane = jnp.arange(d_inner) // SSD_HEAD_DIM
    e_f = (jnp.arange(LANES)[:, None] == head_of_lane[None, :]).astype(BF16)
    e_b = (jnp.arange(LANES)[:, None] == head_of_lane[None, :] + n_heads).astype(BF16)
    return dict(
        d_inner=d_inner, d_xbc=d_xbc, w_in=w_in, w_out=ev_w_out.astype(BF16),
        conv_w=_pad_lanes(conv_w.T, SUBLANES).T.astype(F32), conv_b=_row(conv_b),
        dt_bias=_pad_lanes(_row(dt_bias)), alog_row=_pad_lanes(_row(a_log)),
        alog_col=a_log.reshape(-1, 1).astype(F32), d_skip=_row(jnp.repeat(d_skip, SSD_HEAD_DIM)),
        ssd_norm=_row(ssd_norm), e_f=e_f, e_b=e_b,
        q_a_norm=_row(q_a_norm), wq=wq, kv_a_norm=_row(kv_a_norm), wkvb=wkvb, pair_ones=pair_ones,
        scores_bounded=scores_bounded,
        q_gain_cos=_pad_lanes(_row(q_norm)) * q_scale,
        q_gain_sin=_on_rope_lanes(_swap_halves(_row(q_norm)[:, MLA_NOPE:])) * q_scale,
        k_gain_cos=_pad_lanes(_row(k_norm)),
        k_gain_sin=_on_rope_lanes(_swap_halves(_row(k_norm)[:, MLA_NOPE:])))


def _rope_tables(s):
    pos = jnp.arange(s, dtype=F32)
    inv_freq = jnp.power(ROPE_BASE, -jnp.arange(0, MLA_ROPE, 2, dtype=F32) / MLA_ROPE)
    freqs = pos[:, None] * inv_freq[None, :]
    cos, sin = jnp.cos(freqs), jnp.sin(freqs)
    tail = jnp.ones((s, LANES - MLA_NOPE - MLA_ROPE), F32)
    cos_t = jnp.concatenate([jnp.ones((s, MLA_NOPE), F32), cos, cos, tail], axis=1)
    sin_t = _on_rope_lanes(jnp.concatenate([-sin, sin], axis=1))
    return cos_t, sin_t


def _even_mixer(x, norm_g, p, ffn2, layer):
    b, s, d = x.shape
    z, xbc, dt, dtt, q, k, vt = _even_in(x, norm_g, p, *_rope_tables(s))
    yf, yb = _ssd(xbc, dt, dtt, p["alog_row"], p["alog_col"], p["d_skip"], p["e_f"], p["e_b"],
                  d_inner=p["d_inner"])
    o = _attention(p["scores_bounded"], q, k, vt)
    flat = lambda a: a.reshape(b * s, a.shape[-1])
    return _even_out(flat(x), flat(yf), flat(yb), flat(z), flat(o), p["ssd_norm"],
                     p["w_out"], ffn2, layer).reshape(b, s, d)


def _prep_odd(od_w_in, od_w_out, ig_bias, fg_bias, ml_norm):
    n_qk = ML_N_HEADS * ML_QK
    n_main = 2 * n_qk + 2 * ML_N_HEADS * ML_V
    w_in = jnp.concatenate([od_w_in[:, :n_qk] * ML_QK ** -0.5, od_w_in[:, n_qk:n_main],
                            _pad_lanes(od_w_in[:, n_main:])], axis=1).astype(BF16)
    gate_bias = _pad_lanes(_row(jnp.concatenate([ig_bias.reshape(-1), fg_bias.reshape(-1)])))
    h = ML_N_HEADS
    head_of_col = jnp.arange(h * LANES) // LANES
    rows = jnp.arange(2 * LANES)[:, None]

    def sel(i_off, f_off):
        return ((rows == head_of_col[None, :] + i_off).astype(F32)
                - (rows == head_of_col[None, :] + LANES + f_off).astype(F32)).astype(BF16)

    return dict(w_in=w_in, w_out=od_w_out.astype(BF16), gate_bias=gate_bias, ml_norm=_row(ml_norm),
                sel_f=sel(0, 2 * h), sel_b=sel(h, 3 * h))


def _odd_mixer(x, norm_g, p, ffn2, layer):
    b, s, d = x.shape
    qt, k, kt, v, o, gates, gatest = _odd_in(x, norm_g, p["w_in"], p["gate_bias"])
    hf, hb = _mlstm(qt, k, kt, v, gates, gatest, p["sel_f"], p["sel_b"])
    flat = lambda a: a.reshape(b * s, a.shape[-1])
    return _odd_out(flat(x), flat(hf), flat(hb), flat(o), p["ml_norm"], p["w_out"], ffn2, layer).reshape(b, s, d)


def kernel(x_prompt, x_sample, ffn1_norm, ffn1_w_in, ffn1_w_out, mix_norm, ffn2_norm, ffn2_w_in, ffn2_w_out,
           ev_w_in, ev_w_out, ssd_conv_w, ssd_conv_b, ssd_dt_bias, ssd_a_log, ssd_d_skip, ssd_norm,
           mla_q_a_norm, mla_w_q_b, mla_kv_a_norm, mla_w_kv_b, mla_q_norm, mla_k_norm,
           od_w_in, od_w_out, ml_ig_bias, ml_fg_bias, ml_norm):
    depth = ffn1_norm.shape[0]
    layers = []
    for layer in range(depth):
        j = layer // 2
        if layer % 2 == 0:
            mixer = functools.partial(_even_mixer, p=_prep_even(
                ev_w_in[j], ev_w_out[j], ssd_conv_w[j], ssd_conv_b[j], ssd_dt_bias[j], ssd_a_log[j],
                ssd_d_skip[j], ssd_norm[j], mla_q_a_norm[j], mla_w_q_b[j], mla_kv_a_norm[j],
                mla_w_kv_b[j], mla_q_norm[j], mla_k_norm[j]))
        else:
            mixer = functools.partial(_odd_mixer, p=_prep_odd(
                od_w_in[j], od_w_out[j], ml_ig_bias[j], ml_fg_bias[j], ml_norm[j]))
        layers.append(dict(mixer=mixer, mix_norm=_row(mix_norm[layer])))
    ffn1 = (ffn1_norm[:, None, :].astype(F32), ffn1_w_in.astype(BF16), ffn1_w_out.astype(BF16))
    ffn2 = (ffn2_norm[:, None, :].astype(F32), ffn2_w_in.astype(BF16), ffn2_w_out.astype(BF16))

    def trunk(x):
        b, s, d = x.shape
        for layer, lp in enumerate(layers):
            x = _ffn(x.reshape(b * s, d), ffn1, layer).reshape(b, s, d)
            x = lp["mixer"](x, lp["mix_norm"], ffn2=ffn2, layer=layer)
        return x

    return trunk(x_prompt), trunk(x_sample)
```

```python
import functools

import jax
import jax.numpy as jnp
from jax import lax
from jax.experimental import pallas as pl
from jax.experimental.pallas import tpu as pltpu

F32 = jnp.float32
BF16 = jnp.bfloat16
EPS = 1e-6

LANES = 128
SUBLANES = 8
VMEM_LIMIT = 56 * 1024 * 1024

SSD_HEAD_DIM = 64
SSD_N_GROUPS = 2
SSD_D_STATE = 128
SSD_CONV = 5
MLA_N_HEADS = 8
MLA_Q_RANK = 384
MLA_KV_RANK = 256
MLA_NOPE = 64
MLA_ROPE = 32
MLA_V = 64
MLA_VT_ROWS = MLA_V + 16
ROPE_BASE = 10000.0
LOG2_E = 1.4426950408889634
SAFE_LOG2_SCORE = 60.0
ML_N_HEADS = 8
ML_QK = 64
ML_V = 128

TOKEN_TILE = 512
FFN_TOKEN_TILE = 1024
SCAN_CHUNK = 128
SCAN_STEP_CHUNKS = 2
ATTN_Q_TILE = 512
HALO = SUBLANES


def _params(semantics):
    return pltpu.CompilerParams(dimension_semantics=semantics, vmem_limit_bytes=VMEM_LIMIT)


def _const_spec(shape):
    nd = len(shape)
    return pl.BlockSpec(shape, lambda *_: (0,) * nd, pipeline_mode=pl.Buffered(1))


def _rms(x, g):
    ms = jnp.mean(x * x, axis=-1, keepdims=True)
    return x * lax.rsqrt(ms + EPS) * g


def _sigmoid(x):
    return 1.0 / (1.0 + jnp.exp(-x))


def _softplus(x):
    return jnp.maximum(x, 0.0) + jnp.log1p(jnp.exp(-jnp.abs(x)))


def _dot(a, b):
    return jnp.dot(a, b, preferred_element_type=F32)


def _dot_nt(a, b):
    return lax.dot_general(a, b, (((1,), (1,)), ((), ())), preferred_element_type=F32)


def _dot_tn(a, b):
    return lax.dot_general(a, b, (((0,), (0,)), ((), ())), preferred_element_type=F32)


def _split_terms(x, terms):
    out = []
    for _ in range(terms):
        piece = x.astype(BF16)
        out.append(piece)
        x = x - piece.astype(F32)
    return out


def _scan_masks(n, reverse):
    row = lax.broadcasted_iota(jnp.int32, (n, n), 0)
    col = lax.broadcasted_iota(jnp.int32, (n, n), 1)
    if reverse:
        return col >= row, col <= row
    return col <= row, col >= row


def _ffn_apply(x, g_ref, win_ref, wout_ref):
    d_ff = wout_ref.shape[0]
    h = _rms(x, g_ref[...]).astype(BF16)
    hw = _dot(h, win_ref[...])
    gate = hw[:, :d_ff]
    up = hw[:, d_ff:]
    a = (gate * _sigmoid(gate) * up).astype(BF16)
    return x + 0.5 * _dot(a, wout_ref[...])


def _ffn_kernel(x_ref, g_ref, win_ref, wout_ref, o_ref):
    o_ref[...] = _ffn_apply(x_ref[...], g_ref, win_ref, wout_ref)


def _layer_specs(ffn, layer):
    return [pl.BlockSpec((None,) + a.shape[1:], lambda *_, n=a.ndim: (layer,) + (0,) * (n - 1),
                         pipeline_mode=pl.Buffered(1)) for a in ffn]


def _ffn(x, ffn, layer):
    t, d = x.shape
    tm = FFN_TOKEN_TILE
    return pl.pallas_call(
        _ffn_kernel,
        out_shape=jax.ShapeDtypeStruct((t, d), F32),
        grid=(t // tm,),
        in_specs=[pl.BlockSpec((tm, d), lambda i: (i, 0))] + _layer_specs(ffn, layer),
        out_specs=pl.BlockSpec((tm, d), lambda i: (i, 0)),
        compiler_params=_params(("parallel",)),
        name="ffn",
    )(x, *ffn)


def _conv_silu(ext, w_ref, b_ref):
    n = ext.shape[0]
    tc = n - 2 * HALO
    pad = SSD_CONV // 2
    acc = b_ref[...] + w_ref[pad:pad + 1, :] * ext[HALO:HALO + tc, :]
    for k in range(SSD_CONV):
        if k != pad:
            acc = acc + w_ref[k:k + 1, :] * pltpu.roll(ext, (pad - k) % n, axis=0)[HALO:HALO + tc, :]
    return acc * _sigmoid(acc)


def _even_in_kernel(x_ref, xp_ref, xn_ref, g_ref, w_ref, dtb_ref, qan_ref, wq_ref, kvan_ref, wkvb_ref,
                    qgc_ref, qgs_ref, kgc_ref, kgs_ref, ones_ref, cw_ref, cb_ref, cos_ref, sin_ref,
                    z_ref, xbc_ref, dt_ref, dtt_ref, q_ref, k_ref, vt_ref,
                    *, d_inner, d_xbc, n_tiles):
    i = pl.program_id(1)
    tm = x_ref.shape[0]
    xe = jnp.concatenate([xp_ref[...], x_ref[...], xn_ref[...]], axis=0)
    hwe = _dot(_rms(xe, g_ref[...]).astype(BF16), w_ref[...])
    hw = hwe[HALO:HALO + tm, :]
    c0 = d_inner
    c1 = c0 + d_xbc
    c2 = c1 + MLA_Q_RANK
    c3 = c2 + MLA_KV_RANK
    z_ref[...] = hw[:, :c0].astype(BF16)
    zero = jnp.zeros((HALO, d_xbc), F32)
    ext = jnp.concatenate([jnp.where(i > 0, hwe[:HALO, c0:c1], zero), hw[:, c0:c1],
                           jnp.where(i < n_tiles - 1, hwe[HALO + tm:, c0:c1], zero)], axis=0)
    xbc_ref[...] = _conv_silu(ext, cw_ref, cb_ref)
    small = hw[:, c3:c3 + LANES]
    kpe_sw = hw[:, c3 + LANES:c3 + 2 * LANES]
    dt = _softplus(small + dtb_ref[...])
    dt_ref[...] = dt
    dtt_ref[...] = dt.T[:dtt_ref.shape[0], :]

    n_k = MLA_N_HEADS * LANES
    qa = _rms(hw[:, c1:c2], qan_ref[...]).astype(BF16)
    qq = _dot(qa, wq_ref[...])
    kva = _rms(hw[:, c2:c3], kvan_ref[...]).astype(BF16)
    kv = _dot(kva, wkvb_ref[...])
    v_t = kv[:, n_k:].T
    ones_rows = jnp.ones((MLA_VT_ROWS - MLA_V, v_t.shape[1]), F32)
    vt_ref[...] = jnp.concatenate(
        [blk for hd in range(MLA_N_HEADS) for blk in (v_t[hd * MLA_V:(hd + 1) * MLA_V, :], ones_rows)],
        axis=0).astype(BF16)

    lane = lax.broadcasted_iota(jnp.int32, (1, LANES), 1)
    pe_lanes = (lane >= MLA_NOPE) & (lane < MLA_NOPE + MLA_ROPE)
    kpe = jnp.where(pe_lanes, small, 0.0)
    cos_t, sin_t = cos_ref[...], sin_ref[...]
    q_cos, q_sin = cos_t * qgc_ref[...], sin_t * qgs_ref[...]
    k_cos, k_sin = cos_t * kgc_ref[...], sin_t * kgs_ref[...]
    k_rot = kpe_sw * k_sin
    inv_dim = 1.0 / (MLA_NOPE + MLA_ROPE)

    q_blk = [qq[:, hd * LANES:(hd + 1) * LANES] for hd in range(MLA_N_HEADS)]
    k_blk = [kv[:, hd * LANES:(hd + 1) * LANES] + kpe for hd in range(MLA_N_HEADS)]
    q_ss, k_ss = [], []
    for pr in range(MLA_N_HEADS // 2):
        for blk, out in ((q_blk, q_ss), (k_blk, k_ss)):
            sq = jnp.concatenate([blk[2 * pr] * blk[2 * pr], blk[2 * pr + 1] * blk[2 * pr + 1]], axis=1)
            ss = _dot(sq.astype(BF16), ones_ref[...])
            out.extend([ss[:, :LANES], ss[:, LANES:]])
    for hd in range(MLA_N_HEADS):
        sl = slice(hd * LANES, (hd + 1) * LANES)
        q_rot = qq[:, n_k + hd * LANES:n_k + (hd + 1) * LANES]
        rq = lax.rsqrt(q_ss[hd] * inv_dim + EPS)
        q_ref[:, sl] = ((q_blk[hd] * q_cos + q_rot * q_sin) * rq).astype(BF16)
        rk = lax.rsqrt(k_ss[hd] * inv_dim + EPS)
        k_ref[:, sl] = ((k_blk[hd] * k_cos + k_rot) * rk).astype(BF16)


def _even_in(x, g, p, cos_t, sin_t):
    b, s, d = x.shape
    tm = TOKEN_TILE
    d_inner, d_xbc = p["d_inner"], p["d_xbc"]
    n_q = MLA_N_HEADS * LANES
    n_v = MLA_N_HEADS * MLA_VT_ROWS
    n_dt = 2 * (d_inner // SSD_HEAD_DIM)
    tok = lambda c: pl.BlockSpec((None, tm, c), lambda bi, i: (bi, i, 0))
    tab = pl.BlockSpec((tm, LANES), lambda bi, i: (i, 0))
    consts = [g, p["w_in"], p["dt_bias"], p["q_a_norm"], p["wq"], p["kv_a_norm"], p["wkvb"],
              p["q_gain_cos"], p["q_gain_sin"], p["k_gain_cos"], p["k_gain_sin"], p["pair_ones"],
              p["conv_w"], p["conv_b"]]
    n_tiles = s // tm
    per = tm // HALO
    n_halo = s // HALO
    halo_prev = pl.BlockSpec((None, HALO, d), lambda bi, i: (bi, jnp.maximum(i * per - 1, 0), 0))
    halo_next = pl.BlockSpec((None, HALO, d), lambda bi, i: (bi, jnp.minimum((i + 1) * per, n_halo - 1), 0))
    return pl.pallas_call(
        functools.partial(_even_in_kernel, d_inner=d_inner, d_xbc=d_xbc, n_tiles=n_tiles),
        out_shape=[
            jax.ShapeDtypeStruct((b, s, d_inner), BF16),
            jax.ShapeDtypeStruct((b, s, d_xbc), F32),
            jax.ShapeDtypeStruct((b, s, LANES), F32),
            jax.ShapeDtypeStruct((b, n_dt, s), F32),
            jax.ShapeDtypeStruct((b, s, n_q), BF16),
            jax.ShapeDtypeStruct((b, s, n_q), BF16),
            jax.ShapeDtypeStruct((b, n_v, s), BF16),
        ],
        grid=(b, s // tm),
        in_specs=[tok(d), halo_prev, halo_next] + [_const_spec(c.shape) for c in consts] + [tab, tab],
        out_specs=[
            tok(d_inner), tok(d_xbc), tok(LANES),
            pl.BlockSpec((None, n_dt, tm), lambda bi, i: (bi, 0, i)),
            tok(n_q), tok(n_q),
            pl.BlockSpec((None, n_v, tm), lambda bi, i: (bi, 0, i)),
        ],
        compiler_params=_params(("parallel", "parallel")),
        name="even_in",
    )(x, x, x, *consts, cos_t, sin_t)


def _ssd_stats(dt, dtt, a_row, a_col, expand, *, reverse):
    L = dt.shape[0]
    keep, keep_t = _scan_masks(L, reverse)
    last = 0 if reverse else L - 1
    keep_bf = keep.astype(F32).astype(BF16)
    keep_t_bf = keep_t.astype(F32).astype(BF16)
    cum = sum(_dot(keep_bf, t) for t in _split_terms(dt * a_row, 3))
    cumt = sum(_dot(t, keep_t_bf) for t in _split_terms(dtt * a_col, 3))
    cum_last = cum[last:last + 1, :]
    narrow = jnp.concatenate(
        [jnp.exp(cum), dt * jnp.exp(cum_last - cum),
         jnp.broadcast_to(jnp.exp(cum_last), (SUBLANES, LANES))], axis=0)
    wide = _dot(narrow.astype(BF16), expand)
    return dict(cum=cum, cumt=cumt, keep=keep, e_off=wide[:L], w_state=wide[L:2 * L],
                chunk_decay=wide[2 * L:2 * L + 1])


def _ssd_main(xbc, dtt, stats, s_ref, d_skip, *, lane_off, d_inner):
    n_heads = d_inner // SSD_HEAD_DIM
    hpg = n_heads // SSD_N_GROUPS
    gw = hpg * SSD_HEAD_DIM
    gn = SSD_D_STATE
    xs = xbc[:, :d_inner]
    cum, cumt, keep = stats["cum"], stats["cumt"], stats["keep"]
    e_off, w_state, chunk_decay = stats["e_off"], stats["w_state"], stats["chunk_decay"]
    lane = lax.broadcasted_iota(jnp.int32, (1, LANES), 1)
    lo_lanes = lane < SSD_HEAD_DIM
    parts = []
    for g in range(SSD_N_GROUPS):
        gsl = slice(g * gw, (g + 1) * gw)
        b_g = xbc[:, d_inner + g * gn:d_inner + (g + 1) * gn]
        c_off = d_inner + SSD_N_GROUPS * gn
        c_bf = xbc[:, c_off + g * gn:c_off + (g + 1) * gn].astype(BF16)
        cb = _dot_nt(c_bf, b_g.astype(BF16))
        state = s_ref[g]
        y_off = _dot(c_bf, state.astype(BF16)) * e_off[:, gsl]
        xw = (xs[:, gsl] * w_state[:, gsl]).astype(BF16)
        s_ref[g] = state * chunk_decay[:, gsl] + _dot(b_g.T.astype(BF16), xw)
        for pr in range(hpg // 2):
            m_pair = []
            for j in range(2):
                hd = g * hpg + 2 * pr + j
                seg = cum[:, lane_off + hd:lane_off + hd + 1] - cumt[hd:hd + 1, :]
                m_h = jnp.where(keep, cb * jnp.exp(seg) * dtt[hd:hd + 1, :], 0.0)
                m_pair.append(m_h.astype(BF16))
            psl = slice(g * gw + pr * LANES, g * gw + (pr + 1) * LANES)
            x_pair = xs[:, psl]
            rhs = jnp.concatenate([jnp.where(lo_lanes, x_pair, 0.0),
                                   jnp.where(lo_lanes, 0.0, x_pair)], axis=0).astype(BF16)
            y_pair = _dot(jnp.concatenate(m_pair, axis=1), rhs) + y_off[:, pr * LANES:(pr + 1) * LANES]
            if d_skip is not None:
                y_pair = y_pair + x_pair * d_skip[:, psl]
            parts.append(y_pair)
    return jnp.concatenate(parts, axis=1).astype(BF16)


def _ssd_kernel(xf_ref, xb_ref, dtf_ref, dtb_ref, dttf_ref, dttb_ref, alog_row_ref, alog_col_ref,
                dskip_ref, ef_ref, eb_ref, yf_ref, yb_ref, sf_ref, sb_ref, *, d_inner):
    @pl.when(pl.program_id(1) == 0)
    def _():
        sf_ref[...] = jnp.zeros(sf_ref.shape, F32)
        sb_ref[...] = jnp.zeros(sb_ref.shape, F32)

    n_heads = d_inner // SSD_HEAD_DIM
    lane = lax.broadcasted_iota(jnp.int32, (1, LANES), 1)
    a_all = -jnp.exp(alog_row_ref[...])
    a_col = -jnp.exp(alog_col_ref[...])
    a_f = jnp.where(lane < n_heads, a_all, 0.0)
    a_b = jnp.where((lane >= n_heads) & (lane < 2 * n_heads), a_all, 0.0)
    chunks = [(slice(c * SCAN_CHUNK, (c + 1) * SCAN_CHUNK),
               slice((SCAN_STEP_CHUNKS - 1 - c) * SCAN_CHUNK, (SCAN_STEP_CHUNKS - c) * SCAN_CHUNK))
              for c in range(SCAN_STEP_CHUNKS)]
    stats = [(_ssd_stats(dtf_ref[fs, :], dttf_ref[0:n_heads, fs], a_f, a_col[0:n_heads, :], ef_ref[...],
                         reverse=False),
              _ssd_stats(dtb_ref[bs, :], dttb_ref[n_heads:2 * n_heads, bs], a_b, a_col[n_heads:2 * n_heads, :],
                         eb_ref[...], reverse=True)) for fs, bs in chunks]
    for (fs, bs), (stats_f, stats_b) in zip(chunks, stats):
        yf_ref[fs, :] = _ssd_main(xf_ref[fs, :], dttf_ref[0:n_heads, fs], stats_f, sf_ref, dskip_ref[...],
                                  lane_off=0, d_inner=d_inner)
        yb_ref[bs, :] = _ssd_main(xb_ref[bs, :], dttb_ref[n_heads:2 * n_heads, bs], stats_b, sb_ref, None,
                                  lane_off=n_heads, d_inner=d_inner)


def _ssd(xbc, dt, dtt, alog_row, alog_col, d_skip, e_f, e_b, *, d_inner):
    b, s, c = xbc.shape
    L = SCAN_CHUNK * SCAN_STEP_CHUNKS
    nc = s // L
    n_dt = dtt.shape[1]
    hpg_w = d_inner // SSD_N_GROUPS
    fwd = lambda w: pl.BlockSpec((None, L, w), lambda bi, i: (bi, i, 0))
    bwd = lambda w: pl.BlockSpec((None, L, w), lambda bi, i: (bi, nc - 1 - i, 0))
    return pl.pallas_call(
        functools.partial(_ssd_kernel, d_inner=d_inner),
        out_shape=[jax.ShapeDtypeStruct((b, s, d_inner), BF16)] * 2,
        grid=(b, nc),
        in_specs=[
            fwd(c), bwd(c), fwd(LANES), bwd(LANES),
            pl.BlockSpec((None, n_dt, L), lambda bi, i: (bi, 0, i)),
            pl.BlockSpec((None, n_dt, L), lambda bi, i: (bi, 0, nc - 1 - i)),
            _const_spec(alog_row.shape), _const_spec(alog_col.shape), _const_spec(d_skip.shape),
            _const_spec(e_f.shape), _const_spec(e_b.shape),
        ],
        out_specs=[fwd(d_inner), bwd(d_inner)],
        scratch_shapes=[pltpu.VMEM((SSD_N_GROUPS, SSD_D_STATE, hpg_w), F32)] * 2,
        compiler_params=_params(("parallel", "arbitrary")),
        name="ssd_scan",
    )(xbc, xbc, dt, dt, dtt, dtt, alog_row, alog_col, d_skip, e_f, e_b)


def _attn_kernel(bounded_ref, q_ref, k_ref, vt_ref, o_ref):
    def scores(hd):
        sl = slice(hd * LANES, (hd + 1) * LANES)
        return _dot_nt(k_ref[:, sl], q_ref[:, sl])

    def attend(subtract_max):
        outs = []
        st_next = scores(0)
        for hd in range(MLA_N_HEADS):
            st = st_next
            if hd + 1 < MLA_N_HEADS:
                st_next = scores(hd + 1)
            if subtract_max:
                st = st - jnp.max(st, axis=0, keepdims=True)
            p = jnp.exp2(st).astype(BF16)
            acc = _dot(vt_ref[hd * MLA_VT_ROWS:(hd + 1) * MLA_VT_ROWS, :], p)
            outs.append(acc[:MLA_V] / acc[MLA_V:MLA_V + 1])
        o_ref[...] = jnp.concatenate(outs, axis=0).T.astype(BF16)

    @pl.when(bounded_ref[0, 0] != 0)
    def _():
        attend(subtract_max=False)

    @pl.when(bounded_ref[0, 0] == 0)
    def _():
        attend(subtract_max=True)


def _attention(bounded, q, k, vt):
    b, s, n_q = q.shape
    n_vt = vt.shape[1]
    n_v = MLA_N_HEADS * MLA_V
    tq = ATTN_Q_TILE
    return pl.pallas_call(
        _attn_kernel,
        out_shape=jax.ShapeDtypeStruct((b, s, n_v), BF16),
        grid=(b, s // tq),
        in_specs=[
            pl.BlockSpec(memory_space=pltpu.SMEM),
            pl.BlockSpec((None, tq, n_q), lambda bi, i: (bi, i, 0)),
            pl.BlockSpec((None, s, n_q), lambda bi, i: (bi, 0, 0), pipeline_mode=pl.Buffered(1)),
            pl.BlockSpec((None, n_vt, s), lambda bi, i: (bi, 0, 0), pipeline_mode=pl.Buffered(1)),
        ],
        out_specs=pl.BlockSpec((None, tq, n_v), lambda bi, i: (bi, i, 0)),
        compiler_params=_params(("parallel", "arbitrary")),
        name="mla_attention",
    )(bounded, q, k, vt)


def _even_out_kernel(x_ref, yf_ref, yb_ref, z_ref, o_ref, g_ref, w_ref, fg_ref, fwin_ref, fwout_ref,
                     out_ref, *, d_inner):
    z = z_ref[...].astype(F32)
    y = (yf_ref[...].astype(F32) + yb_ref[...].astype(F32)) * (z * _sigmoid(z))
    gw = d_inner // SSD_N_GROUPS
    normed = []
    for g in range(SSD_N_GROUPS):
        seg = y[:, g * gw:(g + 1) * gw]
        normed.append(seg * lax.rsqrt(jnp.mean(seg * seg, axis=-1, keepdims=True) + EPS))
    yn = (jnp.concatenate(normed, axis=1) * g_ref[...]).astype(BF16)
    x = x_ref[...] + _dot(yn, w_ref[0:d_inner, :]) + _dot(o_ref[...], w_ref[d_inner:, :])
    out_ref[...] = _ffn_apply(x, fg_ref, fwin_ref, fwout_ref)


def _even_out(x, yf, yb, z, o, g, w, ffn, layer):
    t, d = x.shape
    d_inner = yf.shape[1]
    tm = TOKEN_TILE
    tok = lambda c: pl.BlockSpec((tm, c), lambda i: (i, 0))
    return pl.pallas_call(
        functools.partial(_even_out_kernel, d_inner=d_inner),
        out_shape=jax.ShapeDtypeStruct((t, d), F32),
        grid=(t // tm,),
        in_specs=[tok(d), tok(d_inner), tok(d_inner), tok(d_inner), tok(o.shape[1]),
                  _const_spec(g.shape), _const_spec(w.shape)] + _layer_specs(ffn, layer),
        out_specs=tok(d),
        compiler_params=_params(("parallel",)),
        name="even_out_ffn",
    )(x, yf, yb, z, o, g, w, *ffn)


def _odd_in_kernel(x_ref, g_ref, w_ref, gb_ref, qt_ref, k_ref, kt_ref, v_ref, o_ref, gates_ref, gatest_ref):
    h = _rms(x_ref[...], g_ref[...]).astype(BF16)
    hw = _dot(h, w_ref[...])
    n_qk = ML_N_HEADS * ML_QK
    n_v = ML_N_HEADS * ML_V
    qt_ref[...] = hw[:, :n_qk].T.astype(BF16)
    k = hw[:, n_qk:2 * n_qk]
    k_ref[...] = k.astype(BF16)
    kt_ref[...] = k.T.astype(BF16)
    v_ref[...] = hw[:, 2 * n_qk:2 * n_qk + n_v].astype(BF16)
    o_ref[...] = hw[:, 2 * n_qk + n_v:2 * n_qk + 2 * n_v].astype(BF16)
    pre = hw[:, 2 * n_qk + 2 * n_v:] + gb_ref[...]
    lane = lax.broadcasted_iota(jnp.int32, (1, LANES), 1)
    gates = jnp.where(lane < 2 * ML_N_HEADS, pre, -_softplus(-pre))
    gates_ref[...] = gates
    gatest_ref[...] = gates.T[:gatest_ref.shape[0], :]


def _odd_in(x, g, w, gate_bias):
    b, s, d = x.shape
    tm = TOKEN_TILE
    n_qk = ML_N_HEADS * ML_QK
    n_v = ML_N_HEADS * ML_V
    n_g = 4 * ML_N_HEADS
    tok = lambda c: pl.BlockSpec((None, tm, c), lambda bi, i: (bi, i, 0))
    tok_t = lambda c: pl.BlockSpec((None, c, tm), lambda bi, i: (bi, 0, i))
    return pl.pallas_call(
        _odd_in_kernel,
        out_shape=[
            jax.ShapeDtypeStruct((b, n_qk, s), BF16),
            jax.ShapeDtypeStruct((b, s, n_qk), BF16),
            jax.ShapeDtypeStruct((b, n_qk, s), BF16),
            jax.ShapeDtypeStruct((b, s, n_v), BF16),
            jax.ShapeDtypeStruct((b, s, n_v), BF16),
            jax.ShapeDtypeStruct((b, s, LANES), F32),
            jax.ShapeDtypeStruct((b, n_g, s), F32),
        ],
        grid=(b, s // tm),
        in_specs=[tok(d), _const_spec(g.shape), _const_spec(w.shape), _const_spec(gate_bias.shape)],
        out_specs=[tok_t(n_qk), tok(n_qk), tok_t(n_qk), tok(n_v), tok(n_v), tok(LANES), tok_t(n_g)],
        compiler_params=_params(("parallel", "parallel")),
        name="odd_in",
    )(x, g, w, gate_bias)


def _cummax_lanes(u, reverse):
    n = u.shape[1]
    lane = lax.broadcasted_iota(jnp.int32, (1, n), 1)
    d = 1
    while d < n:
        if reverse:
            shifted, valid = pltpu.roll(u, n - d, axis=1), lane < n - d
        else:
            shifted, valid = pltpu.roll(u, d, axis=1), lane >= d
        u = jnp.where(valid, jnp.maximum(u, shifted), u)
        d *= 2
    return u


def _mlstm_stats(gates, gatest, m_ref, *, i_off, f_off, reverse):
    L = gates.shape[0]
    H = ML_N_HEADS
    assert L == LANES == ML_V and 2 * ML_QK == LANES
    keep, keep_t = _scan_masks(L, reverse)
    keep_bf = keep.astype(F32).astype(BF16)
    keep_t_bf = keep_t.astype(F32).astype(BF16)
    last = 0 if reverse else L - 1

    li_r = gatest[i_off:i_off + H, :]
    bcum_r = sum(_dot(t, keep_t_bf) for t in _split_terms(gatest[f_off:f_off + H, :], 3))
    u_r = li_r - bcum_r
    m_in = m_ref[...]
    mx = jnp.maximum(m_in, _cummax_lanes(u_r, reverse))
    w_inter = jnp.exp(m_in - mx)
    e_negm = jnp.exp(-(bcum_r + mx))
    b_end = jnp.broadcast_to(bcum_r[:, last:last + 1], (H, L))
    g_end = b_end + u_r
    m_new = jnp.maximum(b_end + m_in, jnp.max(g_end, axis=1, keepdims=True))
    w_src = jnp.exp(g_end - m_new)
    decay = jnp.exp(b_end + m_in - m_new)
    m_ref[...] = m_new

    bcum_c = sum(_dot(keep_bf, t) for t in _split_terms(gates, 3))
    u_all = jnp.concatenate(
        [jnp.broadcast_to(gates[:, i_off + hd:i_off + hd + 1] - bcum_c[:, f_off + hd:f_off + hd + 1], (L, LANES))
         for hd in range(H)], axis=1)
    return dict(mx=mx, w_inter=w_inter, e_negm=e_negm, w_src=w_src, decay=decay, u_all=u_all, keep_t=keep_t)


def _mlstm_main(k, kt, qt, v, stats, c_ref):
    L = k.shape[0]
    H = ML_N_HEADS
    mx, w_inter, e_negm, w_src = stats["mx"], stats["w_inter"], stats["e_negm"], stats["w_src"]
    decay, u_all, keep_t = stats["decay"], stats["u_all"], stats["keep_t"]
    srow = lax.broadcasted_iota(jnp.int32, (LANES, 1), 0)
    first = srow < ML_QK
    ones = jnp.ones((L, ML_V), BF16)
    c_in = [c_ref[pr] for pr in range(H // 2)]
    qt_own, kt_own, st = [], [], []
    for pr in range(H // 2):
        psl = slice(pr * LANES, (pr + 1) * LANES)
        zero = jnp.zeros((LANES, L), BF16)
        qt_own += [jnp.where(first, qt[psl, :], zero), jnp.where(first, zero, qt[psl, :])]
        kt_own += [jnp.where(first, kt[psl, :], zero), jnp.where(first, zero, kt[psl, :])]
        st.append(_dot(k[:, psl], jnp.concatenate(qt_own[-2:], axis=1)))
    lhs_t = []
    for hd in range(H):
        row = lambda a: a[hd:hd + 1, :]
        c_aug = c_in[hd // 2]
        w = jnp.where(keep_t, jnp.exp(u_all[:, hd * LANES:(hd + 1) * LANES] - row(mx)), 0.0)
        sw = st[hd // 2][:, (hd % 2) * L:(hd % 2 + 1) * L] * w
        q_f32 = qt_own[hd].astype(F32)
        nq = jnp.sum(c_aug[:, ML_V:] * q_f32, axis=0, keepdims=True)
        den = jnp.sum(sw, axis=0, keepdims=True) + row(w_inter) * nq
        inv = 1.0 / jnp.maximum(jnp.abs(den), row(e_negm))
        lhs_t.append(jnp.concatenate(
            [(sw * inv).astype(BF16), (q_f32 * (row(w_inter) * inv)).astype(BF16)], axis=0))
    outs = []
    for hd in range(H):
        v_h = v[:, hd * ML_V:(hd + 1) * ML_V]
        rhs = jnp.concatenate([v_h, c_in[hd // 2][:, :ML_V].astype(BF16)], axis=0)
        outs.append(_dot_tn(lhs_t[hd], rhs))
    for pr in range(H // 2):
        update = jnp.zeros(c_in[pr].shape, F32)
        for hd in (2 * pr, 2 * pr + 1):
            wkt = (kt_own[hd].astype(F32) * w_src[hd:hd + 1, :]).astype(BF16)
            v_aug = jnp.concatenate([v[:, hd * ML_V:(hd + 1) * ML_V], ones], axis=1)
            update = update + _dot(wkt, v_aug)
        dec = jnp.where(first, decay[2 * pr:2 * pr + 1, :], decay[2 * pr + 1:2 * pr + 2, :])
        c_ref[pr] = c_in[pr] * jnp.concatenate([dec, dec], axis=1) + update
    return jnp.concatenate(outs, axis=1).astype(BF16)


def _mlstm_kernel(kf_ref, ktf_ref, qtf_ref, vf_ref, gf_ref, gtf_ref,
                  kb_ref, ktb_ref, qtb_ref, vb_ref, gb_ref, gtb_ref,
                  hf_ref, hb_ref, cf_ref, cb_ref, mf_ref, mb_ref):
    @pl.when(pl.program_id(1) == 0)
    def _():
        for ref in (cf_ref, cb_ref, mf_ref, mb_ref):
            ref[...] = jnp.zeros(ref.shape, F32)

    h = ML_N_HEADS
    chunks = [(slice(c * SCAN_CHUNK, (c + 1) * SCAN_CHUNK),
               slice((SCAN_STEP_CHUNKS - 1 - c) * SCAN_CHUNK, (SCAN_STEP_CHUNKS - c) * SCAN_CHUNK))
              for c in range(SCAN_STEP_CHUNKS)]
    stats = [(_mlstm_stats(gf_ref[fs, :], gtf_ref[:, fs], mf_ref, i_off=0, f_off=2 * h, reverse=False),
              _mlstm_stats(gb_ref[bs, :], gtb_ref[:, bs], mb_ref, i_off=h, f_off=3 * h, reverse=True))
             for fs, bs in chunks]
    for (fs, bs), (stats_f, stats_b) in zip(chunks, stats):
        hf_ref[fs, :] = _mlstm_main(kf_ref[fs, :], ktf_ref[:, fs], qtf_ref[:, fs], vf_ref[fs, :], stats_f, cf_ref)
        hb_ref[bs, :] = _mlstm_main(kb_ref[bs, :], ktb_ref[:, bs], qtb_ref[:, bs], vb_ref[bs, :], stats_b, cb_ref)


def _mlstm(qt, k, kt, v, gates, gatest):
    b, s, n_qk = k.shape
    n_v = v.shape[2]
    n_g = gatest.shape[1]
    L = SCAN_CHUNK * SCAN_STEP_CHUNKS
    nc = s // L
    fwd = lambda w: pl.BlockSpec((None, L, w), lambda bi, i: (bi, i, 0))
    bwd = lambda w: pl.BlockSpec((None, L, w), lambda bi, i: (bi, nc - 1 - i, 0))
    fwd_t = lambda c: pl.BlockSpec((None, c, L), lambda bi, i: (bi, 0, i))
    bwd_t = lambda c: pl.BlockSpec((None, c, L), lambda bi, i: (bi, 0, nc - 1 - i))
    state = pltpu.VMEM((ML_N_HEADS // 2, 2 * ML_QK, 2 * ML_V), F32)
    stab = pltpu.VMEM((ML_N_HEADS, LANES), F32)
    return pl.pallas_call(
        _mlstm_kernel,
        out_shape=[jax.ShapeDtypeStruct((b, s, n_v), BF16)] * 2,
        grid=(b, nc),
        in_specs=[fwd(n_qk), fwd_t(n_qk), fwd_t(n_qk), fwd(n_v), fwd(LANES), fwd_t(n_g),
                  bwd(n_qk), bwd_t(n_qk), bwd_t(n_qk), bwd(n_v), bwd(LANES), bwd_t(n_g)],
        out_specs=[fwd(n_v), bwd(n_v)],
        scratch_shapes=[state, state, stab, stab],
        compiler_params=_params(("parallel", "arbitrary")),
        name="mlstm_scan",
    )(k, kt, qt, v, gates, gatest, k, kt, qt, v, gates, gatest)


def _odd_out_kernel(x_ref, hf_ref, hb_ref, o_ref, g_ref, w_ref, fg_ref, fwin_ref, fwout_ref, out_ref):
    hs = hf_ref[...].astype(F32) + hb_ref[...].astype(F32)
    normed = []
    for hd in range(ML_N_HEADS):
        seg = hs[:, hd * ML_V:(hd + 1) * ML_V]
        normed.append(seg * lax.rsqrt(jnp.mean(seg * seg, axis=-1, keepdims=True) + EPS))
    gated = (_sigmoid(o_ref[...].astype(F32)) * (jnp.concatenate(normed, axis=1) * g_ref[...])).astype(BF16)
    x = x_ref[...] + _dot(gated, w_ref[...])
    out_ref[...] = _ffn_apply(x, fg_ref, fwin_ref, fwout_ref)


def _odd_out(x, hf, hb, o, g, w, ffn, layer):
    t, d = x.shape
    n_v = hf.shape[1]
    tm = TOKEN_TILE
    tok = lambda c: pl.BlockSpec((tm, c), lambda i: (i, 0))
    return pl.pallas_call(
        _odd_out_kernel,
        out_shape=jax.ShapeDtypeStruct((t, d), F32),
        grid=(t // tm,),
        in_specs=[tok(d), tok(n_v), tok(n_v), tok(n_v), _const_spec(g.shape), _const_spec(w.shape)]
        + _layer_specs(ffn, layer),
        out_specs=tok(d),
        compiler_params=_params(("parallel",)),
        name="odd_out_ffn",
    )(x, hf, hb, o, g, w, *ffn)


def _row(v):
    return v.reshape(1, -1).astype(F32)


def _pad_lanes(v, width=LANES):
    return jnp.pad(v, [(0, 0)] * (v.ndim - 1) + [(0, width - v.shape[-1])])


def _swap_halves(v):
    half = v.shape[-1] // 2
    return jnp.concatenate([v[..., half:], v[..., :half]], axis=-1)


def _on_rope_lanes(v):
    return jnp.pad(v, [(0, 0)] * (v.ndim - 1) + [(MLA_NOPE, LANES - MLA_NOPE - v.shape[-1])])


def _prep_even(ev_w_in, ev_w_out, conv_w, conv_b, dt_bias, a_log, d_skip, ssd_norm,
               q_a_norm, w_q_b, kv_a_norm, w_kv_b, q_norm, k_norm):
    d_inner = ssd_norm.shape[0]
    d_xbc = conv_b.shape[0]
    n_heads = a_log.shape[1]
    c0, c1 = d_inner, d_inner + d_xbc
    c2 = c1 + 2 * n_heads
    c3 = c2 + MLA_Q_RANK
    c4 = c3 + MLA_KV_RANK
    w_kpe = ev_w_in[:, c4:]
    small = jnp.concatenate([_pad_lanes(ev_w_in[:, c1:c2], MLA_NOPE), _pad_lanes(w_kpe, LANES - MLA_NOPE)], axis=1)
    w_in = jnp.concatenate([ev_w_in[:, :c1], ev_w_in[:, c2:c4], small,
                            _on_rope_lanes(_swap_halves(w_kpe))], axis=1).astype(BF16)
    d_qk = MLA_NOPE + MLA_ROPE
    wq3 = w_q_b.reshape(MLA_Q_RANK, MLA_N_HEADS, d_qk)
    wq = jnp.concatenate(
        [_pad_lanes(wq3).reshape(MLA_Q_RANK, -1),
         _on_rope_lanes(_swap_halves(wq3[:, :, MLA_NOPE:])).reshape(MLA_Q_RANK, -1)], axis=1).astype(BF16)
    q_scale = d_qk ** -0.5 * LOG2_E
    score_bound = 1.02 * d_qk * q_scale * jnp.max(jnp.abs(q_norm)) * jnp.max(jnp.abs(k_norm))
    scores_bounded = (score_bound <= SAFE_LOG2_SCORE).astype(jnp.int32).reshape(1, 1)
    ones_blk = jnp.ones((LANES, LANES), BF16)
    zero_blk = jnp.zeros((LANES, LANES), BF16)
    pair_ones = jnp.concatenate([jnp.concatenate([ones_blk, zero_blk], axis=1),
                                 jnp.concatenate([zero_blk, ones_blk], axis=1)], axis=0)
    wkv = w_kv_b.reshape(MLA_KV_RANK, MLA_N_HEADS, MLA_NOPE + MLA_V)
    wkvb = jnp.concatenate(
        [_pad_lanes(wkv[:, :, :MLA_NOPE]).reshape(MLA_KV_RANK, -1),
         wkv[:, :, MLA_NOPE:].reshape(MLA_KV_RANK, -1)], axis=1).astype(BF16)
    head_of_lane = jnp.arange(d_inner) // SSD_HEAD_DIM
    e_f = (jnp.arange(LANES)[:, None] == head_of_lane[None, :]).astype(BF16)
    e_b = (jnp.arange(LANES)[:, None] == head_of_lane[None, :] + n_heads).astype(BF16)
    return dict(
        d_inner=d_inner, d_xbc=d_xbc, w_in=w_in, w_out=ev_w_out.astype(BF16),
        conv_w=_pad_lanes(conv_w.T, SUBLANES).T.astype(F32), conv_b=_row(conv_b),
        dt_bias=_pad_lanes(_row(dt_bias)), alog_row=_pad_lanes(_row(a_log)),
        alog_col=a_log.reshape(-1, 1).astype(F32), d_skip=_row(jnp.repeat(d_skip, SSD_HEAD_DIM)),
        ssd_norm=_row(ssd_norm), e_f=e_f, e_b=e_b,
        q_a_norm=_row(q_a_norm), wq=wq, kv_a_norm=_row(kv_a_norm), wkvb=wkvb, pair_ones=pair_ones,
        scores_bounded=scores_bounded,
        q_gain_cos=_pad_lanes(_row(q_norm)) * q_scale,
        q_gain_sin=_on_rope_lanes(_swap_halves(_row(q_norm)[:, MLA_NOPE:])) * q_scale,
        k_gain_cos=_pad_lanes(_row(k_norm)),
        k_gain_sin=_on_rope_lanes(_swap_halves(_row(k_norm)[:, MLA_NOPE:])))


def _rope_tables(s):
    pos = jnp.arange(s, dtype=F32)
    inv_freq = jnp.power(ROPE_BASE, -jnp.arange(0, MLA_ROPE, 2, dtype=F32) / MLA_ROPE)
    freqs = pos[:, None] * inv_freq[None, :]
    cos, sin = jnp.cos(freqs), jnp.sin(freqs)
    tail = jnp.ones((s, LANES - MLA_NOPE - MLA_ROPE), F32)
    cos_t = jnp.concatenate([jnp.ones((s, MLA_NOPE), F32), cos, cos, tail], axis=1)
    sin_t = _on_rope_lanes(jnp.concatenate([-sin, sin], axis=1))
    return cos_t, sin_t


def _even_mixer(x, norm_g, p, ffn2, layer):
    b, s, d = x.shape
    z, xbc, dt, dtt, q, k, vt = _even_in(x, norm_g, p, *_rope_tables(s))
    yf, yb = _ssd(xbc, dt, dtt, p["alog_row"], p["alog_col"], p["d_skip"], p["e_f"], p["e_b"],
                  d_inner=p["d_inner"])
    o = _attention(p["scores_bounded"], q, k, vt)
    flat = lambda a: a.reshape(b * s, a.shape[-1])
    return _even_out(flat(x), flat(yf), flat(yb), flat(z), flat(o), p["ssd_norm"],
                     p["w_out"], ffn2, layer).reshape(b, s, d)


def _prep_odd(od_w_in, od_w_out, ig_bias, fg_bias, ml_norm):
    n_qk = ML_N_HEADS * ML_QK
    n_main = 2 * n_qk + 2 * ML_N_HEADS * ML_V
    w_in = jnp.concatenate([od_w_in[:, :n_qk] * ML_QK ** -0.5, od_w_in[:, n_qk:n_main],
                            _pad_lanes(od_w_in[:, n_main:])], axis=1).astype(BF16)
    gate_bias = _pad_lanes(_row(jnp.concatenate([ig_bias.reshape(-1), fg_bias.reshape(-1)])))
    return dict(w_in=w_in, w_out=od_w_out.astype(BF16), gate_bias=gate_bias, ml_norm=_row(ml_norm))


def _odd_mixer(x, norm_g, p, ffn2, layer):
    b, s, d = x.shape
    qt, k, kt, v, o, gates, gatest = _odd_in(x, norm_g, p["w_in"], p["gate_bias"])
    hf, hb = _mlstm(qt, k, kt, v, gates, gatest)
    flat = lambda a: a.reshape(b * s, a.shape[-1])
    return _odd_out(flat(x), flat(hf), flat(hb), flat(o), p["ml_norm"], p["w_out"], ffn2, layer).reshape(b, s, d)


def kernel(x_prompt, x_sample, ffn1_norm, ffn1_w_in, ffn1_w_out, mix_norm, ffn2_norm, ffn2_w_in, ffn2_w_out,
           ev_w_in, ev_w_out, ssd_conv_w, ssd_conv_b, ssd_dt_bias, ssd_a_log, ssd_d_skip, ssd_norm,
           mla_q_a_norm, mla_w_q_b, mla_kv_a_norm, mla_w_kv_b, mla_q_norm, mla_k_norm,
           od_w_in, od_w_out, ml_ig_bias, ml_fg_bias, ml_norm):
    depth = ffn1_norm.shape[0]
    layers = []
    for layer in range(depth):
        j = layer // 2
        if layer % 2 == 0:
            mixer = functools.partial(_even_mixer, p=_prep_even(
                ev_w_in[j], ev_w_out[j], ssd_conv_w[j], ssd_conv_b[j], ssd_dt_bias[j], ssd_a_log[j],
                ssd_d_skip[j], ssd_norm[j], mla_q_a_norm[j], mla_w_q_b[j], mla_kv_a_norm[j],
                mla_w_kv_b[j], mla_q_norm[j], mla_k_norm[j]))
        else:
            mixer = functools.partial(_odd_mixer, p=_prep_odd(
                od_w_in[j], od_w_out[j], ml_ig_bias[j], ml_fg_bias[j], ml_norm[j]))
        layers.append(dict(mixer=mixer, mix_norm=_row(mix_norm[layer])))
    ffn1 = (ffn1_norm[:, None, :].astype(F32), ffn1_w_in.astype(BF16), ffn1_w_out.astype(BF16))
    ffn2 = (ffn2_norm[:, None, :].astype(F32), ffn2_w_in.astype(BF16), ffn2_w_out.astype(BF16))

    def trunk(x):
        b, s, d = x.shape
        for layer, lp in enumerate(layers):
            x = _ffn(x.reshape(b * s, d), ffn1, layer).reshape(b, s, d)
            x = lp["mixer"](x, lp["mix_norm"], ffn2=ffn2, layer=layer)
        return x

    return trunk(x_prompt), trunk(x_sample)
```

```python
import functools

import jax
import jax.numpy as jnp
from jax import lax
from jax.experimental import pallas as pl
from jax.experimental.pallas import tpu as pltpu

F32 = jnp.float32
BF16 = jnp.bfloat16
EPS = 1e-6

LANES = 128
SUBLANES = 8
BF16_SUBLANES = 16
VMEM_LIMIT = 56 * 1024 * 1024

SSD_HEAD_DIM = 64
SSD_N_GROUPS = 2
SSD_D_STATE = 128
SSD_CONV = 5
MLA_N_HEADS = 8
MLA_Q_RANK = 384
MLA_KV_RANK = 256
MLA_NOPE = 64
MLA_ROPE = 32
MLA_V = 64
MLA_VT_ROWS = MLA_V + BF16_SUBLANES
ROPE_BASE = 10000.0
LOG2_E = 1.4426950408889634
SAFE_LOG2_SCORE = 60.0
SCORE_ROUNDING_MARGIN = 1.02
ML_N_HEADS = 8
ML_QK = 64
ML_V = 128

TOKEN_TILE = 512
FFN_TOKEN_TILE = 1024
SCAN_CHUNK = 128
SCAN_STEP_CHUNKS = 4
ATTN_Q_TILE = 512
HALO = SUBLANES


def _params(semantics):
    return pltpu.CompilerParams(dimension_semantics=semantics, vmem_limit_bytes=VMEM_LIMIT)


def _const_spec(shape):
    nd = len(shape)
    return pl.BlockSpec(shape, lambda *_: (0,) * nd, pipeline_mode=pl.Buffered(1))


def _rms(x, g):
    ms = jnp.mean(x * x, axis=-1, keepdims=True)
    return x * lax.rsqrt(ms + EPS) * g


def _sigmoid(x):
    return 1.0 / (1.0 + jnp.exp(-x))


def _softplus(x):
    return jnp.maximum(x, 0.0) + jnp.log1p(jnp.exp(-jnp.abs(x)))


def _dot(a, b):
    return jnp.dot(a, b, preferred_element_type=F32)


def _dot_nt(a, b):
    return lax.dot_general(a, b, (((1,), (1,)), ((), ())), preferred_element_type=F32)


def _dot_tn(a, b):
    return lax.dot_general(a, b, (((0,), (0,)), ((), ())), preferred_element_type=F32)


def _split_terms(x, terms):
    out = []
    for _ in range(terms):
        piece = x.astype(BF16)
        out.append(piece)
        x = x - piece.astype(F32)
    return out


def _scan_masks(n, reverse):
    row = lax.broadcasted_iota(jnp.int32, (n, n), 0)
    col = lax.broadcasted_iota(jnp.int32, (n, n), 1)
    if reverse:
        return col >= row, col <= row
    return col <= row, col >= row


def _ffn_apply(x, g_ref, win_ref, wout_ref):
    d_ff = wout_ref.shape[0]
    h = _rms(x, g_ref[...]).astype(BF16)
    hw = _dot(h, win_ref[...])
    gate = hw[:, :d_ff]
    up = hw[:, d_ff:]
    a = (gate * _sigmoid(gate) * up).astype(BF16)
    return x + 0.5 * _dot(a, wout_ref[...])


def _ffn_kernel(x_ref, g_ref, win_ref, wout_ref, o_ref):
    o_ref[...] = _ffn_apply(x_ref[...], g_ref, win_ref, wout_ref)


def _layer_specs(ffn, layer):
    return [pl.BlockSpec((None,) + a.shape[1:], lambda *_, n=a.ndim: (layer,) + (0,) * (n - 1),
                         pipeline_mode=pl.Buffered(1)) for a in ffn]


def _ffn(x, ffn, layer):
    t, d = x.shape
    tm = FFN_TOKEN_TILE
    return pl.pallas_call(
        _ffn_kernel,
        out_shape=jax.ShapeDtypeStruct((t, d), F32),
        grid=(t // tm,),
        in_specs=[pl.BlockSpec((tm, d), lambda i: (i, 0))] + _layer_specs(ffn, layer),
        out_specs=pl.BlockSpec((tm, d), lambda i: (i, 0)),
        compiler_params=_params(("parallel",)),
        name="ffn",
    )(x, *ffn)


def _conv_silu(ext, w_ref, b_ref):
    n = ext.shape[0]
    tc = n - 2 * HALO
    pad = SSD_CONV // 2
    acc = b_ref[...] + w_ref[pad:pad + 1, :] * ext[HALO:HALO + tc, :]
    for k in range(SSD_CONV):
        if k != pad:
            acc = acc + w_ref[k:k + 1, :] * pltpu.roll(ext, (pad - k) % n, axis=0)[HALO:HALO + tc, :]
    return acc * _sigmoid(acc)


def _even_in_kernel(x_ref, xp_ref, xn_ref, g_ref, w_ref, dtb_ref, qan_ref, wq_ref, kvan_ref, wkvb_ref,
                    qgc_ref, qgs_ref, kgc_ref, kgs_ref, ones_ref, cw_ref, cb_ref, cos_ref, sin_ref,
                    z_ref, xbc_ref, dt_ref, dtt_ref, q_ref, k_ref, vt_ref,
                    *, d_inner, d_xbc, n_tiles):
    i = pl.program_id(1)
    tm = x_ref.shape[0]
    xe = jnp.concatenate([xp_ref[...], x_ref[...], xn_ref[...]], axis=0)
    hwe = _dot(_rms(xe, g_ref[...]).astype(BF16), w_ref[...])
    hw = hwe[HALO:HALO + tm, :]
    c0 = d_inner
    c1 = c0 + d_xbc
    c2 = c1 + MLA_Q_RANK
    c3 = c2 + MLA_KV_RANK
    z_ref[...] = hw[:, :c0].astype(BF16)
    zero = jnp.zeros((HALO, d_xbc), F32)
    ext = jnp.concatenate([jnp.where(i > 0, hwe[:HALO, c0:c1], zero), hw[:, c0:c1],
                           jnp.where(i < n_tiles - 1, hwe[HALO + tm:, c0:c1], zero)], axis=0)
    xbc_ref[...] = _conv_silu(ext, cw_ref, cb_ref)
    small = hw[:, c3:c3 + LANES]
    kpe_sw = hw[:, c3 + LANES:c3 + 2 * LANES]
    dt = _softplus(small + dtb_ref[...])
    dt_ref[...] = dt
    dtt_ref[...] = dt.T[:dtt_ref.shape[0], :]

    n_k = MLA_N_HEADS * LANES
    qa = _rms(hw[:, c1:c2], qan_ref[...]).astype(BF16)
    qq = _dot(qa, wq_ref[...])
    kva = _rms(hw[:, c2:c3], kvan_ref[...]).astype(BF16)
    kv = _dot(kva, wkvb_ref[...])
    v_t = kv[:, n_k:].T
    ones_rows = jnp.ones((MLA_VT_ROWS - MLA_V, v_t.shape[1]), F32)
    vt_ref[...] = jnp.concatenate(
        [blk for hd in range(MLA_N_HEADS) for blk in (v_t[hd * MLA_V:(hd + 1) * MLA_V, :], ones_rows)],
        axis=0).astype(BF16)

    lane = lax.broadcasted_iota(jnp.int32, (1, LANES), 1)
    pe_lanes = (lane >= MLA_NOPE) & (lane < MLA_NOPE + MLA_ROPE)
    kpe = jnp.where(pe_lanes, small, 0.0)
    cos_t, sin_t = cos_ref[...], sin_ref[...]
    q_cos, q_sin = cos_t * qgc_ref[...], sin_t * qgs_ref[...]
    k_cos, k_sin = cos_t * kgc_ref[...], sin_t * kgs_ref[...]
    k_rot = kpe_sw * k_sin
    inv_dim = 1.0 / (MLA_NOPE + MLA_ROPE)

    q_blk = [qq[:, hd * LANES:(hd + 1) * LANES] for hd in range(MLA_N_HEADS)]
    k_blk = [kv[:, hd * LANES:(hd + 1) * LANES] + kpe for hd in range(MLA_N_HEADS)]
    q_ss, k_ss = [], []
    for pr in range(MLA_N_HEADS // 2):
        for blk, out in ((q_blk, q_ss), (k_blk, k_ss)):
            sq = jnp.concatenate([blk[2 * pr] * blk[2 * pr], blk[2 * pr + 1] * blk[2 * pr + 1]], axis=1)
            ss = _dot(sq.astype(BF16), ones_ref[...])
            out.extend([ss[:, :LANES], ss[:, LANES:]])
    for hd in range(MLA_N_HEADS):
        sl = slice(hd * LANES, (hd + 1) * LANES)
        q_rot = qq[:, n_k + hd * LANES:n_k + (hd + 1) * LANES]
        rq = lax.rsqrt(q_ss[hd] * inv_dim + EPS)
        q_ref[:, sl] = ((q_blk[hd] * q_cos + q_rot * q_sin) * rq).astype(BF16)
        rk = lax.rsqrt(k_ss[hd] * inv_dim + EPS)
        k_ref[:, sl] = ((k_blk[hd] * k_cos + k_rot) * rk).astype(BF16)


def _even_in(x, g, p, cos_t, sin_t):
    b, s, d = x.shape
    tm = TOKEN_TILE
    d_inner, d_xbc = p["d_inner"], p["d_xbc"]
    n_q = MLA_N_HEADS * LANES
    n_v = MLA_N_HEADS * MLA_VT_ROWS
    n_dt = 2 * (d_inner // SSD_HEAD_DIM)
    tok = lambda c: pl.BlockSpec((None, tm, c), lambda bi, i: (bi, i, 0))
    tab = pl.BlockSpec((tm, LANES), lambda bi, i: (i, 0))
    consts = [g, p["w_in"], p["dt_bias"], p["q_a_norm"], p["wq"], p["kv_a_norm"], p["wkvb"],
              p["q_gain_cos"], p["q_gain_sin"], p["k_gain_cos"], p["k_gain_sin"], p["pair_ones"],
              p["conv_w"], p["conv_b"]]
    n_tiles = s // tm
    per = tm // HALO
    n_halo = s // HALO
    halo_prev = pl.BlockSpec((None, HALO, d), lambda bi, i: (bi, jnp.maximum(i * per - 1, 0), 0))
    halo_next = pl.BlockSpec((None, HALO, d), lambda bi, i: (bi, jnp.minimum((i + 1) * per, n_halo - 1), 0))
    return pl.pallas_call(
        functools.partial(_even_in_kernel, d_inner=d_inner, d_xbc=d_xbc, n_tiles=n_tiles),
        out_shape=[
            jax.ShapeDtypeStruct((b, s, d_inner), BF16),
            jax.ShapeDtypeStruct((b, s, d_xbc), F32),
            jax.ShapeDtypeStruct((b, s, LANES), F32),
            jax.ShapeDtypeStruct((b, n_dt, s), F32),
            jax.ShapeDtypeStruct((b, s, n_q), BF16),
            jax.ShapeDtypeStruct((b, s, n_q), BF16),
            jax.ShapeDtypeStruct((b, n_v, s), BF16),
        ],
        grid=(b, s // tm),
        in_specs=[tok(d), halo_prev, halo_next] + [_const_spec(c.shape) for c in consts] + [tab, tab],
        out_specs=[
            tok(d_inner), tok(d_xbc), tok(LANES),
            pl.BlockSpec((None, n_dt, tm), lambda bi, i: (bi, 0, i)),
            tok(n_q), tok(n_q),
            pl.BlockSpec((None, n_v, tm), lambda bi, i: (bi, 0, i)),
        ],
        compiler_params=_params(("parallel", "parallel")),
        name="even_in",
    )(x, x, x, *consts, cos_t, sin_t)


def _ssd_stats(dt, dtt, a_row, a_col, expand, *, reverse):
    L = dt.shape[0]
    keep, keep_t = _scan_masks(L, reverse)
    last = 0 if reverse else L - 1
    keep_bf = keep.astype(F32).astype(BF16)
    keep_t_bf = keep_t.astype(F32).astype(BF16)
    cum = sum(_dot(keep_bf, t) for t in _split_terms(dt * a_row, 3))
    cumt = sum(_dot(t, keep_t_bf) for t in _split_terms(dtt * a_col, 3))
    cum_last = cum[last:last + 1, :]
    narrow = jnp.concatenate(
        [jnp.exp(cum), dt * jnp.exp(cum_last - cum),
         jnp.broadcast_to(jnp.exp(cum_last), (SUBLANES, LANES))], axis=0)
    wide = _dot(narrow.astype(BF16), expand)
    return dict(cum=cum, cumt=cumt, keep=keep, e_off=wide[:L], w_state=wide[L:2 * L],
                chunk_decay=wide[2 * L:2 * L + 1])


def _ssd_main(xbc, dtt, stats, s_ref, d_skip, *, lane_off, d_inner):
    n_heads = d_inner // SSD_HEAD_DIM
    hpg = n_heads // SSD_N_GROUPS
    gw = hpg * SSD_HEAD_DIM
    gn = SSD_D_STATE
    xs = xbc[:, :d_inner]
    cum, cumt, keep = stats["cum"], stats["cumt"], stats["keep"]
    e_off, w_state, chunk_decay = stats["e_off"], stats["w_state"], stats["chunk_decay"]
    lane = lax.broadcasted_iota(jnp.int32, (1, LANES), 1)
    lo_lanes = lane < SSD_HEAD_DIM
    parts = []
    for g in range(SSD_N_GROUPS):
        gsl = slice(g * gw, (g + 1) * gw)
        b_g = xbc[:, d_inner + g * gn:d_inner + (g + 1) * gn]
        c_off = d_inner + SSD_N_GROUPS * gn
        c_bf = xbc[:, c_off + g * gn:c_off + (g + 1) * gn].astype(BF16)
        cb = _dot_nt(c_bf, b_g.astype(BF16))
        state = s_ref[g]
        y_off = _dot(c_bf, state.astype(BF16)) * e_off[:, gsl]
        xw = (xs[:, gsl] * w_state[:, gsl]).astype(BF16)
        s_ref[g] = state * chunk_decay[:, gsl] + _dot(b_g.T.astype(BF16), xw)
        for pr in range(hpg // 2):
            m_pair = []
            for j in range(2):
                hd = g * hpg + 2 * pr + j
                seg = cum[:, lane_off + hd:lane_off + hd + 1] - cumt[hd:hd + 1, :]
                m_h = jnp.where(keep, cb * jnp.exp(seg) * dtt[hd:hd + 1, :], 0.0)
                m_pair.append(m_h.astype(BF16))
            psl = slice(g * gw + pr * LANES, g * gw + (pr + 1) * LANES)
            x_pair = xs[:, psl]
            rhs = jnp.concatenate([jnp.where(lo_lanes, x_pair, 0.0),
                                   jnp.where(lo_lanes, 0.0, x_pair)], axis=0).astype(BF16)
            y_pair = _dot(jnp.concatenate(m_pair, axis=1), rhs) + y_off[:, pr * LANES:(pr + 1) * LANES]
            if d_skip is not None:
                y_pair = y_pair + x_pair * d_skip[:, psl]
            parts.append(y_pair)
    return jnp.concatenate(parts, axis=1).astype(BF16)


def _ssd_kernel(xf_ref, xb_ref, dtf_ref, dtb_ref, dttf_ref, dttb_ref, alog_row_ref, alog_col_ref,
                dskip_ref, ef_ref, eb_ref, yf_ref, yb_ref, sf_ref, sb_ref, *, d_inner):
    @pl.when(pl.program_id(1) == 0)
    def _():
        sf_ref[...] = jnp.zeros(sf_ref.shape, F32)
        sb_ref[...] = jnp.zeros(sb_ref.shape, F32)

    n_heads = d_inner // SSD_HEAD_DIM
    lane = lax.broadcasted_iota(jnp.int32, (1, LANES), 1)
    a_all = -jnp.exp(alog_row_ref[...])
    a_col = -jnp.exp(alog_col_ref[...])
    a_f = jnp.where(lane < n_heads, a_all, 0.0)
    a_b = jnp.where((lane >= n_heads) & (lane < 2 * n_heads), a_all, 0.0)
    chunks = [(slice(c * SCAN_CHUNK, (c + 1) * SCAN_CHUNK),
               slice((SCAN_STEP_CHUNKS - 1 - c) * SCAN_CHUNK, (SCAN_STEP_CHUNKS - c) * SCAN_CHUNK))
              for c in range(SCAN_STEP_CHUNKS)]
    stats = [(_ssd_stats(dtf_ref[fs, :], dttf_ref[0:n_heads, fs], a_f, a_col[0:n_heads, :], ef_ref[...],
                         reverse=False),
              _ssd_stats(dtb_ref[bs, :], dttb_ref[n_heads:2 * n_heads, bs], a_b, a_col[n_heads:2 * n_heads, :],
                         eb_ref[...], reverse=True)) for fs, bs in chunks]
    for (fs, bs), (stats_f, stats_b) in zip(chunks, stats):
        yf_ref[fs, :] = _ssd_main(xf_ref[fs, :], dttf_ref[0:n_heads, fs], stats_f, sf_ref, dskip_ref[...],
                                  lane_off=0, d_inner=d_inner)
        yb_ref[bs, :] = _ssd_main(xb_ref[bs, :], dttb_ref[n_heads:2 * n_heads, bs], stats_b, sb_ref, None,
                                  lane_off=n_heads, d_inner=d_inner)


def _ssd(xbc, dt, dtt, alog_row, alog_col, d_skip, e_f, e_b, *, d_inner):
    b, s, c = xbc.shape
    L = SCAN_CHUNK * SCAN_STEP_CHUNKS
    nc = s // L
    n_dt = dtt.shape[1]
    hpg_w = d_inner // SSD_N_GROUPS
    fwd = lambda w: pl.BlockSpec((None, L, w), lambda bi, i: (bi, i, 0))
    bwd = lambda w: pl.BlockSpec((None, L, w), lambda bi, i: (bi, nc - 1 - i, 0))
    return pl.pallas_call(
        functools.partial(_ssd_kernel, d_inner=d_inner),
        out_shape=[jax.ShapeDtypeStruct((b, s, d_inner), BF16)] * 2,
        grid=(b, nc),
        in_specs=[
            fwd(c), bwd(c), fwd(LANES), bwd(LANES),
            pl.BlockSpec((None, n_dt, L), lambda bi, i: (bi, 0, i)),
            pl.BlockSpec((None, n_dt, L), lambda bi, i: (bi, 0, nc - 1 - i)),
            _const_spec(alog_row.shape), _const_spec(alog_col.shape), _const_spec(d_skip.shape),
            _const_spec(e_f.shape), _const_spec(e_b.shape),
        ],
        out_specs=[fwd(d_inner), bwd(d_inner)],
        scratch_shapes=[pltpu.VMEM((SSD_N_GROUPS, SSD_D_STATE, hpg_w), F32)] * 2,
        compiler_params=_params(("parallel", "arbitrary")),
        name="ssd_scan",
    )(xbc, xbc, dt, dt, dtt, dtt, alog_row, alog_col, d_skip, e_f, e_b)


def _attn_kernel(bounded_ref, q_ref, k_ref, vt_ref, o_ref):
    def scores(hd):
        sl = slice(hd * LANES, (hd + 1) * LANES)
        return _dot_nt(k_ref[:, sl], q_ref[:, sl])

    def attend(subtract_max):
        outs = []
        st_next = scores(0)
        for hd in range(MLA_N_HEADS):
            st = st_next
            if hd + 1 < MLA_N_HEADS:
                st_next = scores(hd + 1)
            if subtract_max:
                st = st - jnp.max(st, axis=0, keepdims=True)
            p = jnp.exp2(st).astype(BF16)
            acc = _dot(vt_ref[hd * MLA_VT_ROWS:(hd + 1) * MLA_VT_ROWS, :], p)
            outs.append(acc[:MLA_V] / acc[MLA_V:MLA_V + 1])
        o_ref[...] = jnp.concatenate(outs, axis=0).T.astype(BF16)

    @pl.when(bounded_ref[0, 0] != 0)
    def _():
        attend(subtract_max=False)

    @pl.when(bounded_ref[0, 0] == 0)
    def _():
        attend(subtract_max=True)


def _attention(bounded, q, k, vt):
    b, s, n_q = q.shape
    n_vt = vt.shape[1]
    n_v = MLA_N_HEADS * MLA_V
    tq = ATTN_Q_TILE
    return pl.pallas_call(
        _attn_kernel,
        out_shape=jax.ShapeDtypeStruct((b, s, n_v), BF16),
        grid=(b, s // tq),
        in_specs=[
            pl.BlockSpec(memory_space=pltpu.SMEM),
            pl.BlockSpec((None, tq, n_q), lambda bi, i: (bi, i, 0)),
            pl.BlockSpec((None, s, n_q), lambda bi, i: (bi, 0, 0), pipeline_mode=pl.Buffered(1)),
            pl.BlockSpec((None, n_vt, s), lambda bi, i: (bi, 0, 0), pipeline_mode=pl.Buffered(1)),
        ],
        out_specs=pl.BlockSpec((None, tq, n_v), lambda bi, i: (bi, i, 0)),
        compiler_params=_params(("parallel", "arbitrary")),
        name="mla_attention",
    )(bounded, q, k, vt)


def _even_out_kernel(x_ref, yf_ref, yb_ref, z_ref, o_ref, g_ref, w_ref, fg_ref, fwin_ref, fwout_ref,
                     out_ref, *, d_inner):
    z = z_ref[...].astype(F32)
    y = (yf_ref[...].astype(F32) + yb_ref[...].astype(F32)) * (z * _sigmoid(z))
    gw = d_inner // SSD_N_GROUPS
    normed = []
    for g in range(SSD_N_GROUPS):
        seg = y[:, g * gw:(g + 1) * gw]
        normed.append(seg * lax.rsqrt(jnp.mean(seg * seg, axis=-1, keepdims=True) + EPS))
    yn = (jnp.concatenate(normed, axis=1) * g_ref[...]).astype(BF16)
    x = x_ref[...] + _dot(yn, w_ref[0:d_inner, :]) + _dot(o_ref[...], w_ref[d_inner:, :])
    out_ref[...] = _ffn_apply(x, fg_ref, fwin_ref, fwout_ref)


def _even_out(x, yf, yb, z, o, g, w, ffn, layer):
    t, d = x.shape
    d_inner = yf.shape[1]
    tm = TOKEN_TILE
    tok = lambda c: pl.BlockSpec((tm, c), lambda i: (i, 0))
    return pl.pallas_call(
        functools.partial(_even_out_kernel, d_inner=d_inner),
        out_shape=jax.ShapeDtypeStruct((t, d), F32),
        grid=(t // tm,),
        in_specs=[tok(d), tok(d_inner), tok(d_inner), tok(d_inner), tok(o.shape[1]),
                  _const_spec(g.shape), _const_spec(w.shape)] + _layer_specs(ffn, layer),
        out_specs=tok(d),
        compiler_params=_params(("parallel",)),
        name="even_out_ffn",
    )(x, yf, yb, z, o, g, w, *ffn)


def _odd_in_kernel(x_ref, g_ref, w_ref, gb_ref, qt_ref, k_ref, kt_ref, v_ref, o_ref, gates_ref, gatest_ref):
    h = _rms(x_ref[...], g_ref[...]).astype(BF16)
    hw = _dot(h, w_ref[...])
    n_qk = ML_N_HEADS * ML_QK
    n_v = ML_N_HEADS * ML_V
    qt_ref[...] = hw[:, :n_qk].T.astype(BF16)
    k = hw[:, n_qk:2 * n_qk]
    k_ref[...] = k.astype(BF16)
    kt_ref[...] = k.T.astype(BF16)
    v_ref[...] = hw[:, 2 * n_qk:2 * n_qk + n_v].astype(BF16)
    o_ref[...] = hw[:, 2 * n_qk + n_v:2 * n_qk + 2 * n_v].astype(BF16)
    pre = hw[:, 2 * n_qk + 2 * n_v:] + gb_ref[...]
    lane = lax.broadcasted_iota(jnp.int32, (1, LANES), 1)
    gates = jnp.where(lane < 2 * ML_N_HEADS, pre, -_softplus(-pre))
    gates_ref[...] = gates
    gatest_ref[...] = gates.T[:gatest_ref.shape[0], :]


def _odd_in(x, g, w, gate_bias):
    b, s, d = x.shape
    tm = TOKEN_TILE
    n_qk = ML_N_HEADS * ML_QK
    n_v = ML_N_HEADS * ML_V
    n_g = 4 * ML_N_HEADS
    tok = lambda c: pl.BlockSpec((None, tm, c), lambda bi, i: (bi, i, 0))
    tok_t = lambda c: pl.BlockSpec((None, c, tm), lambda bi, i: (bi, 0, i))
    return pl.pallas_call(
        _odd_in_kernel,
        out_shape=[
            jax.ShapeDtypeStruct((b, n_qk, s), BF16),
            jax.ShapeDtypeStruct((b, s, n_qk), BF16),
            jax.ShapeDtypeStruct((b, n_qk, s), BF16),
            jax.ShapeDtypeStruct((b, s, n_v), BF16),
            jax.ShapeDtypeStruct((b, s, n_v), BF16),
            jax.ShapeDtypeStruct((b, s, LANES), F32),
            jax.ShapeDtypeStruct((b, n_g, s), F32),
        ],
        grid=(b, s // tm),
        in_specs=[tok(d), _const_spec(g.shape), _const_spec(w.shape), _const_spec(gate_bias.shape)],
        out_specs=[tok_t(n_qk), tok(n_qk), tok_t(n_qk), tok(n_v), tok(n_v), tok(LANES), tok_t(n_g)],
        compiler_params=_params(("parallel", "parallel")),
        name="odd_in",
    )(x, g, w, gate_bias)


def _cummax_lanes(u, reverse):
    n = u.shape[1]
    lane = lax.broadcasted_iota(jnp.int32, (1, n), 1)
    d = 1
    while d < n:
        if reverse:
            shifted, valid = pltpu.roll(u, n - d, axis=1), lane < n - d
        else:
            shifted, valid = pltpu.roll(u, d, axis=1), lane >= d
        u = jnp.where(valid, jnp.maximum(u, shifted), u)
        d *= 2
    return u


def _mlstm_stats(gates, gatest, m_ref, *, i_off, f_off, reverse):
    L = gates.shape[0]
    H = ML_N_HEADS
    assert L == LANES == ML_V and 2 * ML_QK == LANES
    keep, keep_t = _scan_masks(L, reverse)
    keep_bf = keep.astype(F32).astype(BF16)
    keep_t_bf = keep_t.astype(F32).astype(BF16)
    last = 0 if reverse else L - 1

    li_r = gatest[i_off:i_off + H, :]
    bcum_r = sum(_dot(t, keep_t_bf) for t in _split_terms(gatest[f_off:f_off + H, :], 3))
    u_r = li_r - bcum_r
    m_in = m_ref[...]
    mx = jnp.maximum(m_in, _cummax_lanes(u_r, reverse))
    w_inter = jnp.exp(m_in - mx)
    e_negm = jnp.exp(-(bcum_r + mx))
    b_end = jnp.broadcast_to(bcum_r[:, last:last + 1], (H, L))
    g_end = b_end + u_r
    m_new = jnp.maximum(b_end + m_in, jnp.max(g_end, axis=1, keepdims=True))
    w_src = jnp.exp(g_end - m_new)
    decay = jnp.exp(b_end + m_in - m_new)
    m_ref[...] = m_new

    bcum_c = sum(_dot(keep_bf, t) for t in _split_terms(gates, 3))
    u_all = jnp.concatenate(
        [jnp.broadcast_to(gates[:, i_off + hd:i_off + hd + 1] - bcum_c[:, f_off + hd:f_off + hd + 1], (L, LANES))
         for hd in range(H)], axis=1)
    return dict(mx=mx, w_inter=w_inter, e_negm=e_negm, w_src=w_src, decay=decay, u_all=u_all, keep_t=keep_t)


def _mlstm_main(k, kt, qt, v, stats, c_ref):
    L = k.shape[0]
    H = ML_N_HEADS
    mx, w_inter, e_negm, w_src = stats["mx"], stats["w_inter"], stats["e_negm"], stats["w_src"]
    decay, u_all, keep_t = stats["decay"], stats["u_all"], stats["keep_t"]
    srow = lax.broadcasted_iota(jnp.int32, (LANES, 1), 0)
    first = srow < ML_QK
    ones = jnp.ones((L, ML_V), BF16)
    c_in = [c_ref[pr] for pr in range(H // 2)]
    qt_own, kt_own, st = [], [], []
    for pr in range(H // 2):
        psl = slice(pr * LANES, (pr + 1) * LANES)
        zero = jnp.zeros((LANES, L), BF16)
        qt_own += [jnp.where(first, qt[psl, :], zero), jnp.where(first, zero, qt[psl, :])]
        kt_own += [jnp.where(first, kt[psl, :], zero), jnp.where(first, zero, kt[psl, :])]
        st.append(_dot(k[:, psl], jnp.concatenate(qt_own[-2:], axis=1)))
    lhs_t = []
    for hd in range(H):
        row = lambda a: a[hd:hd + 1, :]
        c_aug = c_in[hd // 2]
        w = jnp.where(keep_t, jnp.exp(u_all[:, hd * LANES:(hd + 1) * LANES] - row(mx)), 0.0)
        sw = st[hd // 2][:, (hd % 2) * L:(hd % 2 + 1) * L] * w
        q_f32 = qt_own[hd].astype(F32)
        nq = jnp.sum(c_aug[:, ML_V:] * q_f32, axis=0, keepdims=True)
        den = jnp.sum(sw, axis=0, keepdims=True) + row(w_inter) * nq
        inv = 1.0 / jnp.maximum(jnp.abs(den), row(e_negm))
        lhs_t.append(jnp.concatenate(
            [(sw * inv).astype(BF16), (q_f32 * (row(w_inter) * inv)).astype(BF16)], axis=0))
    outs = []
    for hd in range(H):
        v_h = v[:, hd * ML_V:(hd + 1) * ML_V]
        rhs = jnp.concatenate([v_h, c_in[hd // 2][:, :ML_V].astype(BF16)], axis=0)
        outs.append(_dot_tn(lhs_t[hd], rhs))
    for pr in range(H // 2):
        update = jnp.zeros(c_in[pr].shape, F32)
        for hd in (2 * pr, 2 * pr + 1):
            wkt = (kt_own[hd].astype(F32) * w_src[hd:hd + 1, :]).astype(BF16)
            v_aug = jnp.concatenate([v[:, hd * ML_V:(hd + 1) * ML_V], ones], axis=1)
            update = update + _dot(wkt, v_aug)
        dec = jnp.where(first, decay[2 * pr:2 * pr + 1, :], decay[2 * pr + 1:2 * pr + 2, :])
        c_ref[pr] = c_in[pr] * jnp.concatenate([dec, dec], axis=1) + update
    return jnp.concatenate(outs, axis=1).astype(BF16)


def _mlstm_kernel(kf_ref, ktf_ref, qtf_ref, vf_ref, gf_ref, gtf_ref,
                  kb_ref, ktb_ref, qtb_ref, vb_ref, gb_ref, gtb_ref,
                  hf_ref, hb_ref, cf_ref, cb_ref, mf_ref, mb_ref):
    @pl.when(pl.program_id(1) == 0)
    def _():
        for ref in (cf_ref, cb_ref, mf_ref, mb_ref):
            ref[...] = jnp.zeros(ref.shape, F32)

    h = ML_N_HEADS
    chunks = [(slice(c * SCAN_CHUNK, (c + 1) * SCAN_CHUNK),
               slice((SCAN_STEP_CHUNKS - 1 - c) * SCAN_CHUNK, (SCAN_STEP_CHUNKS - c) * SCAN_CHUNK))
              for c in range(SCAN_STEP_CHUNKS)]
    stats = [(_mlstm_stats(gf_ref[fs, :], gtf_ref[:, fs], mf_ref, i_off=0, f_off=2 * h, reverse=False),
              _mlstm_stats(gb_ref[bs, :], gtb_ref[:, bs], mb_ref, i_off=h, f_off=3 * h, reverse=True))
             for fs, bs in chunks]
    for (fs, bs), (stats_f, stats_b) in zip(chunks, stats):
        hf_ref[fs, :] = _mlstm_main(kf_ref[fs, :], ktf_ref[:, fs], qtf_ref[:, fs], vf_ref[fs, :], stats_f, cf_ref)
        hb_ref[bs, :] = _mlstm_main(kb_ref[bs, :], ktb_ref[:, bs], qtb_ref[:, bs], vb_ref[bs, :], stats_b, cb_ref)


def _mlstm(qt, k, kt, v, gates, gatest):
    b, s, n_qk = k.shape
    n_v = v.shape[2]
    n_g = gatest.shape[1]
    L = SCAN_CHUNK * SCAN_STEP_CHUNKS
    nc = s // L
    fwd = lambda w: pl.BlockSpec((None, L, w), lambda bi, i: (bi, i, 0))
    bwd = lambda w: pl.BlockSpec((None, L, w), lambda bi, i: (bi, nc - 1 - i, 0))
    fwd_t = lambda c: pl.BlockSpec((None, c, L), lambda bi, i: (bi, 0, i))
    bwd_t = lambda c: pl.BlockSpec((None, c, L), lambda bi, i: (bi, 0, nc - 1 - i))
    state = pltpu.VMEM((ML_N_HEADS // 2, 2 * ML_QK, 2 * ML_V), F32)
    stab = pltpu.VMEM((ML_N_HEADS, LANES), F32)
    return pl.pallas_call(
        _mlstm_kernel,
        out_shape=[jax.ShapeDtypeStruct((b, s, n_v), BF16)] * 2,
        grid=(b, nc),
        in_specs=[fwd(n_qk), fwd_t(n_qk), fwd_t(n_qk), fwd(n_v), fwd(LANES), fwd_t(n_g),
                  bwd(n_qk), bwd_t(n_qk), bwd_t(n_qk), bwd(n_v), bwd(LANES), bwd_t(n_g)],
        out_specs=[fwd(n_v), bwd(n_v)],
        scratch_shapes=[state, state, stab, stab],
        compiler_params=_params(("parallel", "arbitrary")),
        name="mlstm_scan",
    )(k, kt, qt, v, gates, gatest, k, kt, qt, v, gates, gatest)


def _odd_out_kernel(x_ref, hf_ref, hb_ref, o_ref, g_ref, w_ref, fg_ref, fwin_ref, fwout_ref, out_ref):
    hs = hf_ref[...].astype(F32) + hb_ref[...].astype(F32)
    normed = []
    for hd in range(ML_N_HEADS):
        seg = hs[:, hd * ML_V:(hd + 1) * ML_V]
        normed.append(seg * lax.rsqrt(jnp.mean(seg * seg, axis=-1, keepdims=True) + EPS))
    gated = (_sigmoid(o_ref[...].astype(F32)) * (jnp.concatenate(normed, axis=1) * g_ref[...])).astype(BF16)
    x = x_ref[...] + _dot(gated, w_ref[...])
    out_ref[...] = _ffn_apply(x, fg_ref, fwin_ref, fwout_ref)


def _odd_out(x, hf, hb, o, g, w, ffn, layer):
    t, d = x.shape
    n_v = hf.shape[1]
    tm = TOKEN_TILE
    tok = lambda c: pl.BlockSpec((tm, c), lambda i: (i, 0))
    return pl.pallas_call(
        _odd_out_kernel,
        out_shape=jax.ShapeDtypeStruct((t, d), F32),
        grid=(t // tm,),
        in_specs=[tok(d), tok(n_v), tok(n_v), tok(n_v), _const_spec(g.shape), _const_spec(w.shape)]
        + _layer_specs(ffn, layer),
        out_specs=tok(d),
        compiler_params=_params(("parallel",)),
        name="odd_out_ffn",
    )(x, hf, hb, o, g, w, *ffn)


def _row(v):
    return v.reshape(1, -1).astype(F32)


def _pad_lanes(v, width=LANES):
    return jnp.pad(v, [(0, 0)] * (v.ndim - 1) + [(0, width - v.shape[-1])])


def _swap_halves(v):
    half = v.shape[-1] // 2
    return jnp.concatenate([v[..., half:], v[..., :half]], axis=-1)


def _on_rope_lanes(v):
    return jnp.pad(v, [(0, 0)] * (v.ndim - 1) + [(MLA_NOPE, LANES - MLA_NOPE - v.shape[-1])])


def _prep_even(ev_w_in, ev_w_out, conv_w, conv_b, dt_bias, a_log, d_skip, ssd_norm,
               q_a_norm, w_q_b, kv_a_norm, w_kv_b, q_norm, k_norm):
    d_inner = ssd_norm.shape[0]
    d_xbc = conv_b.shape[0]
    n_heads = a_log.shape[1]
    c0, c1 = d_inner, d_inner + d_xbc
    c2 = c1 + 2 * n_heads
    c3 = c2 + MLA_Q_RANK
    c4 = c3 + MLA_KV_RANK
    w_kpe = ev_w_in[:, c4:]
    small = jnp.concatenate([_pad_lanes(ev_w_in[:, c1:c2], MLA_NOPE), _pad_lanes(w_kpe, LANES - MLA_NOPE)], axis=1)
    w_in = jnp.concatenate([ev_w_in[:, :c1], ev_w_in[:, c2:c4], small,
                            _on_rope_lanes(_swap_halves(w_kpe))], axis=1).astype(BF16)
    d_qk = MLA_NOPE + MLA_ROPE
    wq3 = w_q_b.reshape(MLA_Q_RANK, MLA_N_HEADS, d_qk)
    wq = jnp.concatenate(
        [_pad_lanes(wq3).reshape(MLA_Q_RANK, -1),
         _on_rope_lanes(_swap_halves(wq3[:, :, MLA_NOPE:])).reshape(MLA_Q_RANK, -1)], axis=1).astype(BF16)
    q_scale = d_qk ** -0.5 * LOG2_E
    score_bound = (SCORE_ROUNDING_MARGIN * d_qk * q_scale
                   * jnp.max(jnp.abs(q_norm)) * jnp.max(jnp.abs(k_norm)))
    scores_bounded = (score_bound <= SAFE_LOG2_SCORE).astype(jnp.int32).reshape(1, 1)
    ones_blk = jnp.ones((LANES, LANES), BF16)
    zero_blk = jnp.zeros((LANES, LANES), BF16)
    pair_ones = jnp.concatenate([jnp.concatenate([ones_blk, zero_blk], axis=1),
                                 jnp.concatenate([zero_blk, ones_blk], axis=1)], axis=0)
    wkv = w_kv_b.reshape(MLA_KV_RANK, MLA_N_HEADS, MLA_NOPE + MLA_V)
    wkvb = jnp.concatenate(
        [_pad_lanes(wkv[:, :, :MLA_NOPE]).reshape(MLA_KV_RANK, -1),
         wkv[:, :, MLA_NOPE:].reshape(MLA_KV_RANK, -1)], axis=1).astype(BF16)
    head_of_lane = jnp.arange(d_inner) // SSD_HEAD_DIM
    e_f = (jnp.arange(LANES)[:, None] == head_of_lane[None, :]).astype(BF16)
    e_b = (jnp.arange(LANES)[:, None] == head_of_lane[None, :] + n_heads).astype(BF16)
    return dict(
        d_inner=d_inner, d_xbc=d_xbc, w_in=w_in, w_out=ev_w_out.astype(BF16),
        conv_w=_pad_lanes(conv_w.T, SUBLANES).T.astype(F32), conv_b=_row(conv_b),
        dt_bias=_pad_lanes(_row(dt_bias)), alog_row=_pad_lanes(_row(a_log)),
        alog_col=a_log.reshape(-1, 1).astype(F32), d_skip=_row(jnp.repeat(d_skip, SSD_HEAD_DIM)),
        ssd_norm=_row(ssd_norm), e_f=e_f, e_b=e_b,
        q_a_norm=_row(q_a_norm), wq=wq, kv_a_norm=_row(kv_a_norm), wkvb=wkvb, pair_ones=pair_ones,
        scores_bounded=scores_bounded,
        q_gain_cos=_pad_lanes(_row(q_norm)) * q_scale,
        q_gain_sin=_on_rope_lanes(_swap_halves(_row(q_norm)[:, MLA_NOPE:])) * q_scale,
        k_gain_cos=_pad_lanes(_row(k_norm)),
        k_gain_sin=_on_rope_lanes(_swap_halves(_row(k_norm)[:, MLA_NOPE:])))


def _rope_tables(s):
    pos = jnp.arange(s, dtype=F32)
    inv_freq = jnp.power(ROPE_BASE, -jnp.arange(0, MLA_ROPE, 2, dtype=F32) / MLA_ROPE)
    freqs = pos[:, None] * inv_freq[None, :]
    cos, sin = jnp.cos(freqs), jnp.sin(freqs)
    tail = jnp.ones((s, LANES - MLA_NOPE - MLA_ROPE), F32)
    cos_t = jnp.concatenate([jnp.ones((s, MLA_NOPE), F32), cos, cos, tail], axis=1)
    sin_t = _on_rope_lanes(jnp.concatenate([-sin, sin], axis=1))
    return cos_t, sin_t


def _even_mixer(x, norm_g, p, ffn2, layer):
    b, s, d = x.shape
    z, xbc, dt, dtt, q, k, vt = _even_in(x, norm_g, p, *_rope_tables(s))
    yf, yb = _ssd(xbc, dt, dtt, p["alog_row"], p["alog_col"], p["d_skip"], p["e_f"], p["e_b"],
                  d_inner=p["d_inner"])
    o = _attention(p["scores_bounded"], q, k, vt)
    flat = lambda a: a.reshape(b * s, a.shape[-1])
    return _even_out(flat(x), flat(yf), flat(yb), flat(z), flat(o), p["ssd_norm"],
                     p["w_out"], ffn2, layer).reshape(b, s, d)


def _prep_odd(od_w_in, od_w_out, ig_bias, fg_bias, ml_norm):
    n_qk = ML_N_HEADS * ML_QK
    n_main = 2 * n_qk + 2 * ML_N_HEADS * ML_V
    w_in = jnp.concatenate([od_w_in[:, :n_qk] * ML_QK ** -0.5, od_w_in[:, n_qk:n_main],
                            _pad_lanes(od_w_in[:, n_main:])], axis=1).astype(BF16)
    gate_bias = _pad_lanes(_row(jnp.concatenate([ig_bias.reshape(-1), fg_bias.reshape(-1)])))
    return dict(w_in=w_in, w_out=od_w_out.astype(BF16), gate_bias=gate_bias, ml_norm=_row(ml_norm))


def _odd_mixer(x, norm_g, p, ffn2, layer):
    b, s, d = x.shape
    qt, k, kt, v, o, gates, gatest = _odd_in(x, norm_g, p["w_in"], p["gate_bias"])
    hf, hb = _mlstm(qt, k, kt, v, gates, gatest)
    flat = lambda a: a.reshape(b * s, a.shape[-1])
    return _odd_out(flat(x), flat(hf), flat(hb), flat(o), p["ml_norm"], p["w_out"], ffn2, layer).reshape(b, s, d)


def kernel(x_prompt, x_sample, ffn1_norm, ffn1_w_in, ffn1_w_out, mix_norm, ffn2_norm, ffn2_w_in, ffn2_w_out,
           ev_w_in, ev_w_out, ssd_conv_w, ssd_conv_b, ssd_dt_bias, ssd_a_log, ssd_d_skip, ssd_norm,
           mla_q_a_norm, mla_w_q_b, mla_kv_a_norm, mla_w_kv_b, mla_q_norm, mla_k_norm,
           od_w_in, od_w_out, ml_ig_bias, ml_fg_bias, ml_norm):
    depth = ffn1_norm.shape[0]
    layers = []
    for layer in range(depth):
        j = layer // 2
        if layer % 2 == 0:
            mixer = functools.partial(_even_mixer, p=_prep_even(
                ev_w_in[j], ev_w_out[j], ssd_conv_w[j], ssd_conv_b[j], ssd_dt_bias[j], ssd_a_log[j],
                ssd_d_skip[j], ssd_norm[j], mla_q_a_norm[j], mla_w_q_b[j], mla_kv_a_norm[j],
                mla_w_kv_b[j], mla_q_norm[j], mla_k_norm[j]))
        else:
            mixer = functools.partial(_odd_mixer, p=_prep_odd(
                od_w_in[j], od_w_out[j], ml_ig_bias[j], ml_fg_bias[j], ml_norm[j]))
        layers.append(dict(mixer=mixer, mix_norm=_row(mix_norm[layer])))
    ffn1 = (ffn1_norm[:, None, :].astype(F32), ffn1_w_in.astype(BF16), ffn1_w_out.astype(BF16))
    ffn2 = (ffn2_norm[:, None, :].astype(F32), ffn2_w_in.astype(BF16), ffn2_w_out.astype(BF16))

    def trunk(x):
        b, s, d = x.shape
        for layer, lp in enumerate(layers):
            x = _ffn(x.reshape(b * s, d), ffn1, layer).reshape(b, s, d)
            x = lp["mixer"](x, lp["mix_norm"], ffn2=ffn2, layer=layer)
        return x

    return trunk(x_prompt), trunk(x_sample)
```

```python
import functools

import jax
import jax.numpy as jnp
from jax import lax
from jax.experimental import pallas as pl
from jax.experimental.pallas import tpu as pltpu

F32 = jnp.float32
BF16 = jnp.bfloat16
EPS = 1e-6

LANES = 128
SUBLANES = 8
BF16_SUBLANES = 16
VMEM_LIMIT = 56 * 1024 * 1024

SSD_HEAD_DIM = 64
SSD_N_GROUPS = 2
SSD_D_STATE = 128
SSD_CONV = 5
MLA_N_HEADS = 8
MLA_Q_RANK = 384
MLA_KV_RANK = 256
MLA_NOPE = 64
MLA_ROPE = 32
MLA_V = 64
MLA_VT_ROWS = MLA_V + BF16_SUBLANES
ROPE_BASE = 10000.0
LOG2_E = 1.4426950408889634
SAFE_LOG2_SCORE = 60.0
SCORE_ROUNDING_MARGIN = 1.02
ML_N_HEADS = 8
ML_QK = 64
ML_V = 128

TOKEN_TILE = 512
FFN_TOKEN_TILE = 1024
SCAN_CHUNK = 128
SCAN_STEP_CHUNKS = 4
ATTN_Q_TILE = 512
HALO = SUBLANES


def _params(semantics):
    return pltpu.CompilerParams(dimension_semantics=semantics, vmem_limit_bytes=VMEM_LIMIT)


def _const_spec(shape):
    nd = len(shape)
    return pl.BlockSpec(shape, lambda *_: (0,) * nd, pipeline_mode=pl.Buffered(1))


def _rms(x, g):
    ms = jnp.mean(x * x, axis=-1, keepdims=True)
    return x * lax.rsqrt(ms + EPS) * g


def _sigmoid(x):
    return 1.0 / (1.0 + jnp.exp(-x))


def _softplus(x):
    return jnp.maximum(x, 0.0) + jnp.log1p(jnp.exp(-jnp.abs(x)))


def _dot(a, b):
    return jnp.dot(a, b, preferred_element_type=F32)


def _dot_nt(a, b):
    return lax.dot_general(a, b, (((1,), (1,)), ((), ())), preferred_element_type=F32)


def _dot_tn(a, b):
    return lax.dot_general(a, b, (((0,), (0,)), ((), ())), preferred_element_type=F32)


def _split_terms(x, terms):
    out = []
    for _ in range(terms):
        piece = x.astype(BF16)
        out.append(piece)
        x = x - piece.astype(F32)
    return out


def _scan_masks(n, reverse):
    row = lax.broadcasted_iota(jnp.int32, (n, n), 0)
    col = lax.broadcasted_iota(jnp.int32, (n, n), 1)
    if reverse:
        return col >= row, col <= row
    return col <= row, col >= row


def _ffn_apply(x, g_ref, win_ref, wout_ref):
    d_ff = wout_ref.shape[0]
    h = _rms(x, g_ref[...]).astype(BF16)
    hw = _dot(h, win_ref[...])
    gate = hw[:, :d_ff]
    up = hw[:, d_ff:]
    a = (gate * _sigmoid(gate) * up).astype(BF16)
    return x + 0.5 * _dot(a, wout_ref[...])


def _ffn_kernel(x_ref, g_ref, win_ref, wout_ref, o_ref):
    o_ref[...] = _ffn_apply(x_ref[...], g_ref, win_ref, wout_ref)


def _layer_specs(ffn, layer):
    return [pl.BlockSpec((None,) + a.shape[1:], lambda *_, n=a.ndim: (layer,) + (0,) * (n - 1),
                         pipeline_mode=pl.Buffered(1)) for a in ffn]


def _ffn(x, ffn, layer):
    t, d = x.shape
    tm = FFN_TOKEN_TILE
    return pl.pallas_call(
        _ffn_kernel,
        out_shape=jax.ShapeDtypeStruct((t, d), F32),
        grid=(t // tm,),
        in_specs=[pl.BlockSpec((tm, d), lambda i: (i, 0))] + _layer_specs(ffn, layer),
        out_specs=pl.BlockSpec((tm, d), lambda i: (i, 0)),
        compiler_params=_params(("parallel",)),
        name="ffn",
    )(x, *ffn)


def _conv_silu(ext, w_ref, b_ref):
    n = ext.shape[0]
    tc = n - 2 * HALO
    pad = SSD_CONV // 2
    acc = b_ref[...] + w_ref[pad:pad + 1, :] * ext[HALO:HALO + tc, :]
    for k in range(SSD_CONV):
        if k != pad:
            acc = acc + w_ref[k:k + 1, :] * pltpu.roll(ext, (pad - k) % n, axis=0)[HALO:HALO + tc, :]
    return acc * _sigmoid(acc)


def _even_in_kernel(x_ref, xp_ref, xn_ref, g_ref, w_ref, dtb_ref, qan_ref, wq_ref, kvan_ref, wkvb_ref,
                    qgc_ref, qgs_ref, kgc_ref, kgs_ref, ones_ref, cw_ref, cb_ref, cos_ref, sin_ref,
                    z_ref, xbc_ref, dt_ref, dtt_ref, q_ref, k_ref, vt_ref,
                    *, d_inner, d_xbc, n_tiles):
    i = pl.program_id(1)
    tm = x_ref.shape[0]
    xe = jnp.concatenate([xp_ref[...], x_ref[...], xn_ref[...]], axis=0)
    hwe = _dot(_rms(xe, g_ref[...]).astype(BF16), w_ref[...])
    hw = hwe[HALO:HALO + tm, :]
    c0 = d_inner
    c1 = c0 + d_xbc
    c2 = c1 + MLA_Q_RANK
    c3 = c2 + MLA_KV_RANK
    z_ref[...] = hw[:, :c0].astype(BF16)
    zero = jnp.zeros((HALO, d_xbc), F32)
    ext = jnp.concatenate([jnp.where(i > 0, hwe[:HALO, c0:c1], zero), hw[:, c0:c1],
                           jnp.where(i < n_tiles - 1, hwe[HALO + tm:, c0:c1], zero)], axis=0)
    xbc_ref[...] = _conv_silu(ext, cw_ref, cb_ref)
    small = hw[:, c3:c3 + LANES]
    kpe_sw = hw[:, c3 + LANES:c3 + 2 * LANES]
    dt = _softplus(small + dtb_ref[...])
    dt_ref[...] = dt
    dtt_ref[...] = dt.T[:dtt_ref.shape[0], :]

    n_k = MLA_N_HEADS * LANES
    qa = _rms(hw[:, c1:c2], qan_ref[...]).astype(BF16)
    qq = _dot(qa, wq_ref[...])
    kva = _rms(hw[:, c2:c3], kvan_ref[...]).astype(BF16)
    kv = _dot(kva, wkvb_ref[...])
    v_t = kv[:, n_k:].T
    ones_rows = jnp.ones((MLA_VT_ROWS - MLA_V, v_t.shape[1]), F32)
    vt_ref[...] = jnp.concatenate(
        [blk for hd in range(MLA_N_HEADS) for blk in (v_t[hd * MLA_V:(hd + 1) * MLA_V, :], ones_rows)],
        axis=0).astype(BF16)

    lane = lax.broadcasted_iota(jnp.int32, (1, LANES), 1)
    pe_lanes = (lane >= MLA_NOPE) & (lane < MLA_NOPE + MLA_ROPE)
    kpe = jnp.where(pe_lanes, small, 0.0)
    cos_t, sin_t = cos_ref[...], sin_ref[...]
    q_cos, q_sin = cos_t * qgc_ref[...], sin_t * qgs_ref[...]
    k_cos, k_sin = cos_t * kgc_ref[...], sin_t * kgs_ref[...]
    k_rot = kpe_sw * k_sin
    inv_dim = 1.0 / (MLA_NOPE + MLA_ROPE)

    q_blk = [qq[:, hd * LANES:(hd + 1) * LANES] for hd in range(MLA_N_HEADS)]
    k_blk = [kv[:, hd * LANES:(hd + 1) * LANES] + kpe for hd in range(MLA_N_HEADS)]
    q_ss, k_ss = [], []
    for pr in range(MLA_N_HEADS // 2):
        for blk, out in ((q_blk, q_ss), (k_blk, k_ss)):
            sq = jnp.concatenate([blk[2 * pr] * blk[2 * pr], blk[2 * pr + 1] * blk[2 * pr + 1]], axis=1)
            ss = _dot(sq.astype(BF16), ones_ref[...])
            out.extend([ss[:, :LANES], ss[:, LANES:]])
    for hd in range(MLA_N_HEADS):
        sl = slice(hd * LANES, (hd + 1) * LANES)
        q_rot = qq[:, n_k + hd * LANES:n_k + (hd + 1) * LANES]
        rq = lax.rsqrt(q_ss[hd] * inv_dim + EPS)
        q_ref[:, sl] = ((q_blk[hd] * q_cos + q_rot * q_sin) * rq).astype(BF16)
        rk = lax.rsqrt(k_ss[hd] * inv_dim + EPS)
        k_ref[:, sl] = ((k_blk[hd] * k_cos + k_rot) * rk).astype(BF16)


def _even_in(x, g, p, cos_t, sin_t):
    b, s, d = x.shape
    tm = TOKEN_TILE
    d_inner, d_xbc = p["d_inner"], p["d_xbc"]
    n_q = MLA_N_HEADS * LANES
    n_v = MLA_N_HEADS * MLA_VT_ROWS
    n_dt = 2 * (d_inner // SSD_HEAD_DIM)
    tok = lambda c: pl.BlockSpec((None, tm, c), lambda bi, i: (bi, i, 0))
    tab = pl.BlockSpec((tm, LANES), lambda bi, i: (i, 0))
    consts = [g, p["w_in"], p["dt_bias"], p["q_a_norm"], p["wq"], p["kv_a_norm"], p["wkvb"],
              p["q_gain_cos"], p["q_gain_sin"], p["k_gain_cos"], p["k_gain_sin"], p["pair_ones"],
              p["conv_w"], p["conv_b"]]
    n_tiles = s // tm
    per = tm // HALO
    n_halo = s // HALO
    halo_prev = pl.BlockSpec((None, HALO, d), lambda bi, i: (bi, jnp.maximum(i * per - 1, 0), 0))
    halo_next = pl.BlockSpec((None, HALO, d), lambda bi, i: (bi, jnp.minimum((i + 1) * per, n_halo - 1), 0))
    return pl.pallas_call(
        functools.partial(_even_in_kernel, d_inner=d_inner, d_xbc=d_xbc, n_tiles=n_tiles),
        out_shape=[
            jax.ShapeDtypeStruct((b, s, d_inner), BF16),
            jax.ShapeDtypeStruct((b, s, d_xbc), F32),
            jax.ShapeDtypeStruct((b, s, LANES), F32),
            jax.ShapeDtypeStruct((b, n_dt, s), F32),
            jax.ShapeDtypeStruct((b, s, n_q), BF16),
            jax.ShapeDtypeStruct((b, s, n_q), BF16),
            jax.ShapeDtypeStruct((b, n_v, s), BF16),
        ],
        grid=(b, s // tm),
        in_specs=[tok(d), halo_prev, halo_next] + [_const_spec(c.shape) for c in consts] + [tab, tab],
        out_specs=[
            tok(d_inner), tok(d_xbc), tok(LANES),
            pl.BlockSpec((None, n_dt, tm), lambda bi, i: (bi, 0, i)),
            tok(n_q), tok(n_q),
            pl.BlockSpec((None, n_v, tm), lambda bi, i: (bi, 0, i)),
        ],
        compiler_params=_params(("parallel", "parallel")),
        name="even_in",
    )(x, x, x, *consts, cos_t, sin_t)


def _ssd_stats(dt, dtt, a_row, a_col, expand, *, reverse):
    L = dt.shape[0]
    keep, keep_t = _scan_masks(L, reverse)
    last = 0 if reverse else L - 1
    keep_bf = keep.astype(F32).astype(BF16)
    keep_t_bf = keep_t.astype(F32).astype(BF16)
    cum = sum(_dot(keep_bf, t) for t in _split_terms(dt * a_row, 3))
    cumt = sum(_dot(t, keep_t_bf) for t in _split_terms(dtt * a_col, 3))
    cum_last = cum[last:last + 1, :]
    narrow = jnp.concatenate(
        [jnp.exp(cum), dt * jnp.exp(cum_last - cum),
         jnp.broadcast_to(jnp.exp(cum_last), (SUBLANES, LANES))], axis=0)
    wide = _dot(narrow.astype(BF16), expand)
    return dict(cum=cum, cumt=cumt, keep=keep, e_off=wide[:L], w_state=wide[L:2 * L],
                chunk_decay=wide[2 * L:2 * L + 1])


def _ssd_main(xbc, dtt, stats, s_ref, d_skip, *, lane_off, d_inner):
    n_heads = d_inner // SSD_HEAD_DIM
    hpg = n_heads // SSD_N_GROUPS
    gw = hpg * SSD_HEAD_DIM
    gn = SSD_D_STATE
    xs = xbc[:, :d_inner]
    cum, cumt, keep = stats["cum"], stats["cumt"], stats["keep"]
    e_off, w_state, chunk_decay = stats["e_off"], stats["w_state"], stats["chunk_decay"]
    lane = lax.broadcasted_iota(jnp.int32, (1, LANES), 1)
    lo_lanes = lane < SSD_HEAD_DIM
    parts = []
    for g in range(SSD_N_GROUPS):
        gsl = slice(g * gw, (g + 1) * gw)
        b_g = xbc[:, d_inner + g * gn:d_inner + (g + 1) * gn]
        c_off = d_inner + SSD_N_GROUPS * gn
        c_bf = xbc[:, c_off + g * gn:c_off + (g + 1) * gn].astype(BF16)
        cb = _dot_nt(c_bf, b_g.astype(BF16))
        state = s_ref[g]
        y_off = _dot(c_bf, state.astype(BF16)) * e_off[:, gsl]
        xw = (xs[:, gsl] * w_state[:, gsl]).astype(BF16)
        s_ref[g] = state * chunk_decay[:, gsl] + _dot(b_g.T.astype(BF16), xw)
        for pr in range(hpg // 2):
            m_pair = []
            for j in range(2):
                hd = g * hpg + 2 * pr + j
                seg = cum[:, lane_off + hd:lane_off + hd + 1] - cumt[hd:hd + 1, :]
                m_h = jnp.where(keep, cb * jnp.exp(seg) * dtt[hd:hd + 1, :], 0.0)
                m_pair.append(m_h.astype(BF16))
            psl = slice(g * gw + pr * LANES, g * gw + (pr + 1) * LANES)
            x_pair = xs[:, psl]
            rhs = jnp.concatenate([jnp.where(lo_lanes, x_pair, 0.0),
                                   jnp.where(lo_lanes, 0.0, x_pair)], axis=0).astype(BF16)
            y_pair = _dot(jnp.concatenate(m_pair, axis=1), rhs) + y_off[:, pr * LANES:(pr + 1) * LANES]
            if d_skip is not None:
                y_pair = y_pair + x_pair * d_skip[:, psl]
            parts.append(y_pair)
    return jnp.concatenate(parts, axis=1).astype(BF16)


def _ssd_kernel(xf_ref, xb_ref, dtf_ref, dtb_ref, dttf_ref, dttb_ref, alog_row_ref, alog_col_ref,
                dskip_ref, ef_ref, eb_ref, yf_ref, yb_ref, sf_ref, sb_ref, *, d_inner):
    @pl.when(pl.program_id(1) == 0)
    def _():
        sf_ref[...] = jnp.zeros(sf_ref.shape, F32)
        sb_ref[...] = jnp.zeros(sb_ref.shape, F32)

    n_heads = d_inner // SSD_HEAD_DIM
    lane = lax.broadcasted_iota(jnp.int32, (1, LANES), 1)
    a_all = -jnp.exp(alog_row_ref[...])
    a_col = -jnp.exp(alog_col_ref[...])
    a_f = jnp.where(lane < n_heads, a_all, 0.0)
    a_b = jnp.where((lane >= n_heads) & (lane < 2 * n_heads), a_all, 0.0)
    chunks = [(slice(c * SCAN_CHUNK, (c + 1) * SCAN_CHUNK),
               slice((SCAN_STEP_CHUNKS - 1 - c) * SCAN_CHUNK, (SCAN_STEP_CHUNKS - c) * SCAN_CHUNK))
              for c in range(SCAN_STEP_CHUNKS)]
    stats = [(_ssd_stats(dtf_ref[fs, :], dttf_ref[0:n_heads, fs], a_f, a_col[0:n_heads, :], ef_ref[...],
                         reverse=False),
              _ssd_stats(dtb_ref[bs, :], dttb_ref[n_heads:2 * n_heads, bs], a_b, a_col[n_heads:2 * n_heads, :],
                         eb_ref[...], reverse=True)) for fs, bs in chunks]
    for (fs, bs), (stats_f, stats_b) in zip(chunks, stats):
        yf_ref[fs, :] = _ssd_main(xf_ref[fs, :], dttf_ref[0:n_heads, fs], stats_f, sf_ref, dskip_ref[...],
                                  lane_off=0, d_inner=d_inner)
        yb_ref[bs, :] = _ssd_main(xb_ref[bs, :], dttb_ref[n_heads:2 * n_heads, bs], stats_b, sb_ref, None,
                                  lane_off=n_heads, d_inner=d_inner)


def _ssd(xbc, dt, dtt, alog_row, alog_col, d_skip, e_f, e_b, *, d_inner):
    b, s, c = xbc.shape
    L = SCAN_CHUNK * SCAN_STEP_CHUNKS
    nc = s // L
    n_dt = dtt.shape[1]
    hpg_w = d_inner // SSD_N_GROUPS
    fwd = lambda w: pl.BlockSpec((None, L, w), lambda bi, i: (bi, i, 0))
    bwd = lambda w: pl.BlockSpec((None, L, w), lambda bi, i: (bi, nc - 1 - i, 0))
    return pl.pallas_call(
        functools.partial(_ssd_kernel, d_inner=d_inner),
        out_shape=[jax.ShapeDtypeStruct((b, s, d_inner), BF16)] * 2,
        grid=(b, nc),
        in_specs=[
            fwd(c), bwd(c), fwd(LANES), bwd(LANES),
            pl.BlockSpec((None, n_dt, L), lambda bi, i: (bi, 0, i)),
            pl.BlockSpec((None, n_dt, L), lambda bi, i: (bi, 0, nc - 1 - i)),
            _const_spec(alog_row.shape), _const_spec(alog_col.shape), _const_spec(d_skip.shape),
            _const_spec(e_f.shape), _const_spec(e_b.shape),
        ],
        out_specs=[fwd(d_inner), bwd(d_inner)],
        scratch_shapes=[pltpu.VMEM((SSD_N_GROUPS, SSD_D_STATE, hpg_w), F32)] * 2,
        compiler_params=_params(("parallel", "arbitrary")),
        name="ssd_scan",
    )(xbc, xbc, dt, dt, dtt, dtt, alog_row, alog_col, d_skip, e_f, e_b)


def _attn_kernel(bounded_ref, q_ref, k_ref, vt_ref, o_ref):
    def scores(hd):
        sl = slice(hd * LANES, (hd + 1) * LANES)
        return _dot_nt(k_ref[:, sl], q_ref[:, sl])

    def attend(subtract_max):
        outs = []
        st_next = scores(0)
        for hd in range(MLA_N_HEADS):
            st = st_next
            if hd + 1 < MLA_N_HEADS:
                st_next = scores(hd + 1)
            if subtract_max:
                st = st - jnp.max(st, axis=0, keepdims=True)
            p = jnp.exp2(st).astype(BF16)
            acc = _dot(vt_ref[hd * MLA_VT_ROWS:(hd + 1) * MLA_VT_ROWS, :], p)
            outs.append(acc[:MLA_V] / acc[MLA_V:MLA_V + 1])
        o_ref[...] = jnp.concatenate(outs, axis=0).T.astype(BF16)

    @pl.when(bounded_ref[0, 0] != 0)
    def _():
        attend(subtract_max=False)

    @pl.when(bounded_ref[0, 0] == 0)
    def _():
        attend(subtract_max=True)


def _attention(bounded, q, k, vt):
    b, s, n_q = q.shape
    n_vt = vt.shape[1]
    n_v = MLA_N_HEADS * MLA_V
    tq = ATTN_Q_TILE
    return pl.pallas_call(
        _attn_kernel,
        out_shape=jax.ShapeDtypeStruct((b, s, n_v), BF16),
        grid=(b, s // tq),
        in_specs=[
            pl.BlockSpec(memory_space=pltpu.SMEM),
            pl.BlockSpec((None, tq, n_q), lambda bi, i: (bi, i, 0)),
            pl.BlockSpec((None, s, n_q), lambda bi, i: (bi, 0, 0)),
            pl.BlockSpec((None, n_vt, s), lambda bi, i: (bi, 0, 0)),
        ],
        out_specs=pl.BlockSpec((None, tq, n_v), lambda bi, i: (bi, i, 0)),
        compiler_params=_params(("parallel", "arbitrary")),
        name="mla_attention",
    )(bounded, q, k, vt)


def _even_out_kernel(x_ref, yf_ref, yb_ref, z_ref, o_ref, g_ref, w_ref, fg_ref, fwin_ref, fwout_ref,
                     out_ref, *, d_inner):
    z = z_ref[...].astype(F32)
    y = (yf_ref[...].astype(F32) + yb_ref[...].astype(F32)) * (z * _sigmoid(z))
    gw = d_inner // SSD_N_GROUPS
    normed = []
    for g in range(SSD_N_GROUPS):
        seg = y[:, g * gw:(g + 1) * gw]
        normed.append(seg * lax.rsqrt(jnp.mean(seg * seg, axis=-1, keepdims=True) + EPS))
    yn = (jnp.concatenate(normed, axis=1) * g_ref[...]).astype(BF16)
    x = x_ref[...] + _dot(yn, w_ref[0:d_inner, :]) + _dot(o_ref[...], w_ref[d_inner:, :])
    out_ref[...] = _ffn_apply(x, fg_ref, fwin_ref, fwout_ref)


def _even_out(x, yf, yb, z, o, g, w, ffn, layer):
    t, d = x.shape
    d_inner = yf.shape[1]
    tm = TOKEN_TILE
    tok = lambda c: pl.BlockSpec((tm, c), lambda i: (i, 0))
    return pl.pallas_call(
        functools.partial(_even_out_kernel, d_inner=d_inner),
        out_shape=jax.ShapeDtypeStruct((t, d), F32),
        grid=(t // tm,),
        in_specs=[tok(d), tok(d_inner), tok(d_inner), tok(d_inner), tok(o.shape[1]),
                  _const_spec(g.shape), _const_spec(w.shape)] + _layer_specs(ffn, layer),
        out_specs=tok(d),
        compiler_params=_params(("parallel",)),
        name="even_out_ffn",
    )(x, yf, yb, z, o, g, w, *ffn)


def _odd_in_kernel(x_ref, g_ref, w_ref, gb_ref, qt_ref, k_ref, kt_ref, v_ref, o_ref, gates_ref, gatest_ref):
    h = _rms(x_ref[...], g_ref[...]).astype(BF16)
    hw = _dot(h, w_ref[...])
    n_qk = ML_N_HEADS * ML_QK
    n_v = ML_N_HEADS * ML_V
    qt_ref[...] = hw[:, :n_qk].T.astype(BF16)
    k = hw[:, n_qk:2 * n_qk]
    k_ref[...] = k.astype(BF16)
    kt_ref[...] = k.T.astype(BF16)
    v_ref[...] = hw[:, 2 * n_qk:2 * n_qk + n_v].astype(BF16)
    o_ref[...] = hw[:, 2 * n_qk + n_v:2 * n_qk + 2 * n_v].astype(BF16)
    pre = hw[:, 2 * n_qk + 2 * n_v:] + gb_ref[...]
    lane = lax.broadcasted_iota(jnp.int32, (1, LANES), 1)
    gates = jnp.where(lane < 2 * ML_N_HEADS, pre, -_softplus(-pre))
    gates_ref[...] = gates
    gatest_ref[...] = gates.T[:gatest_ref.shape[0], :]


def _odd_in(x, g, w, gate_bias):
    b, s, d = x.shape
    tm = TOKEN_TILE
    n_qk = ML_N_HEADS * ML_QK
    n_v = ML_N_HEADS * ML_V
    n_g = 4 * ML_N_HEADS
    tok = lambda c: pl.BlockSpec((None, tm, c), lambda bi, i: (bi, i, 0))
    tok_t = lambda c: pl.BlockSpec((None, c, tm), lambda bi, i: (bi, 0, i))
    return pl.pallas_call(
        _odd_in_kernel,
        out_shape=[
            jax.ShapeDtypeStruct((b, n_qk, s), BF16),
            jax.ShapeDtypeStruct((b, s, n_qk), BF16),
            jax.ShapeDtypeStruct((b, n_qk, s), BF16),
            jax.ShapeDtypeStruct((b, s, n_v), BF16),
            jax.ShapeDtypeStruct((b, s, n_v), BF16),
            jax.ShapeDtypeStruct((b, s, LANES), F32),
            jax.ShapeDtypeStruct((b, n_g, s), F32),
        ],
        grid=(b, s // tm),
        in_specs=[tok(d), _const_spec(g.shape), _const_spec(w.shape), _const_spec(gate_bias.shape)],
        out_specs=[tok_t(n_qk), tok(n_qk), tok_t(n_qk), tok(n_v), tok(n_v), tok(LANES), tok_t(n_g)],
        compiler_params=_params(("parallel", "parallel")),
        name="odd_in",
    )(x, g, w, gate_bias)


def _cummax_lanes(u, reverse):
    n = u.shape[1]
    lane = lax.broadcasted_iota(jnp.int32, (1, n), 1)
    d = 1
    while d < n:
        if reverse:
            shifted, valid = pltpu.roll(u, n - d, axis=1), lane < n - d
        else:
            shifted, valid = pltpu.roll(u, d, axis=1), lane >= d
        u = jnp.where(valid, jnp.maximum(u, shifted), u)
        d *= 2
    return u


def _mlstm_stats(gates, gatest, m_ref, *, i_off, f_off, reverse):
    L = gates.shape[0]
    H = ML_N_HEADS
    assert L == LANES == ML_V and 2 * ML_QK == LANES
    keep, keep_t = _scan_masks(L, reverse)
    keep_bf = keep.astype(F32).astype(BF16)
    keep_t_bf = keep_t.astype(F32).astype(BF16)
    last = 0 if reverse else L - 1

    li_r = gatest[i_off:i_off + H, :]
    bcum_r = sum(_dot(t, keep_t_bf) for t in _split_terms(gatest[f_off:f_off + H, :], 3))
    u_r = li_r - bcum_r
    m_in = m_ref[...]
    mx = jnp.maximum(m_in, _cummax_lanes(u_r, reverse))
    w_inter = jnp.exp(m_in - mx)
    e_negm = jnp.exp(-(bcum_r + mx))
    b_end = jnp.broadcast_to(bcum_r[:, last:last + 1], (H, L))
    g_end = b_end + u_r
    m_new = jnp.maximum(b_end + m_in, jnp.max(g_end, axis=1, keepdims=True))
    w_src = jnp.exp(g_end - m_new)
    decay = jnp.exp(b_end + m_in - m_new)
    m_ref[...] = m_new

    bcum_c = sum(_dot(keep_bf, t) for t in _split_terms(gates, 3))
    u_all = jnp.concatenate(
        [jnp.broadcast_to(gates[:, i_off + hd:i_off + hd + 1] - bcum_c[:, f_off + hd:f_off + hd + 1], (L, LANES))
         for hd in range(H)], axis=1)
    return dict(mx=mx, w_inter=w_inter, e_negm=e_negm, w_src=w_src, decay=decay, u_all=u_all, keep_t=keep_t)


def _mlstm_main(k, kt, qt, v, stats, c_ref):
    L = k.shape[0]
    H = ML_N_HEADS
    mx, w_inter, e_negm, w_src = stats["mx"], stats["w_inter"], stats["e_negm"], stats["w_src"]
    decay, u_all, keep_t = stats["decay"], stats["u_all"], stats["keep_t"]
    srow = lax.broadcasted_iota(jnp.int32, (LANES, 1), 0)
    first = srow < ML_QK
    ones = jnp.ones((L, ML_V), BF16)
    c_in = [c_ref[pr] for pr in range(H // 2)]
    qt_own, kt_own, st = [], [], []
    for pr in range(H // 2):
        psl = slice(pr * LANES, (pr + 1) * LANES)
        zero = jnp.zeros((LANES, L), BF16)
        qt_own += [jnp.where(first, qt[psl, :], zero), jnp.where(first, zero, qt[psl, :])]
        kt_own += [jnp.where(first, kt[psl, :], zero), jnp.where(first, zero, kt[psl, :])]
        st.append(_dot(k[:, psl], jnp.concatenate(qt_own[-2:], axis=1)))
    lhs_t = []
    for hd in range(H):
        row = lambda a: a[hd:hd + 1, :]
        c_aug = c_in[hd // 2]
        w = jnp.where(keep_t, jnp.exp(u_all[:, hd * LANES:(hd + 1) * LANES] - row(mx)), 0.0)
        sw = st[hd // 2][:, (hd % 2) * L:(hd % 2 + 1) * L] * w
        q_f32 = qt_own[hd].astype(F32)
        nq = jnp.sum(c_aug[:, ML_V:] * q_f32, axis=0, keepdims=True)
        den = jnp.sum(sw, axis=0, keepdims=True) + row(w_inter) * nq
        inv = 1.0 / jnp.maximum(jnp.abs(den), row(e_negm))
        lhs_t.append(jnp.concatenate(
            [(sw * inv).astype(BF16), (q_f32 * (row(w_inter) * inv)).astype(BF16)], axis=0))
    outs = []
    for hd in range(H):
        v_h = v[:, hd * ML_V:(hd + 1) * ML_V]
        rhs = jnp.concatenate([v_h, c_in[hd // 2][:, :ML_V].astype(BF16)], axis=0)
        outs.append(_dot_tn(lhs_t[hd], rhs))
    for pr in range(H // 2):
        update = jnp.zeros(c_in[pr].shape, F32)
        for hd in (2 * pr, 2 * pr + 1):
            wkt = (kt_own[hd].astype(F32) * w_src[hd:hd + 1, :]).astype(BF16)
            v_aug = jnp.concatenate([v[:, hd * ML_V:(hd + 1) * ML_V], ones], axis=1)
            update = update + _dot(wkt, v_aug)
        dec = jnp.where(first, decay[2 * pr:2 * pr + 1, :], decay[2 * pr + 1:2 * pr + 2, :])
        c_ref[pr] = c_in[pr] * jnp.concatenate([dec, dec], axis=1) + update
    return jnp.concatenate(outs, axis=1).astype(BF16)


def _mlstm_kernel(kf_ref, ktf_ref, qtf_ref, vf_ref, gf_ref, gtf_ref,
                  kb_ref, ktb_ref, qtb_ref, vb_ref, gb_ref, gtb_ref,
                  hf_ref, hb_ref, cf_ref, cb_ref, mf_ref, mb_ref):
    @pl.when(pl.program_id(1) == 0)
    def _():
        for ref in (cf_ref, cb_ref, mf_ref, mb_ref):
            ref[...] = jnp.zeros(ref.shape, F32)

    h = ML_N_HEADS
    chunks = [(slice(c * SCAN_CHUNK, (c + 1) * SCAN_CHUNK),
               slice((SCAN_STEP_CHUNKS - 1 - c) * SCAN_CHUNK, (SCAN_STEP_CHUNKS - c) * SCAN_CHUNK))
              for c in range(SCAN_STEP_CHUNKS)]
    stats = [(_mlstm_stats(gf_ref[fs, :], gtf_ref[:, fs], mf_ref, i_off=0, f_off=2 * h, reverse=False),
              _mlstm_stats(gb_ref[bs, :], gtb_ref[:, bs], mb_ref, i_off=h, f_off=3 * h, reverse=True))
             for fs, bs in chunks]
    for (fs, bs), (stats_f, stats_b) in zip(chunks, stats):
        hf_ref[fs, :] = _mlstm_main(kf_ref[fs, :], ktf_ref[:, fs], qtf_ref[:, fs], vf_ref[fs, :], stats_f, cf_ref)
        hb_ref[bs, :] = _mlstm_main(kb_ref[bs, :], ktb_ref[:, bs], qtb_ref[:, bs], vb_ref[bs, :], stats_b, cb_ref)


def _mlstm(qt, k, kt, v, gates, gatest):
    b, s, n_qk = k.shape
    n_v = v.shape[2]
    n_g = gatest.shape[1]
    L = SCAN_CHUNK * SCAN_STEP_CHUNKS
    nc = s // L
    fwd = lambda w: pl.BlockSpec((None, L, w), lambda bi, i: (bi, i, 0))
    bwd = lambda w: pl.BlockSpec((None, L, w), lambda bi, i: (bi, nc - 1 - i, 0))
    fwd_t = lambda c: pl.BlockSpec((None, c, L), lambda bi, i: (bi, 0, i))
    bwd_t = lambda c: pl.BlockSpec((None, c, L), lambda bi, i: (bi, 0, nc - 1 - i))
    state = pltpu.VMEM((ML_N_HEADS // 2, 2 * ML_QK, 2 * ML_V), F32)
    stab = pltpu.VMEM((ML_N_HEADS, LANES), F32)
    return pl.pallas_call(
        _mlstm_kernel,
        out_shape=[jax.ShapeDtypeStruct((b, s, n_v), BF16)] * 2,
        grid=(b, nc),
        in_specs=[fwd(n_qk), fwd_t(n_qk), fwd_t(n_qk), fwd(n_v), fwd(LANES), fwd_t(n_g),
                  bwd(n_qk), bwd_t(n_qk), bwd_t(n_qk), bwd(n_v), bwd(LANES), bwd_t(n_g)],
        out_specs=[fwd(n_v), bwd(n_v)],
        scratch_shapes=[state, state, stab, stab],
        compiler_params=_params(("parallel", "arbitrary")),
        name="mlstm_scan",
    )(k, kt, qt, v, gates, gatest, k, kt, qt, v, gates, gatest)


def _odd_out_kernel(x_ref, hf_ref, hb_ref, o_ref, g_ref, w_ref, fg_ref, fwin_ref, fwout_ref, out_ref):
    hs = hf_ref[...].astype(F32) + hb_ref[...].astype(F32)
    normed = []
    for hd in range(ML_N_HEADS):
        seg = hs[:, hd * ML_V:(hd + 1) * ML_V]
        normed.append(seg * lax.rsqrt(jnp.mean(seg * seg, axis=-1, keepdims=True) + EPS))
    gated = (_sigmoid(o_ref[...].astype(F32)) * (jnp.concatenate(normed, axis=1) * g_ref[...])).astype(BF16)
    x = x_ref[...] + _dot(gated, w_ref[...])
    out_ref[...] = _ffn_apply(x, fg_ref, fwin_ref, fwout_ref)


def _odd_out(x, hf, hb, o, g, w, ffn, layer):
    t, d = x.shape
    n_v = hf.shape[1]
    tm = TOKEN_TILE
    tok = lambda c: pl.BlockSpec((tm, c), lambda i: (i, 0))
    return pl.pallas_call(
        _odd_out_kernel,
        out_shape=jax.ShapeDtypeStruct((t, d), F32),
        grid=(t // tm,),
        in_specs=[tok(d), tok(n_v), tok(n_v), tok(n_v), _const_spec(g.shape), _const_spec(w.shape)]
        + _layer_specs(ffn, layer),
        out_specs=tok(d),
        compiler_params=_params(("parallel",)),
        name="odd_out_ffn",
    )(x, hf, hb, o, g, w, *ffn)


def _row(v):
    return v.reshape(1, -1).astype(F32)


def _pad_lanes(v, width=LANES):
    return jnp.pad(v, [(0, 0)] * (v.ndim - 1) + [(0, width - v.shape[-1])])


def _swap_halves(v):
    half = v.shape[-1] // 2
    return jnp.concatenate([v[..., half:], v[..., :half]], axis=-1)


def _on_rope_lanes(v):
    return jnp.pad(v, [(0, 0)] * (v.ndim - 1) + [(MLA_NOPE, LANES - MLA_NOPE - v.shape[-1])])


def _prep_even(ev_w_in, ev_w_out, conv_w, conv_b, dt_bias, a_log, d_skip, ssd_norm,
               q_a_norm, w_q_b, kv_a_norm, w_kv_b, q_norm, k_norm):
    d_inner = ssd_norm.shape[0]
    d_xbc = conv_b.shape[0]
    n_heads = a_log.shape[1]
    c0, c1 = d_inner, d_inner + d_xbc
    c2 = c1 + 2 * n_heads
    c3 = c2 + MLA_Q_RANK
    c4 = c3 + MLA_KV_RANK
    w_kpe = ev_w_in[:, c4:]
    small = jnp.concatenate([_pad_lanes(ev_w_in[:, c1:c2], MLA_NOPE), _pad_lanes(w_kpe, LANES - MLA_NOPE)], axis=1)
    w_in = jnp.concatenate([ev_w_in[:, :c1], ev_w_in[:, c2:c4], small,
                            _on_rope_lanes(_swap_halves(w_kpe))], axis=1).astype(BF16)
    d_qk = MLA_NOPE + MLA_ROPE
    wq3 = w_q_b.reshape(MLA_Q_RANK, MLA_N_HEADS, d_qk)
    wq = jnp.concatenate(
        [_pad_lanes(wq3).reshape(MLA_Q_RANK, -1),
         _on_rope_lanes(_swap_halves(wq3[:, :, MLA_NOPE:])).reshape(MLA_Q_RANK, -1)], axis=1).astype(BF16)
    q_scale = d_qk ** -0.5 * LOG2_E
    score_bound = (SCORE_ROUNDING_MARGIN * d_qk * q_scale
                   * jnp.max(jnp.abs(q_norm)) * jnp.max(jnp.abs(k_norm)))
    scores_bounded = (score_bound <= SAFE_LOG2_SCORE).astype(jnp.int32).reshape(1, 1)
    ones_blk = jnp.ones((LANES, LANES), BF16)
    zero_blk = jnp.zeros((LANES, LANES), BF16)
    pair_ones = jnp.concatenate([jnp.concatenate([ones_blk, zero_blk], axis=1),
                                 jnp.concatenate([zero_blk, ones_blk], axis=1)], axis=0)
    wkv = w_kv_b.reshape(MLA_KV_RANK, MLA_N_HEADS, MLA_NOPE + MLA_V)
    wkvb = jnp.concatenate(
        [_pad_lanes(wkv[:, :, :MLA_NOPE]).reshape(MLA_KV_RANK, -1),
         wkv[:, :, MLA_NOPE:].reshape(MLA_KV_RANK, -1)], axis=1).astype(BF16)
    head_of_lane = jnp.arange(d_inner) // SSD_HEAD_DIM
    e_f = (jnp.arange(LANES)[:, None] == head_of_lane[None, :]).astype(BF16)
    e_b = (jnp.arange(LANES)[:, None] == head_of_lane[None, :] + n_heads).astype(BF16)
    return dict(
        d_inner=d_inner, d_xbc=d_xbc, w_in=w_in, w_out=ev_w_out.astype(BF16),
        conv_w=_pad_lanes(conv_w.T, SUBLANES).T.astype(F32), conv_b=_row(conv_b),
        dt_bias=_pad_lanes(_row(dt_bias)), alog_row=_pad_lanes(_row(a_log)),
        alog_col=a_log.reshape(-1, 1).astype(F32), d_skip=_row(jnp.repeat(d_skip, SSD_HEAD_DIM)),
        ssd_norm=_row(ssd_norm), e_f=e_f, e_b=e_b,
        q_a_norm=_row(q_a_norm), wq=wq, kv_a_norm=_row(kv_a_norm), wkvb=wkvb, pair_ones=pair_ones,
        scores_bounded=scores_bounded,
        q_gain_cos=_pad_lanes(_row(q_norm)) * q_scale,
        q_gain_sin=_on_rope_lanes(_swap_halves(_row(q_norm)[:, MLA_NOPE:])) * q_scale,
        k_gain_cos=_pad_lanes(_row(k_norm)),
        k_gain_sin=_on_rope_lanes(_swap_halves(_row(k_norm)[:, MLA_NOPE:])))


def _rope_tables(s):
    pos = jnp.arange(s, dtype=F32)
    inv_freq = jnp.power(ROPE_BASE, -jnp.arange(0, MLA_ROPE, 2, dtype=F32) / MLA_ROPE)
    freqs = pos[:, None] * inv_freq[None, :]
    cos, sin = jnp.cos(freqs), jnp.sin(freqs)
    tail = jnp.ones((s, LANES - MLA_NOPE - MLA_ROPE), F32)
    cos_t = jnp.concatenate([jnp.ones((s, MLA_NOPE), F32), cos, cos, tail], axis=1)
    sin_t = _on_rope_lanes(jnp.concatenate([-sin, sin], axis=1))
    return cos_t, sin_t


def _even_mixer(x, norm_g, p, ffn2, layer):
    b, s, d = x.shape
    z, xbc, dt, dtt, q, k, vt = _even_in(x, norm_g, p, *_rope_tables(s))
    yf, yb = _ssd(xbc, dt, dtt, p["alog_row"], p["alog_col"], p["d_skip"], p["e_f"], p["e_b"],
                  d_inner=p["d_inner"])
    o = _attention(p["scores_bounded"], q, k, vt)
    flat = lambda a: a.reshape(b * s, a.shape[-1])
    return _even_out(flat(x), flat(yf), flat(yb), flat(z), flat(o), p["ssd_norm"],
                     p["w_out"], ffn2, layer).reshape(b, s, d)


def _prep_odd(od_w_in, od_w_out, ig_bias, fg_bias, ml_norm):
    n_qk = ML_N_HEADS * ML_QK
    n_main = 2 * n_qk + 2 * ML_N_HEADS * ML_V
    w_in = jnp.concatenate([od_w_in[:, :n_qk] * ML_QK ** -0.5, od_w_in[:, n_qk:n_main],
                            _pad_lanes(od_w_in[:, n_main:])], axis=1).astype(BF16)
    gate_bias = _pad_lanes(_row(jnp.concatenate([ig_bias.reshape(-1), fg_bias.reshape(-1)])))
    return dict(w_in=w_in, w_out=od_w_out.astype(BF16), gate_bias=gate_bias, ml_norm=_row(ml_norm))


def _odd_mixer(x, norm_g, p, ffn2, layer):
    b, s, d = x.shape
    qt, k, kt, v, o, gates, gatest = _odd_in(x, norm_g, p["w_in"], p["gate_bias"])
    hf, hb = _mlstm(qt, k, kt, v, gates, gatest)
    flat = lambda a: a.reshape(b * s, a.shape[-1])
    return _odd_out(flat(x), flat(hf), flat(hb), flat(o), p["ml_norm"], p["w_out"], ffn2, layer).reshape(b, s, d)


def kernel(x_prompt, x_sample, ffn1_norm, ffn1_w_in, ffn1_w_out, mix_norm, ffn2_norm, ffn2_w_in, ffn2_w_out,
           ev_w_in, ev_w_out, ssd_conv_w, ssd_conv_b, ssd_dt_bias, ssd_a_log, ssd_d_skip, ssd_norm,
           mla_q_a_norm, mla_w_q_b, mla_kv_a_norm, mla_w_kv_b, mla_q_norm, mla_k_norm,
           od_w_in, od_w_out, ml_ig_bias, ml_fg_bias, ml_norm):
    depth = ffn1_norm.shape[0]
    layers = []
    for layer in range(depth):
        j = layer // 2
        if layer % 2 == 0:
            mixer = functools.partial(_even_mixer, p=_prep_even(
                ev_w_in[j], ev_w_out[j], ssd_conv_w[j], ssd_conv_b[j], ssd_dt_bias[j], ssd_a_log[j],
                ssd_d_skip[j], ssd_norm[j], mla_q_a_norm[j], mla_w_q_b[j], mla_kv_a_norm[j],
                mla_w_kv_b[j], mla_q_norm[j], mla_k_norm[j]))
        else:
            mixer = functools.partial(_odd_mixer, p=_prep_odd(
                od_w_in[j], od_w_out[j], ml_ig_bias[j], ml_fg_bias[j], ml_norm[j]))
        layers.append(dict(mixer=mixer, mix_norm=_row(mix_norm[layer])))
    ffn1 = (ffn1_norm[:, None, :].astype(F32), ffn1_w_in.astype(BF16), ffn1_w_out.astype(BF16))
    ffn2 = (ffn2_norm[:, None, :].astype(F32), ffn2_w_in.astype(BF16), ffn2_w_out.astype(BF16))

    def trunk(x):
        b, s, d = x.shape
        for layer, lp in enumerate(layers):
            x = _ffn(x.reshape(b * s, d), ffn1, layer).reshape(b, s, d)
            x = lp["mixer"](x, lp["mix_norm"], ffn2=ffn2, layer=layer)
        return x

    return trunk(x_prompt), trunk(x_sample)
```

```python
import functools

import jax
import jax.numpy as jnp
from jax import lax
from jax.experimental import pallas as pl
from jax.experimental.pallas import tpu as pltpu

F32 = jnp.float32
BF16 = jnp.bfloat16
EPS = 1e-6

LANES = 128
SUBLANES = 8
BF16_SUBLANES = 16
VMEM_LIMIT = 56 * 1024 * 1024

SSD_HEAD_DIM = 64
SSD_N_GROUPS = 2
SSD_D_STATE = 128
SSD_CONV = 5
MLA_N_HEADS = 8
MLA_Q_RANK = 384
MLA_KV_RANK = 256
MLA_NOPE = 64
MLA_ROPE = 32
MLA_V = 64
MLA_VT_ROWS = MLA_V + BF16_SUBLANES
ROPE_BASE = 10000.0
LOG2_E = 1.4426950408889634
SAFE_LOG2_SCORE = 60.0
SCORE_ROUNDING_MARGIN = 1.02
ML_N_HEADS = 8
ML_QK = 64
ML_V = 128

TOKEN_TILE = 512
FFN_TOKEN_TILE = 1024
SCAN_CHUNK = 128
SCAN_STEP_CHUNKS = 4
ATTN_Q_TILE = 512
HALO = SUBLANES


def _params(semantics):
    return pltpu.CompilerParams(dimension_semantics=semantics, vmem_limit_bytes=VMEM_LIMIT)


def _const_spec(shape):
    nd = len(shape)
    return pl.BlockSpec(shape, lambda *_: (0,) * nd, pipeline_mode=pl.Buffered(1))


def _rms(x, g):
    ms = jnp.mean(x * x, axis=-1, keepdims=True)
    return x * lax.rsqrt(ms + EPS) * g


def _sigmoid(x):
    return 1.0 / (1.0 + jnp.exp(-x))


def _softplus(x):
    return jnp.maximum(x, 0.0) + jnp.log1p(jnp.exp(-jnp.abs(x)))


def _dot(a, b):
    return jnp.dot(a, b, preferred_element_type=F32)


def _dot_nt(a, b):
    return lax.dot_general(a, b, (((1,), (1,)), ((), ())), preferred_element_type=F32)


def _dot_tn(a, b):
    return lax.dot_general(a, b, (((0,), (0,)), ((), ())), preferred_element_type=F32)


def _split_terms(x, terms):
    out = []
    for _ in range(terms):
        piece = x.astype(BF16)
        out.append(piece)
        x = x - piece.astype(F32)
    return out


def _scan_masks(n, reverse):
    row = lax.broadcasted_iota(jnp.int32, (n, n), 0)
    col = lax.broadcasted_iota(jnp.int32, (n, n), 1)
    if reverse:
        return col >= row, col <= row
    return col <= row, col >= row


def _ffn_apply(x, g_ref, win_ref, wout_ref):
    d_ff = wout_ref.shape[0]
    h = _rms(x, g_ref[...]).astype(BF16)
    hw = _dot(h, win_ref[...])
    gate = hw[:, :d_ff]
    up = hw[:, d_ff:]
    a = (gate * _sigmoid(gate) * up).astype(BF16)
    return x + 0.5 * _dot(a, wout_ref[...])


def _ffn_kernel(x_ref, g_ref, win_ref, wout_ref, o_ref):
    o_ref[...] = _ffn_apply(x_ref[...], g_ref, win_ref, wout_ref)


def _layer_specs(ffn, layer):
    return [pl.BlockSpec((None,) + a.shape[1:], lambda *_, n=a.ndim: (layer,) + (0,) * (n - 1),
                         pipeline_mode=pl.Buffered(1)) for a in ffn]


def _ffn(x, ffn, layer):
    t, d = x.shape
    tm = FFN_TOKEN_TILE
    return pl.pallas_call(
        _ffn_kernel,
        out_shape=jax.ShapeDtypeStruct((t, d), F32),
        grid=(t // tm,),
        in_specs=[pl.BlockSpec((tm, d), lambda i: (i, 0))] + _layer_specs(ffn, layer),
        out_specs=pl.BlockSpec((tm, d), lambda i: (i, 0)),
        compiler_params=_params(("parallel",)),
        name="ffn",
    )(x, *ffn)


def _conv_silu(ext, w_ref, b_ref):
    n = ext.shape[0]
    tc = n - 2 * HALO
    pad = SSD_CONV // 2
    acc = b_ref[...] + w_ref[pad:pad + 1, :] * ext[HALO:HALO + tc, :]
    for k in range(SSD_CONV):
        if k != pad:
            acc = acc + w_ref[k:k + 1, :] * pltpu.roll(ext, (pad - k) % n, axis=0)[HALO:HALO + tc, :]
    return acc * _sigmoid(acc)


def _even_in_kernel(x_ref, xp_ref, xn_ref, g_ref, w_ref, dtb_ref, qan_ref, wq_ref, kvan_ref, wkvb_ref,
                    qgc_ref, qgs_ref, kgc_ref, kgs_ref, ones_ref, cw_ref, cb_ref, cos_ref, sin_ref,
                    z_ref, xbc_ref, dt_ref, dtt_ref, q_ref, k_ref, vt_ref,
                    *, d_inner, d_xbc, n_tiles):
    i = pl.program_id(1)
    tm = x_ref.shape[0]
    xe = jnp.concatenate([xp_ref[...], x_ref[...], xn_ref[...]], axis=0)
    hwe = _dot(_rms(xe, g_ref[...]).astype(BF16), w_ref[...])
    hw = hwe[HALO:HALO + tm, :]
    c0 = d_inner
    c1 = c0 + d_xbc
    c2 = c1 + MLA_Q_RANK
    c3 = c2 + MLA_KV_RANK
    z_ref[...] = hw[:, :c0].astype(BF16)
    zero = jnp.zeros((HALO, d_xbc), F32)
    ext = jnp.concatenate([jnp.where(i > 0, hwe[:HALO, c0:c1], zero), hw[:, c0:c1],
                           jnp.where(i < n_tiles - 1, hwe[HALO + tm:, c0:c1], zero)], axis=0)
    xbc_ref[...] = _conv_silu(ext, cw_ref, cb_ref)
    small = hw[:, c3:c3 + LANES]
    kpe_sw = hw[:, c3 + LANES:c3 + 2 * LANES]
    dt = _softplus(small + dtb_ref[...])
    dt_ref[...] = dt
    dtt_ref[...] = dt.T[:dtt_ref.shape[0], :]

    n_k = MLA_N_HEADS * LANES
    qa = _rms(hw[:, c1:c2], qan_ref[...]).astype(BF16)
    qq = _dot(qa, wq_ref[...])
    kva = _rms(hw[:, c2:c3], kvan_ref[...]).astype(BF16)
    kv = _dot(kva, wkvb_ref[...])
    v_t = kv[:, n_k:].T
    ones_rows = jnp.ones((MLA_VT_ROWS - MLA_V, v_t.shape[1]), F32)
    vt_ref[...] = jnp.concatenate(
        [blk for hd in range(MLA_N_HEADS) for blk in (v_t[hd * MLA_V:(hd + 1) * MLA_V, :], ones_rows)],
        axis=0).astype(BF16)

    lane = lax.broadcasted_iota(jnp.int32, (1, LANES), 1)
    pe_lanes = (lane >= MLA_NOPE) & (lane < MLA_NOPE + MLA_ROPE)
    kpe = jnp.where(pe_lanes, small, 0.0)
    cos_t, sin_t = cos_ref[...], sin_ref[...]
    q_cos, q_sin = cos_t * qgc_ref[...], sin_t * qgs_ref[...]
    k_cos, k_sin = cos_t * kgc_ref[...], sin_t * kgs_ref[...]
    k_rot = kpe_sw * k_sin
    inv_dim = 1.0 / (MLA_NOPE + MLA_ROPE)

    q_blk = [qq[:, hd * LANES:(hd + 1) * LANES] for hd in range(MLA_N_HEADS)]
    k_blk = [kv[:, hd * LANES:(hd + 1) * LANES] + kpe for hd in range(MLA_N_HEADS)]
    q_ss, k_ss = [], []
    for pr in range(MLA_N_HEADS // 2):
        for blk, out in ((q_blk, q_ss), (k_blk, k_ss)):
            sq = jnp.concatenate([blk[2 * pr] * blk[2 * pr], blk[2 * pr + 1] * blk[2 * pr + 1]], axis=1)
            ss = _dot(sq.astype(BF16), ones_ref[...])
            out.extend([ss[:, :LANES], ss[:, LANES:]])
    for hd in range(MLA_N_HEADS):
        sl = slice(hd * LANES, (hd + 1) * LANES)
        q_rot = qq[:, n_k + hd * LANES:n_k + (hd + 1) * LANES]
        rq = lax.rsqrt(q_ss[hd] * inv_dim + EPS)
        q_ref[:, sl] = ((q_blk[hd] * q_cos + q_rot * q_sin) * rq).astype(BF16)
        rk = lax.rsqrt(k_ss[hd] * inv_dim + EPS)
        k_ref[:, sl] = ((k_blk[hd] * k_cos + k_rot) * rk).astype(BF16)


def _even_in(x, g, p, cos_t, sin_t):
    b, s, d = x.shape
    tm = TOKEN_TILE
    d_inner, d_xbc = p["d_inner"], p["d_xbc"]
    n_q = MLA_N_HEADS * LANES
    n_v = MLA_N_HEADS * MLA_VT_ROWS
    n_dt = 2 * (d_inner // SSD_HEAD_DIM)
    tok = lambda c: pl.BlockSpec((None, tm, c), lambda bi, i: (bi, i, 0))
    tab = pl.BlockSpec((tm, LANES), lambda bi, i: (i, 0))
    consts = [g, p["w_in"], p["dt_bias"], p["q_a_norm"], p["wq"], p["kv_a_norm"], p["wkvb"],
              p["q_gain_cos"], p["q_gain_sin"], p["k_gain_cos"], p["k_gain_sin"], p["pair_ones"],
              p["conv_w"], p["conv_b"]]
    n_tiles = s // tm
    per = tm // HALO
    n_halo = s // HALO
    halo_prev = pl.BlockSpec((None, HALO, d), lambda bi, i: (bi, jnp.maximum(i * per - 1, 0), 0))
    halo_next = pl.BlockSpec((None, HALO, d), lambda bi, i: (bi, jnp.minimum((i + 1) * per, n_halo - 1), 0))
    return pl.pallas_call(
        functools.partial(_even_in_kernel, d_inner=d_inner, d_xbc=d_xbc, n_tiles=n_tiles),
        out_shape=[
            jax.ShapeDtypeStruct((b, s, d_inner), BF16),
            jax.ShapeDtypeStruct((b, s, d_xbc), F32),
            jax.ShapeDtypeStruct((b, s, LANES), F32),
            jax.ShapeDtypeStruct((b, n_dt, s), F32),
            jax.ShapeDtypeStruct((b, s, n_q), BF16),
            jax.ShapeDtypeStruct((b, s, n_q), BF16),
            jax.ShapeDtypeStruct((b, n_v, s), BF16),
        ],
        grid=(b, s // tm),
        in_specs=[tok(d), halo_prev, halo_next] + [_const_spec(c.shape) for c in consts] + [tab, tab],
        out_specs=[
            tok(d_inner), tok(d_xbc), tok(LANES),
            pl.BlockSpec((None, n_dt, tm), lambda bi, i: (bi, 0, i)),
            tok(n_q), tok(n_q),
            pl.BlockSpec((None, n_v, tm), lambda bi, i: (bi, 0, i)),
        ],
        compiler_params=_params(("parallel", "parallel")),
        name="even_in",
    )(x, x, x, *consts, cos_t, sin_t)


def _ssd_stats(dt, dtt, a_row, a_col, expand, *, reverse):
    L = dt.shape[0]
    keep, keep_t = _scan_masks(L, reverse)
    last = 0 if reverse else L - 1
    keep_bf = keep.astype(F32).astype(BF16)
    keep_t_bf = keep_t.astype(F32).astype(BF16)
    cum = sum(_dot(keep_bf, t) for t in _split_terms(dt * a_row, 3))
    cumt = sum(_dot(t, keep_t_bf) for t in _split_terms(dtt * a_col, 3))
    cum_last = cum[last:last + 1, :]
    narrow = jnp.concatenate(
        [jnp.exp(cum), dt * jnp.exp(cum_last - cum),
         jnp.broadcast_to(jnp.exp(cum_last), (SUBLANES, LANES))], axis=0)
    wide = _dot(narrow.astype(BF16), expand)
    return dict(cum=cum, cumt=cumt, keep=keep, e_off=wide[:L], w_state=wide[L:2 * L],
                chunk_decay=wide[2 * L:2 * L + 1])


def _ssd_main(xbc, dtt, stats, s_ref, d_skip, *, lane_off, d_inner):
    n_heads = d_inner // SSD_HEAD_DIM
    hpg = n_heads // SSD_N_GROUPS
    gw = hpg * SSD_HEAD_DIM
    gn = SSD_D_STATE
    xs = xbc[:, :d_inner]
    cum, cumt, keep = stats["cum"], stats["cumt"], stats["keep"]
    e_off, w_state, chunk_decay = stats["e_off"], stats["w_state"], stats["chunk_decay"]
    lane = lax.broadcasted_iota(jnp.int32, (1, LANES), 1)
    lo_lanes = lane < SSD_HEAD_DIM
    parts = []
    for g in range(SSD_N_GROUPS):
        gsl = slice(g * gw, (g + 1) * gw)
        b_g = xbc[:, d_inner + g * gn:d_inner + (g + 1) * gn]
        c_off = d_inner + SSD_N_GROUPS * gn
        c_bf = xbc[:, c_off + g * gn:c_off + (g + 1) * gn].astype(BF16)
        cb = _dot_nt(c_bf, b_g.astype(BF16))
        state = s_ref[g]
        y_off = _dot(c_bf, state.astype(BF16)) * e_off[:, gsl]
        xw = (xs[:, gsl] * w_state[:, gsl]).astype(BF16)
        s_ref[g] = state * chunk_decay[:, gsl] + _dot(b_g.T.astype(BF16), xw)
        for pr in range(hpg // 2):
            m_pair = []
            for j in range(2):
                hd = g * hpg + 2 * pr + j
                seg = cum[:, lane_off + hd:lane_off + hd + 1] - cumt[hd:hd + 1, :]
                m_h = jnp.where(keep, cb * jnp.exp(seg) * dtt[hd:hd + 1, :], 0.0)
                m_pair.append(m_h.astype(BF16))
            psl = slice(g * gw + pr * LANES, g * gw + (pr + 1) * LANES)
            x_pair = xs[:, psl]
            rhs = jnp.concatenate([jnp.where(lo_lanes, x_pair, 0.0),
                                   jnp.where(lo_lanes, 0.0, x_pair)], axis=0).astype(BF16)
            y_pair = _dot(jnp.concatenate(m_pair, axis=1), rhs) + y_off[:, pr * LANES:(pr + 1) * LANES]
            if d_skip is not None:
                y_pair = y_pair + x_pair * d_skip[:, psl]
            parts.append(y_pair)
    return jnp.concatenate(parts, axis=1).astype(BF16)


def _ssd_kernel(xf_ref, xb_ref, dtf_ref, dtb_ref, dttf_ref, dttb_ref, alog_row_ref, alog_col_ref,
                dskip_ref, ef_ref, eb_ref, yf_ref, yb_ref, sf_ref, sb_ref, *, d_inner):
    @pl.when(pl.program_id(1) == 0)
    def _():
        sf_ref[...] = jnp.zeros(sf_ref.shape, F32)
        sb_ref[...] = jnp.zeros(sb_ref.shape, F32)

    n_heads = d_inner // SSD_HEAD_DIM
    lane = lax.broadcasted_iota(jnp.int32, (1, LANES), 1)
    a_all = -jnp.exp(alog_row_ref[...])
    a_col = -jnp.exp(alog_col_ref[...])
    a_f = jnp.where(lane < n_heads, a_all, 0.0)
    a_b = jnp.where((lane >= n_heads) & (lane < 2 * n_heads), a_all, 0.0)
    chunks = [(slice(c * SCAN_CHUNK, (c + 1) * SCAN_CHUNK),
               slice((SCAN_STEP_CHUNKS - 1 - c) * SCAN_CHUNK, (SCAN_STEP_CHUNKS - c) * SCAN_CHUNK))
              for c in range(SCAN_STEP_CHUNKS)]
    stats = [(_ssd_stats(dtf_ref[fs, :], dttf_ref[0:n_heads, fs], a_f, a_col[0:n_heads, :], ef_ref[...],
                         reverse=False),
              _ssd_stats(dtb_ref[bs, :], dttb_ref[n_heads:2 * n_heads, bs], a_b, a_col[n_heads:2 * n_heads, :],
                         eb_ref[...], reverse=True)) for fs, bs in chunks]
    for (fs, bs), (stats_f, stats_b) in zip(chunks, stats):
        yf_ref[fs, :] = _ssd_main(xf_ref[fs, :], dttf_ref[0:n_heads, fs], stats_f, sf_ref, dskip_ref[...],
                                  lane_off=0, d_inner=d_inner)
        yb_ref[bs, :] = _ssd_main(xb_ref[bs, :], dttb_ref[n_heads:2 * n_heads, bs], stats_b, sb_ref, None,
                                  lane_off=n_heads, d_inner=d_inner)


def _ssd(xbc, dt, dtt, alog_row, alog_col, d_skip, e_f, e_b, *, d_inner):
    b, s, c = xbc.shape
    L = SCAN_CHUNK * SCAN_STEP_CHUNKS
    nc = s // L
    n_dt = dtt.shape[1]
    hpg_w = d_inner // SSD_N_GROUPS
    fwd = lambda w: pl.BlockSpec((None, L, w), lambda bi, i: (bi, i, 0))
    bwd = lambda w: pl.BlockSpec((None, L, w), lambda bi, i: (bi, nc - 1 - i, 0))
    return pl.pallas_call(
        functools.partial(_ssd_kernel, d_inner=d_inner),
        out_shape=[jax.ShapeDtypeStruct((b, s, d_inner), BF16)] * 2,
        grid=(b, nc),
        in_specs=[
            fwd(c), bwd(c), fwd(LANES), bwd(LANES),
            pl.BlockSpec((None, n_dt, L), lambda bi, i: (bi, 0, i)),
            pl.BlockSpec((None, n_dt, L), lambda bi, i: (bi, 0, nc - 1 - i)),
            _const_spec(alog_row.shape), _const_spec(alog_col.shape), _const_spec(d_skip.shape),
            _const_spec(e_f.shape), _const_spec(e_b.shape),
        ],
        out_specs=[fwd(d_inner), bwd(d_inner)],
        scratch_shapes=[pltpu.VMEM((SSD_N_GROUPS, SSD_D_STATE, hpg_w), F32)] * 2,
        compiler_params=_params(("parallel", "arbitrary")),
        name="ssd_scan",
    )(xbc, xbc, dt, dt, dtt, dtt, alog_row, alog_col, d_skip, e_f, e_b)


def _attn_kernel(bounded_ref, q_ref, k_ref, vt_ref, o_ref):
    def scores(hd):
        sl = slice(hd * LANES, (hd + 1) * LANES)
        return _dot_nt(k_ref[:, sl], q_ref[:, sl])

    def attend(subtract_max):
        outs = []
        st_next = scores(0)
        for hd in range(MLA_N_HEADS):
            st = st_next
            if hd + 1 < MLA_N_HEADS:
                st_next = scores(hd + 1)
            if subtract_max:
                st = st - jnp.max(st, axis=0, keepdims=True)
            p = jnp.exp2(st).astype(BF16)
            acc = _dot(vt_ref[hd * MLA_VT_ROWS:(hd + 1) * MLA_VT_ROWS, :], p)
            outs.append(acc[:MLA_V] / acc[MLA_V:MLA_V + 1])
        o_ref[...] = jnp.concatenate(outs, axis=0).T.astype(BF16)

    @pl.when(bounded_ref[0, 0] != 0)
    def _():
        attend(subtract_max=False)

    @pl.when(bounded_ref[0, 0] == 0)
    def _():
        attend(subtract_max=True)


def _attention(bounded, q, k, vt):
    b, s, n_q = q.shape
    n_vt = vt.shape[1]
    n_v = MLA_N_HEADS * MLA_V
    tq = ATTN_Q_TILE
    return pl.pallas_call(
        _attn_kernel,
        out_shape=jax.ShapeDtypeStruct((b, s, n_v), BF16),
        grid=(b, s // tq),
        in_specs=[
            pl.BlockSpec(memory_space=pltpu.SMEM),
            pl.BlockSpec((None, tq, n_q), lambda bi, i: (bi, i, 0)),
            pl.BlockSpec((None, s, n_q), lambda bi, i: (bi, 0, 0)),
            pl.BlockSpec((None, n_vt, s), lambda bi, i: (bi, 0, 0)),
        ],
        out_specs=pl.BlockSpec((None, tq, n_v), lambda bi, i: (bi, i, 0)),
        compiler_params=_params(("parallel", "arbitrary")),
        name="mla_attention",
    )(bounded, q, k, vt)


def _even_out_kernel(x_ref, yf_ref, yb_ref, z_ref, o_ref, g_ref, w_ref, fg_ref, fwin_ref, fwout_ref,
                     out_ref, *, d_inner):
    z = z_ref[...].astype(F32)
    y = (yf_ref[...].astype(F32) + yb_ref[...].astype(F32)) * (z * _sigmoid(z))
    gw = d_inner // SSD_N_GROUPS
    normed = []
    for g in range(SSD_N_GROUPS):
        seg = y[:, g * gw:(g + 1) * gw]
        normed.append(seg * lax.rsqrt(jnp.mean(seg * seg, axis=-1, keepdims=True) + EPS))
    yn = (jnp.concatenate(normed, axis=1) * g_ref[...]).astype(BF16)
    x = x_ref[...] + _dot(yn, w_ref[0:d_inner, :]) + _dot(o_ref[...], w_ref[d_inner:, :])
    out_ref[...] = _ffn_apply(x, fg_ref, fwin_ref, fwout_ref)


def _even_out(x, yf, yb, z, o, g, w, ffn, layer):
    t, d = x.shape
    d_inner = yf.shape[1]
    tm = TOKEN_TILE
    tok = lambda c: pl.BlockSpec((tm, c), lambda i: (i, 0))
    return pl.pallas_call(
        functools.partial(_even_out_kernel, d_inner=d_inner),
        out_shape=jax.ShapeDtypeStruct((t, d), F32),
        grid=(t // tm,),
        in_specs=[tok(d), tok(d_inner), tok(d_inner), tok(d_inner), tok(o.shape[1]),
                  _const_spec(g.shape), _const_spec(w.shape)] + _layer_specs(ffn, layer),
        out_specs=tok(d),
        compiler_params=_params(("parallel",)),
        name="even_out_ffn",
    )(x, yf, yb, z, o, g, w, *ffn)


def _odd_in_kernel(x_ref, g_ref, w_ref, gb_ref, qt_ref, k_ref, kt_ref, v_ref, o_ref, gates_ref, gatest_ref):
    h = _rms(x_ref[...], g_ref[...]).astype(BF16)
    hw = _dot(h, w_ref[...])
    n_qk = ML_N_HEADS * ML_QK
    n_v = ML_N_HEADS * ML_V
    qt_ref[...] = hw[:, :n_qk].T.astype(BF16)
    k = hw[:, n_qk:2 * n_qk]
    k_ref[...] = k.astype(BF16)
    kt_ref[...] = k.T.astype(BF16)
    v_ref[...] = hw[:, 2 * n_qk:2 * n_qk + n_v].astype(BF16)
    o_ref[...] = hw[:, 2 * n_qk + n_v:2 * n_qk + 2 * n_v].astype(BF16)
    pre = hw[:, 2 * n_qk + 2 * n_v:] + gb_ref[...]
    lane = lax.broadcasted_iota(jnp.int32, (1, LANES), 1)
    gates = jnp.where(lane < 2 * ML_N_HEADS, pre, -_softplus(-pre))
    gates_ref[...] = gates
    gatest_ref[...] = gates.T[:gatest_ref.shape[0], :]


def _odd_in(x, g, w, gate_bias):
    b, s, d = x.shape
    tm = FFN_TOKEN_TILE
    n_qk = ML_N_HEADS * ML_QK
    n_v = ML_N_HEADS * ML_V
    n_g = 4 * ML_N_HEADS
    tok = lambda c: pl.BlockSpec((None, tm, c), lambda bi, i: (bi, i, 0))
    tok_t = lambda c: pl.BlockSpec((None, c, tm), lambda bi, i: (bi, 0, i))
    return pl.pallas_call(
        _odd_in_kernel,
        out_shape=[
            jax.ShapeDtypeStruct((b, n_qk, s), BF16),
            jax.ShapeDtypeStruct((b, s, n_qk), BF16),
            jax.ShapeDtypeStruct((b, n_qk, s), BF16),
            jax.ShapeDtypeStruct((b, s, n_v), BF16),
            jax.ShapeDtypeStruct((b, s, n_v), BF16),
            jax.ShapeDtypeStruct((b, s, LANES), F32),
            jax.ShapeDtypeStruct((b, n_g, s), F32),
        ],
        grid=(b, s // tm),
        in_specs=[tok(d), _const_spec(g.shape), _const_spec(w.shape), _const_spec(gate_bias.shape)],
        out_specs=[tok_t(n_qk), tok(n_qk), tok_t(n_qk), tok(n_v), tok(n_v), tok(LANES), tok_t(n_g)],
        compiler_params=_params(("parallel", "parallel")),
        name="odd_in",
    )(x, g, w, gate_bias)


def _cummax_lanes(u, reverse):
    n = u.shape[1]
    lane = lax.broadcasted_iota(jnp.int32, (1, n), 1)
    d = 1
    while d < n:
        if reverse:
            shifted, valid = pltpu.roll(u, n - d, axis=1), lane < n - d
        else:
            shifted, valid = pltpu.roll(u, d, axis=1), lane >= d
        u = jnp.where(valid, jnp.maximum(u, shifted), u)
        d *= 2
    return u


def _mlstm_stats(gates, gatest, m_ref, *, i_off, f_off, reverse):
    L = gates.shape[0]
    H = ML_N_HEADS
    assert L == LANES == ML_V and 2 * ML_QK == LANES
    keep, keep_t = _scan_masks(L, reverse)
    keep_bf = keep.astype(F32).astype(BF16)
    keep_t_bf = keep_t.astype(F32).astype(BF16)
    last = 0 if reverse else L - 1

    li_r = gatest[i_off:i_off + H, :]
    bcum_r = sum(_dot(t, keep_t_bf) for t in _split_terms(gatest[f_off:f_off + H, :], 3))
    u_r = li_r - bcum_r
    m_in = m_ref[...]
    mx = jnp.maximum(m_in, _cummax_lanes(u_r, reverse))
    w_inter = jnp.exp(m_in - mx)
    e_negm = jnp.exp(-(bcum_r + mx))
    b_end = jnp.broadcast_to(bcum_r[:, last:last + 1], (H, L))
    g_end = b_end + u_r
    m_new = jnp.maximum(b_end + m_in, jnp.max(g_end, axis=1, keepdims=True))
    w_src = jnp.exp(g_end - m_new)
    decay = jnp.exp(b_end + m_in - m_new)
    m_ref[...] = m_new

    bcum_c = sum(_dot(keep_bf, t) for t in _split_terms(gates, 3))
    u_all = jnp.concatenate(
        [jnp.broadcast_to(gates[:, i_off + hd:i_off + hd + 1] - bcum_c[:, f_off + hd:f_off + hd + 1], (L, LANES))
         for hd in range(H)], axis=1)
    return dict(mx=mx, w_inter=w_inter, e_negm=e_negm, w_src=w_src, decay=decay, u_all=u_all, keep_t=keep_t)


def _mlstm_main(k, kt, qt, v, stats, c_ref):
    L = k.shape[0]
    H = ML_N_HEADS
    mx, w_inter, e_negm, w_src = stats["mx"], stats["w_inter"], stats["e_negm"], stats["w_src"]
    decay, u_all, keep_t = stats["decay"], stats["u_all"], stats["keep_t"]
    srow = lax.broadcasted_iota(jnp.int32, (LANES, 1), 0)
    first = srow < ML_QK
    ones = jnp.ones((L, ML_V), BF16)
    c_in = [c_ref[pr] for pr in range(H // 2)]
    qt_own, kt_own, st = [], [], []
    for pr in range(H // 2):
        psl = slice(pr * LANES, (pr + 1) * LANES)
        zero = jnp.zeros((LANES, L), BF16)
        qt_own += [jnp.where(first, qt[psl, :], zero), jnp.where(first, zero, qt[psl, :])]
        kt_own += [jnp.where(first, kt[psl, :], zero), jnp.where(first, zero, kt[psl, :])]
        st.append(_dot(k[:, psl], jnp.concatenate(qt_own[-2:], axis=1)))
    lhs_t = []
    for hd in range(H):
        row = lambda a: a[hd:hd + 1, :]
        c_aug = c_in[hd // 2]
        w = jnp.where(keep_t, jnp.exp(u_all[:, hd * LANES:(hd + 1) * LANES] - row(mx)), 0.0)
        sw = st[hd // 2][:, (hd % 2) * L:(hd % 2 + 1) * L] * w
        q_f32 = qt_own[hd].astype(F32)
        nq = jnp.sum(c_aug[:, ML_V:] * q_f32, axis=0, keepdims=True)
        den = jnp.sum(sw, axis=0, keepdims=True) + row(w_inter) * nq
        inv = 1.0 / jnp.maximum(jnp.abs(den), row(e_negm))
        lhs_t.append(jnp.concatenate(
            [(sw * inv).astype(BF16), (q_f32 * (row(w_inter) * inv)).astype(BF16)], axis=0))
    outs = []
    for hd in range(H):
        v_h = v[:, hd * ML_V:(hd + 1) * ML_V]
        rhs = jnp.concatenate([v_h, c_in[hd // 2][:, :ML_V].astype(BF16)], axis=0)
        outs.append(_dot_tn(lhs_t[hd], rhs))
    for pr in range(H // 2):
        update = jnp.zeros(c_in[pr].shape, F32)
        for hd in (2 * pr, 2 * pr + 1):
            wkt = (kt_own[hd].astype(F32) * w_src[hd:hd + 1, :]).astype(BF16)
            v_aug = jnp.concatenate([v[:, hd * ML_V:(hd + 1) * ML_V], ones], axis=1)
            update = update + _dot(wkt, v_aug)
        dec = jnp.where(first, decay[2 * pr:2 * pr + 1, :], decay[2 * pr + 1:2 * pr + 2, :])
        c_ref[pr] = c_in[pr] * jnp.concatenate([dec, dec], axis=1) + update
    return jnp.concatenate(outs, axis=1).astype(BF16)


def _mlstm_kernel(kf_ref, ktf_ref, qtf_ref, vf_ref, gf_ref, gtf_ref,
                  kb_ref, ktb_ref, qtb_ref, vb_ref, gb_ref, gtb_ref,
                  hf_ref, hb_ref, cf_ref, cb_ref, mf_ref, mb_ref):
    @pl.when(pl.program_id(1) == 0)
    def _():
        for ref in (cf_ref, cb_ref, mf_ref, mb_ref):
            ref[...] = jnp.zeros(ref.shape, F32)

    h = ML_N_HEADS
    chunks = [(slice(c * SCAN_CHUNK, (c + 1) * SCAN_CHUNK),
               slice((SCAN_STEP_CHUNKS - 1 - c) * SCAN_CHUNK, (SCAN_STEP_CHUNKS - c) * SCAN_CHUNK))
              for c in range(SCAN_STEP_CHUNKS)]
    stats = [(_mlstm_stats(gf_ref[fs, :], gtf_ref[:, fs], mf_ref, i_off=0, f_off=2 * h, reverse=False),
              _mlstm_stats(gb_ref[bs, :], gtb_ref[:, bs], mb_ref, i_off=h, f_off=3 * h, reverse=True))
             for fs, bs in chunks]
    for (fs, bs), (stats_f, stats_b) in zip(chunks, stats):
        hf_ref[fs, :] = _mlstm_main(kf_ref[fs, :], ktf_ref[:, fs], qtf_ref[:, fs], vf_ref[fs, :], stats_f, cf_ref)
        hb_ref[bs, :] = _mlstm_main(kb_ref[bs, :], ktb_ref[:, bs], qtb_ref[:, bs], vb_ref[bs, :], stats_b, cb_ref)


def _mlstm(qt, k, kt, v, gates, gatest):
    b, s, n_qk = k.shape
    n_v = v.shape[2]
    n_g = gatest.shape[1]
    L = SCAN_CHUNK * SCAN_STEP_CHUNKS
    nc = s // L
    fwd = lambda w: pl.BlockSpec((None, L, w), lambda bi, i: (bi, i, 0))
    bwd = lambda w: pl.BlockSpec((None, L, w), lambda bi, i: (bi, nc - 1 - i, 0))
    fwd_t = lambda c: pl.BlockSpec((None, c, L), lambda bi, i: (bi, 0, i))
    bwd_t = lambda c: pl.BlockSpec((None, c, L), lambda bi, i: (bi, 0, nc - 1 - i))
    state = pltpu.VMEM((ML_N_HEADS // 2, 2 * ML_QK, 2 * ML_V), F32)
    stab = pltpu.VMEM((ML_N_HEADS, LANES), F32)
    return pl.pallas_call(
        _mlstm_kernel,
        out_shape=[jax.ShapeDtypeStruct((b, s, n_v), BF16)] * 2,
        grid=(b, nc),
        in_specs=[fwd(n_qk), fwd_t(n_qk), fwd_t(n_qk), fwd(n_v), fwd(LANES), fwd_t(n_g),
                  bwd(n_qk), bwd_t(n_qk), bwd_t(n_qk), bwd(n_v), bwd(LANES), bwd_t(n_g)],
        out_specs=[fwd(n_v), bwd(n_v)],
        scratch_shapes=[state, state, stab, stab],
        compiler_params=_params(("parallel", "arbitrary")),
        name="mlstm_scan",
    )(k, kt, qt, v, gates, gatest, k, kt, qt, v, gates, gatest)


def _odd_out_kernel(x_ref, hf_ref, hb_ref, o_ref, g_ref, w_ref, fg_ref, fwin_ref, fwout_ref, out_ref):
    hs = hf_ref[...].astype(F32) + hb_ref[...].astype(F32)
    normed = []
    for hd in range(ML_N_HEADS):
        seg = hs[:, hd * ML_V:(hd + 1) * ML_V]
        normed.append(seg * lax.rsqrt(jnp.mean(seg * seg, axis=-1, keepdims=True) + EPS))
    gated = (_sigmoid(o_ref[...].astype(F32)) * (jnp.concatenate(normed, axis=1) * g_ref[...])).astype(BF16)
    x = x_ref[...] + _dot(gated, w_ref[...])
    out_ref[...] = _ffn_apply(x, fg_ref, fwin_ref, fwout_ref)


def _odd_out(x, hf, hb, o, g, w, ffn, layer):
    t, d = x.shape
    n_v = hf.shape[1]
    tm = TOKEN_TILE
    tok = lambda c: pl.BlockSpec((tm, c), lambda i: (i, 0))
    return pl.pallas_call(
        _odd_out_kernel,
        out_shape=jax.ShapeDtypeStruct((t, d), F32),
        grid=(t // tm,),
        in_specs=[tok(d), tok(n_v), tok(n_v), tok(n_v), _const_spec(g.shape), _const_spec(w.shape)]
        + _layer_specs(ffn, layer),
        out_specs=tok(d),
        compiler_params=_params(("parallel",)),
        name="odd_out_ffn",
    )(x, hf, hb, o, g, w, *ffn)


def _row(v):
    return v.reshape(1, -1).astype(F32)


def _pad_lanes(v, width=LANES):
    return jnp.pad(v, [(0, 0)] * (v.ndim - 1) + [(0, width - v.shape[-1])])


def _swap_halves(v):
    half = v.shape[-1] // 2
    return jnp.concatenate([v[..., half:], v[..., :half]], axis=-1)


def _on_rope_lanes(v):
    return jnp.pad(v, [(0, 0)] * (v.ndim - 1) + [(MLA_NOPE, LANES - MLA_NOPE - v.shape[-1])])


def _prep_even(ev_w_in, ev_w_out, conv_w, conv_b, dt_bias, a_log, d_skip, ssd_norm,
               q_a_norm, w_q_b, kv_a_norm, w_kv_b, q_norm, k_norm):
    d_inner = ssd_norm.shape[0]
    d_xbc = conv_b.shape[0]
    n_heads = a_log.shape[1]
    c0, c1 = d_inner, d_inner + d_xbc
    c2 = c1 + 2 * n_heads
    c3 = c2 + MLA_Q_RANK
    c4 = c3 + MLA_KV_RANK
    w_kpe = ev_w_in[:, c4:]
    small = jnp.concatenate([_pad_lanes(ev_w_in[:, c1:c2], MLA_NOPE), _pad_lanes(w_kpe, LANES - MLA_NOPE)], axis=1)
    w_in = jnp.concatenate([ev_w_in[:, :c1], ev_w_in[:, c2:c4], small,
                            _on_rope_lanes(_swap_halves(w_kpe))], axis=1).astype(BF16)
    d_qk = MLA_NOPE + MLA_ROPE
    wq3 = w_q_b.reshape(MLA_Q_RANK, MLA_N_HEADS, d_qk)
    wq = jnp.concatenate(
        [_pad_lanes(wq3).reshape(MLA_Q_RANK, -1),
         _on_rope_lanes(_swap_halves(wq3[:, :, MLA_NOPE:])).reshape(MLA_Q_RANK, -1)], axis=1).astype(BF16)
    q_scale = d_qk ** -0.5 * LOG2_E
    score_bound = (SCORE_ROUNDING_MARGIN * d_qk * q_scale
                   * jnp.max(jnp.abs(q_norm)) * jnp.max(jnp.abs(k_norm)))
    scores_bounded = (score_bound <= SAFE_LOG2_SCORE).astype(jnp.int32).reshape(1, 1)
    ones_blk = jnp.ones((LANES, LANES), BF16)
    zero_blk = jnp.zeros((LANES, LANES), BF16)
    pair_ones = jnp.concatenate([jnp.concatenate([ones_blk, zero_blk], axis=1),
                                 jnp.concatenate([zero_blk, ones_blk], axis=1)], axis=0)
    wkv = w_kv_b.reshape(MLA_KV_RANK, MLA_N_HEADS, MLA_NOPE + MLA_V)
    wkvb = jnp.concatenate(
        [_pad_lanes(wkv[:, :, :MLA_NOPE]).reshape(MLA_KV_RANK, -1),
         wkv[:, :, MLA_NOPE:].reshape(MLA_KV_RANK, -1)], axis=1).astype(BF16)
    head_of_lane = jnp.arange(d_inner) // SSD_HEAD_DIM
    e_f = (jnp.arange(LANES)[:, None] == head_of_lane[None, :]).astype(BF16)
    e_b = (jnp.arange(LANES)[:, None] == head_of_lane[None, :] + n_heads).astype(BF16)
    return dict(
        d_inner=d_inner, d_xbc=d_xbc, w_in=w_in, w_out=ev_w_out.astype(BF16),
        conv_w=_pad_lanes(conv_w.T, SUBLANES).T.astype(F32), conv_b=_row(conv_b),
        dt_bias=_pad_lanes(_row(dt_bias)), alog_row=_pad_lanes(_row(a_log)),
        alog_col=a_log.reshape(-1, 1).astype(F32), d_skip=_row(jnp.repeat(d_skip, SSD_HEAD_DIM)),
        ssd_norm=_row(ssd_norm), e_f=e_f, e_b=e_b,
        q_a_norm=_row(q_a_norm), wq=wq, kv_a_norm=_row(kv_a_norm), wkvb=wkvb, pair_ones=pair_ones,
        scores_bounded=scores_bounded,
        q_gain_cos=_pad_lanes(_row(q_norm)) * q_scale,
        q_gain_sin=_on_rope_lanes(_swap_halves(_row(q_norm)[:, MLA_NOPE:])) * q_scale,
        k_gain_cos=_pad_lanes(_row(k_norm)),
        k_gain_sin=_on_rope_lanes(_swap_halves(_row(k_norm)[:, MLA_NOPE:])))


def _rope_tables(s):
    pos = jnp.arange(s, dtype=F32)
    inv_freq = jnp.power(ROPE_BASE, -jnp.arange(0, MLA_ROPE, 2, dtype=F32) / MLA_ROPE)
    freqs = pos[:, None] * inv_freq[None, :]
    cos, sin = jnp.cos(freqs), jnp.sin(freqs)
    tail = jnp.ones((s, LANES - MLA_NOPE - MLA_ROPE), F32)
    cos_t = jnp.concatenate([jnp.ones((s, MLA_NOPE), F32), cos, cos, tail], axis=1)
    sin_t = _on_rope_lanes(jnp.concatenate([-sin, sin], axis=1))
    return cos_t, sin_t


def _even_mixer(x, norm_g, p, ffn2, layer):
    b, s, d = x.shape
    z, xbc, dt, dtt, q, k, vt = _even_in(x, norm_g, p, *_rope_tables(s))
    yf, yb = _ssd(xbc, dt, dtt, p["alog_row"], p["alog_col"], p["d_skip"], p["e_f"], p["e_b"],
                  d_inner=p["d_inner"])
    o = _attention(p["scores_bounded"], q, k, vt)
    flat = lambda a: a.reshape(b * s, a.shape[-1])
    return _even_out(flat(x), flat(yf), flat(yb), flat(z), flat(o), p["ssd_norm"],
                     p["w_out"], ffn2, layer).reshape(b, s, d)


def _prep_odd(od_w_in, od_w_out, ig_bias, fg_bias, ml_norm):
    n_qk = ML_N_HEADS * ML_QK
    n_main = 2 * n_qk + 2 * ML_N_HEADS * ML_V
    w_in = jnp.concatenate([od_w_in[:, :n_qk] * ML_QK ** -0.5, od_w_in[:, n_qk:n_main],
                            _pad_lanes(od_w_in[:, n_main:])], axis=1).astype(BF16)
    gate_bias = _pad_lanes(_row(jnp.concatenate([ig_bias.reshape(-1), fg_bias.reshape(-1)])))
    return dict(w_in=w_in, w_out=od_w_out.astype(BF16), gate_bias=gate_bias, ml_norm=_row(ml_norm))


def _odd_mixer(x, norm_g, p, ffn2, layer):
    b, s, d = x.shape
    qt, k, kt, v, o, gates, gatest = _odd_in(x, norm_g, p["w_in"], p["gate_bias"])
    hf, hb = _mlstm(qt, k, kt, v, gates, gatest)
    flat = lambda a: a.reshape(b * s, a.shape[-1])
    return _odd_out(flat(x), flat(hf), flat(hb), flat(o), p["ml_norm"], p["w_out"], ffn2, layer).reshape(b, s, d)


def kernel(x_prompt, x_sample, ffn1_norm, ffn1_w_in, ffn1_w_out, mix_norm, ffn2_norm, ffn2_w_in, ffn2_w_out,
           ev_w_in, ev_w_out, ssd_conv_w, ssd_conv_b, ssd_dt_bias, ssd_a_log, ssd_d_skip, ssd_norm,
           mla_q_a_norm, mla_w_q_b, mla_kv_a_norm, mla_w_kv_b, mla_q_norm, mla_k_norm,
           od_w_in, od_w_out, ml_ig_bias, ml_fg_bias, ml_norm):
    depth = ffn1_norm.shape[0]
    layers = []
    for layer in range(depth):
        j = layer // 2
        if layer % 2 == 0:
            mixer = functools.partial(_even_mixer, p=_prep_even(
                ev_w_in[j], ev_w_out[j], ssd_conv_w[j], ssd_conv_b[j], ssd_dt_bias[j], ssd_a_log[j],
                ssd_d_skip[j], ssd_norm[j], mla_q_a_norm[j], mla_w_q_b[j], mla_kv_a_norm[j],
                mla_w_kv_b[j], mla_q_norm[j], mla_k_norm[j]))
        else:
            mixer = functools.partial(_odd_mixer, p=_prep_odd(
                od_w_in[j], od_w_out[j], ml_ig_bias[j], ml_fg_bias[j], ml_norm[j]))
        layers.append(dict(mixer=mixer, mix_norm=_row(mix_norm[layer])))
    ffn1 = (ffn1_norm[:, None, :].astype(F32), ffn1_w_in.astype(BF16), ffn1_w_out.astype(BF16))
    ffn2 = (ffn2_norm[:, None, :].astype(F32), ffn2_w_in.astype(BF16), ffn2_w_out.astype(BF16))

    def trunk(x):
        b, s, d = x.shape
        for layer, lp in enumerate(layers):
            x = _ffn(x.reshape(b * s, d), ffn1, layer).reshape(b, s, d)
            x = lp["mixer"](x, lp["mix_norm"], ffn2=ffn2, layer=layer)
        return x

    return trunk(x_prompt), trunk(x_sample)
```
